```python
import math
import jax, jax.numpy as jnp
from jax import lax
import numpy as np

D_MODEL = 1024
BATCH = 8
SEQ = 2048
DEPTH = 1
DEC_BATCH = 32
DEC_SEQ = 32
PAST_LEN = 4096

CHUNK = 64
Q_BLOCK = 128
D_INNER = 2 * D_MODEL
SSM_HEAD_DIM = 64
SSM_HEADS = D_INNER // SSM_HEAD_DIM
SSM_GROUPS = 8
HEADS_PER_GROUP = SSM_HEADS // SSM_GROUPS
SSM_STATE = 128
CONV_WIDTH = 4
CONV_DIM = D_INNER + 2 * SSM_GROUPS * SSM_STATE
MLA_HEADS = 16
QK_NOPE = 128
QK_ROPE = 64
V_HEAD = 128
Q_RANK = 512
KV_RANK = 512
ROPE_THETA = 10000.0
ATTN_SCALE = (QK_NOPE + QK_ROPE) ** -0.5
D_FF = 4 * D_MODEL
RMS_EPS = 1e-6
LN_EPS = 1e-5
ALPHA = (2 * DEPTH) ** 0.25
BETA = (8 * DEPTH) ** -0.25
IN_SPLITS = [2 * D_MODEL,
             2 * D_MODEL + D_INNER,
             2 * D_MODEL + D_INNER + CONV_DIM,
             2 * D_MODEL + D_INNER + CONV_DIM + SSM_HEADS,
             2 * D_MODEL + D_INNER + CONV_DIM + SSM_HEADS + Q_RANK]
IN_COLS = 2 * D_MODEL + D_INNER + CONV_DIM + SSM_HEADS + Q_RANK + KV_RANK + QK_ROPE

kernel_name = 'hybrid_ssd_mla_deepnorm_stream_step'


def _rmsnorm(t):
    tf = t.astype(jnp.float32)
    return (tf * lax.rsqrt(jnp.mean(tf * tf, axis=-1, keepdims=True) + RMS_EPS)).astype(t.dtype)


def _layernorm(t, g, b):
    tf = t.astype(jnp.float32)
    mu = jnp.mean(tf, axis=-1, keepdims=True)
    var = jnp.mean(jnp.square(tf - mu), axis=-1, keepdims=True)
    return ((tf - mu) * lax.rsqrt(var + LN_EPS)).astype(t.dtype) * g + b


def _rope(t, pos):
    inv = ROPE_THETA ** (-jnp.arange(0, QK_ROPE, 2, dtype=jnp.float32) / QK_ROPE)
    ang = pos.astype(jnp.float32)[:, None] * inv[None, :]
    cos = jnp.cos(ang)[None, :, None, :].astype(t.dtype)
    sin = jnp.sin(ang)[None, :, None, :].astype(t.dtype)
    t1, t2 = jnp.split(t, 2, axis=-1)
    return jnp.concatenate([t1 * cos - t2 * sin, t1 * sin + t2 * cos], axis=-1)


def _causal_conv(xbc, conv_state, conv_w, conv_b):
    L = xbc.shape[1]
    xp = jnp.concatenate([conv_state, xbc], axis=1)
    out = conv_b + sum(xp[:, k:k + L] * conv_w[k] for k in range(CONV_WIDTH))
    return jax.nn.silu(out), xp[:, L:]


def _ssd(xh, dt, a, bm, cm, s0, chunk):
    bsz, L = xh.shape[:2]
    nc = L // chunk

    def blocks(t):
        return jnp.moveaxis(t.reshape(bsz, nc, chunk, *t.shape[2:]), 1, 0)

    causal = jnp.tril(jnp.ones((chunk, chunk), dtype=bool))[None, :, :, None, None]

    def step(s, inp):
        xc, dtc, bc, cc = inp
        dtype = xc.dtype
        acum = jnp.cumsum((dtc * a).astype(jnp.float32), axis=1)
        seg = jnp.where(causal, acum[:, :, None] - acum[:, None, :], -jnp.inf)
        xdt = xc * dtc[..., None]
        cb = jnp.einsum('bign,bjgn->bijg', cc, bc)
        y = jnp.einsum('bijg,bijgr,bjgrp->bigrp', cb, jnp.exp(seg).astype(dtype), xdt)
        y = y + jnp.einsum('bign,bgrpn->bigrp', cc, s) * jnp.exp(acum).astype(dtype)[..., None]
        w_end = jnp.exp(acum[:, -1:] - acum).astype(dtype)
        s_new = (s * jnp.exp(acum[:, -1]).astype(dtype)[..., None, None]
                 + jnp.einsum('bjgr,bjgn,bjgrp->bgrpn', w_end, bc, xdt))
        return s_new.astype(s.dtype), y

    s_fin, ys = lax.scan(step, s0, (blocks(xh), blocks(dt), blocks(bm), blocks(cm)))
    return jnp.moveaxis(ys, 0, 1).reshape(xh.shape), s_fin


def _mla_attend(q_nope, q_pe, ckv, kpe, w_uk, w_uv, q_pos, k_pos):
    q_lat = jnp.einsum('bqhd,chd->bqhc', q_nope, w_uk)
    s = (jnp.einsum('bqhc,bkc->bhqk', q_lat, ckv)
         + jnp.einsum('bqhr,bkr->bhqk', q_pe, kpe)).astype(jnp.float32) * ATTN_SCALE
    mask = (k_pos[None, :] // CHUNK) <= (q_pos[:, None] // CHUNK)
    p = jax.nn.softmax(jnp.where(mask[None, None], s, -jnp.inf), axis=-1).astype(ckv.dtype)
    o_lat = jnp.einsum('bhqk,bkc->bqhc', p, ckv)
    return jnp.einsum('bqhc,chv->bqhv', o_lat, w_uv)


def _layer(x, pos0, lp, past):
    bsz, L, _ = x.shape
    dtype = x.dtype
    proj = x @ lp['w_in']
    gate_pre, z, xbc, dt_raw, q_a, kv_a = jnp.split(proj, IN_SPLITS, axis=-1)
    g_ssm, g_mla = jnp.split(jax.nn.sigmoid(gate_pre + lp['b_gate']), 2, axis=-1)
    pos = pos0 + jnp.arange(L, dtype=jnp.int32)

    if past is None:
        conv_state = jnp.zeros((bsz, CONV_WIDTH - 1, CONV_DIM), dtype)
        s0 = jnp.zeros((bsz, SSM_GROUPS, HEADS_PER_GROUP, SSM_HEAD_DIM, SSM_STATE), dtype)
        chunk = CHUNK
    else:
        conv_state = past['conv']
        s0 = past['ssm'].reshape(bsz, SSM_GROUPS, HEADS_PER_GROUP, SSM_HEAD_DIM, SSM_STATE)
        chunk = L
    xbc, conv_new = _causal_conv(xbc, conv_state, lp['conv_w'], lp['conv_b'])
    xs, bm, cm = jnp.split(xbc, [D_INNER, D_INNER + SSM_GROUPS * SSM_STATE], axis=-1)
    xh = xs.reshape(bsz, L, SSM_GROUPS, HEADS_PER_GROUP, SSM_HEAD_DIM)
    bm = bm.reshape(bsz, L, SSM_GROUPS, SSM_STATE)
    cm = cm.reshape(bsz, L, SSM_GROUPS, SSM_STATE)
    dt = jax.nn.softplus(dt_raw + lp['dt_bias']).reshape(bsz, L, SSM_GROUPS, HEADS_PER_GROUP)
    a = -jnp.exp(lp['a_log']).reshape(SSM_GROUPS, HEADS_PER_GROUP)
    yh, s_fin = _ssd(xh, dt, a, bm, cm, s0, chunk)
    yh = yh + xh * lp['d_skip'].reshape(SSM_GROUPS, HEADS_PER_GROUP, 1)
    yg = (yh.reshape(bsz, L, D_INNER) * jax.nn.silu(z)).reshape(bsz, L, SSM_GROUPS, D_INNER // SSM_GROUPS)
    yg = _rmsnorm(yg).reshape(bsz, L, D_INNER) * lp['ssm_norm_g']
    y_ssm = yg @ lp['w_ssm_proj']

    q = ((_rmsnorm(q_a) * lp['q_norm_g']) @ lp['w_q_b']).reshape(bsz, L, MLA_HEADS, QK_NOPE + QK_ROPE)
    q_nope, q_pe = jnp.split(q, [QK_NOPE], axis=-1)
    q_pe = _rope(q_pe, pos)
    ckv, kpe = jnp.split(kv_a, [KV_RANK], axis=-1)
    ckv = _rmsnorm(ckv) * lp['kv_norm_g']
    kpe = _rope(kpe[:, :, None, :], pos)[:, :, 0, :]
    if past is None:
        def block(i):
            q0 = i * Q_BLOCK
            return _mla_attend(lax.dynamic_slice_in_dim(q_nope, q0, Q_BLOCK, axis=1),
                               lax.dynamic_slice_in_dim(q_pe, q0, Q_BLOCK, axis=1),
                               ckv, kpe, lp['w_uk'], lp['w_uv'],
                               q0 + jnp.arange(Q_BLOCK, dtype=jnp.int32), pos)
        o = lax.map(block, jnp.arange(L // Q_BLOCK, dtype=jnp.int32))
        o = jnp.moveaxis(o, 0, 1).reshape(bsz, L, MLA_HEADS * V_HEAD)
    else:
        ckv_all = jnp.concatenate([past['ckv'], ckv], axis=1)
        kpe_all = jnp.concatenate([past['kpe'], kpe], axis=1)
        k_pos = jnp.arange(ckv_all.shape[1], dtype=jnp.int32)
        o = _mla_attend(q_nope, q_pe, ckv_all, kpe_all, lp['w_uk'], lp['w_uv'], pos, k_pos)
        o = o.reshape(bsz, L, MLA_HEADS * V_HEAD)
    y_mla = o @ lp['w_mla_proj']

    mix = (g_ssm * y_ssm + g_mla * y_mla) @ lp['w_out']
    h = _layernorm(ALPHA * x + mix, lp['ln1_g'], lp['ln1_b'])
    ff = jnp.square(jax.nn.relu(h @ lp['w_up'])) @ lp['w_down']
    y = _layernorm(ALPHA * h + ff, lp['ln2_g'], lp['ln2_b'])
    s_fin = s_fin.reshape(bsz, SSM_HEADS, SSM_HEAD_DIM, SSM_STATE)
    return y, ckv, kpe, s_fin, conv_new


def setup_inputs(seed: int = 0) -> dict:
    key = jax.random.key(seed)
    ks = jax.random.split(key, 32)
    f32 = jnp.float32
    nrm = lambda k, shape, s: jax.random.normal(k, shape, f32) * s
    dt0 = jnp.exp(jax.random.uniform(ks[8], (DEPTH, SSM_HEADS), f32) * (math.log(0.1) - math.log(0.001)) + math.log(0.001))
    return {
        'x_prompt': nrm(ks[0], (BATCH, SEQ, D_MODEL), 1.0),
        'x_sample': nrm(ks[1], (DEC_BATCH, DEC_SEQ, D_MODEL), 1.0),
        'cache_ckv': nrm(ks[2], (DEPTH, DEC_BATCH, PAST_LEN, KV_RANK), 1.0),
        'cache_kpe': nrm(ks[3], (DEPTH, DEC_BATCH, PAST_LEN, QK_ROPE), 1.0),
        'state_ssm': nrm(ks[4], (DEPTH, DEC_BATCH, SSM_HEADS, SSM_HEAD_DIM, SSM_STATE), 0.5),
        'state_conv': nrm(ks[5], (DEPTH, DEC_BATCH, CONV_WIDTH - 1, CONV_DIM), 1.0),
        'w_in': nrm(ks[6], (DEPTH, D_MODEL, IN_COLS), D_MODEL ** -0.5),
        'b_gate': nrm(ks[7], (DEPTH, 2 * D_MODEL), 0.02),
        'conv_w': nrm(ks[9], (DEPTH, CONV_WIDTH, CONV_DIM), CONV_WIDTH ** -0.5),
        'conv_b': nrm(ks[10], (DEPTH, CONV_DIM), 0.02),
        'dt_bias': dt0 + jnp.log(-jnp.expm1(-dt0)),
        'a_log': jnp.log(jax.random.uniform(ks[11], (DEPTH, SSM_HEADS), f32, 1.0, 16.0)),
        'd_skip': 1.0 + nrm(ks[12], (DEPTH, SSM_HEADS), 0.02),
        'ssm_norm_g': 1.0 + nrm(ks[13], (DEPTH, D_INNER), 0.02),
        'w_ssm_proj': nrm(ks[14], (DEPTH, D_INNER, D_MODEL), BETA * D_INNER ** -0.5),
        'q_norm_g': 1.0 + nrm(ks[15], (DEPTH, Q_RANK), 0.02),
        'w_q_b': nrm(ks[16], (DEPTH, Q_RANK, MLA_HEADS * (QK_NOPE + QK_ROPE)), Q_RANK ** -0.5),
        'kv_norm_g': 1.0 + nrm(ks[17], (DEPTH, KV_RANK), 0.02),
        'w_uk': nrm(ks[18], (DEPTH, KV_RANK, MLA_HEADS, QK_NOPE), KV_RANK ** -0.5),
        'w_uv': nrm(ks[19], (DEPTH, KV_RANK, MLA_HEADS, V_HEAD), BETA * KV_RANK ** -0.5),
        'w_mla_proj': nrm(ks[20], (DEPTH, MLA_HEADS * V_HEAD, D_MODEL), BETA * (MLA_HEADS * V_HEAD) ** -0.5),
        'w_out': nrm(ks[21], (DEPTH, D_MODEL, D_MODEL), BETA * D_MODEL ** -0.5),
        'ln1_g': 1.0 + nrm(ks[22], (DEPTH, D_MODEL), 0.02),
        'ln1_b': nrm(ks[23], (DEPTH, D_MODEL), 0.02),
        'w_up': nrm(ks[24], (DEPTH, D_MODEL, D_FF), BETA * D_MODEL ** -0.5),
        'w_down': nrm(ks[25], (DEPTH, D_FF, D_MODEL), BETA * D_FF ** -0.5),
        'ln2_g': 1.0 + nrm(ks[26], (DEPTH, D_MODEL), 0.02),
        'ln2_b': nrm(ks[27], (DEPTH, D_MODEL), 0.02),
    }


def reference(x_prompt, x_sample, cache_ckv, cache_kpe, state_ssm, state_conv,
              w_in, b_gate, conv_w, conv_b, dt_bias, a_log, d_skip, ssm_norm_g, w_ssm_proj,
              q_norm_g, w_q_b, kv_norm_g, w_uk, w_uv, w_mla_proj, w_out,
              ln1_g, ln1_b, w_up, w_down, ln2_g, ln2_b):
    yp, ys = x_prompt, x_sample
    st_p, st_s = [], []
    for i in range(DEPTH):
        lp = dict(w_in=w_in[i], b_gate=b_gate[i], conv_w=conv_w[i], conv_b=conv_b[i],
                  dt_bias=dt_bias[i], a_log=a_log[i], d_skip=d_skip[i], ssm_norm_g=ssm_norm_g[i],
                  w_ssm_proj=w_ssm_proj[i], q_norm_g=q_norm_g[i], w_q_b=w_q_b[i],
                  kv_norm_g=kv_norm_g[i], w_uk=w_uk[i], w_uv=w_uv[i], w_mla_proj=w_mla_proj[i],
                  w_out=w_out[i], ln1_g=ln1_g[i], ln1_b=ln1_b[i], w_up=w_up[i], w_down=w_down[i],
                  ln2_g=ln2_g[i], ln2_b=ln2_b[i])
        yp, ckv_p, kpe_p, ssm_p, conv_p = _layer(yp, 0, lp, None)
        past = dict(ckv=cache_ckv[i], kpe=cache_kpe[i], ssm=state_ssm[i], conv=state_conv[i])
        ys, ckv_s, kpe_s, ssm_s, conv_s = _layer(ys, PAST_LEN, lp, past)
        st_p.append((ckv_p, kpe_p, ssm_p, conv_p))
        st_s.append((ckv_s, kpe_s, ssm_s, conv_s))
    new_ckv_p = jnp.stack([t[0] for t in st_p])
    new_kpe_p = jnp.stack([t[1] for t in st_p])
    new_ssm_p = jnp.stack([t[2] for t in st_p])
    new_conv_p = jnp.stack([t[3] for t in st_p])
    new_ckv_s = jnp.stack([t[0] for t in st_s])
    new_kpe_s = jnp.stack([t[1] for t in st_s])
    new_ssm_s = jnp.stack([t[2] for t in st_s])
    new_conv_s = jnp.stack([t[3] for t in st_s])
    return (yp, ys, new_ckv_p, new_kpe_p, new_ssm_p, new_conv_p, new_ckv_s, new_kpe_s, new_ssm_s, new_conv_s)
```

```python
import functools
import math

import jax
import jax.numpy as jnp
from jax import lax
from jax.experimental import pallas as pl
from jax.experimental.pallas import tpu as pltpu

D_MODEL = 1024
D_INNER = 2 * D_MODEL
SSM_HEAD_DIM = 64
SSM_HEADS = D_INNER // SSM_HEAD_DIM
SSM_GROUPS = 8
HEADS_PER_GROUP = SSM_HEADS // SSM_GROUPS
GROUP_DIM = HEADS_PER_GROUP * SSM_HEAD_DIM
SSM_STATE = 128
CONV_WIDTH = 4
CONV_DIM = D_INNER + 2 * SSM_GROUPS * SSM_STATE
MLA_HEADS = 16
QK_NOPE = 128
QK_ROPE = 64
V_HEAD = 128
Q_RANK = 512
KV_RANK = 512
QK_LAT = KV_RANK + QK_ROPE
ROPE_THETA = 10000.0
ATTN_SCALE = (QK_NOPE + QK_ROPE) ** -0.5
D_FF = 4 * D_MODEL
RMS_EPS = 1e-6
LN_EPS = 1e-5
CHUNK = 64
CHUNK_SHIFT = 6

LANES = 128
SUBLANES = 8
VMEM_LIMIT = 56 * 1024 * 1024

SSD_Q = 128
SSD_ROWS = 256
SSD_SAMPLE_SEQS = 4
MLA_TQ = 128
MLA_TK = 512
DEC_TK = 1024
ROW_TILE = 256

BF16 = jnp.bfloat16
F32 = jnp.float32
NEG_INF = float("-inf")


def _dot(a, b):
    return jnp.dot(a, b, preferred_element_type=F32)


def _dot_nt(a, b):
    return lax.dot_general(a, b, (((1,), (1,)), ((), ())), preferred_element_type=F32)


def _rms(t):
    return t * lax.rsqrt(jnp.mean(t * t, axis=-1, keepdims=True) + RMS_EPS)


def _layernorm(t, g, b):
    mu = jnp.mean(t, axis=-1, keepdims=True)
    d = t - mu
    var = jnp.mean(d * d, axis=-1, keepdims=True)
    return d * lax.rsqrt(var + LN_EPS) * g + b


def _silu(t):
    return t * jax.nn.sigmoid(t)


def _softplus(t):
    return jnp.maximum(t, 0.0) + jnp.log(1.0 + jnp.exp(-jnp.abs(t)))


def _const_spec(shape):
    zeros = (0,) * len(shape)
    return pl.BlockSpec(shape, lambda *_: zeros, pipeline_mode=pl.Buffered(1))


def _params(n_axes):
    return pltpu.CompilerParams(dimension_semantics=("arbitrary",) * n_axes, vmem_limit_bytes=VMEM_LIMIT)


def _expand_heads(v, e_ref):
    hi = v.astype(BF16).astype(F32)
    r1 = v - hi
    mid = r1.astype(BF16).astype(F32)
    lo = r1 - mid
    lane = lax.broadcasted_iota(jnp.int32, v.shape, 1)
    packed = jnp.where(lane < 32, hi,
                       jnp.where(lane < 64, pltpu.roll(mid, 32, 1),
                                 jnp.where(lane < 96, pltpu.roll(lo, 64, 1), 0.0)))
    return _dot(packed.astype(BF16), e_ref[...])


def _ssd_block(xc, z, dt, neg_a, dskip, norm_g, e_ref, st_ref):
    q = xc.shape[0]
    row = lax.broadcasted_iota(jnp.int32, (q, q), 0)
    col = lax.broadcasted_iota(jnp.int32, (q, q), 1)
    causal = row >= col
    da = dt * neg_a
    acum = jnp.dot(causal.astype(F32), da, precision=lax.Precision.HIGHEST, preferred_element_type=F32)
    acum_t = acum.T
    dt_full = _expand_heads(dt, e_ref)
    acum_full = _expand_heads(acum, e_ref)
    decay_full = jnp.exp(acum_full)
    alast_full = acum_full[q - 1:q, :]
    xs = xc[:, :D_INNER]
    xdt = xs * dt_full
    xdt_b = xdt.astype(BF16)
    xw_b = (xdt * jnp.exp(alast_full - acum_full)).astype(BF16)
    state_decay = jnp.exp(alast_full)
    bm_t = xc[:, D_INNER:D_INNER + SSM_GROUPS * SSM_STATE].T.astype(BF16)
    cm_b = xc[:, D_INNER + SSM_GROUPS * SSM_STATE:].astype(BF16)
    lane_g = lax.broadcasted_iota(jnp.int32, (1, GROUP_DIM), 1)
    ys = []
    for g in range(SSM_GROUPS):
        gsl = slice(g * GROUP_DIM, (g + 1) * GROUP_DIM)
        nsl = slice(g * SSM_STATE, (g + 1) * SSM_STATE)
        cg = cm_b[:, nsl]
        bg_t = bm_t[nsl, :]
        cb = _dot(cg, bg_t)
        xg = xdt_b[:, gsl]
        yg = None
        for r in range(HEADS_PER_GROUP):
            h = g * HEADS_PER_GROUP + r
            seg = acum[:, h:h + 1] - acum_t[h:h + 1, :]
            mh = (cb * jnp.exp(jnp.where(causal, seg, NEG_INF))).astype(BF16)
            in_head = (lane_g >= r * SSM_HEAD_DIM) & (lane_g < (r + 1) * SSM_HEAD_DIM)
            t = _dot(mh, jnp.where(in_head, xg, jnp.zeros_like(xg)))
            yg = t if yg is None else yg + t
        st = st_ref[g]
        yg = yg + _dot(cg, st.astype(BF16)) * decay_full[:, gsl]
        st_ref[g] = st * state_decay[:, gsl] + _dot(bg_t, xw_b[:, gsl])
        ys.append(yg)
    y = jnp.concatenate(ys, axis=1) + xs * dskip
    yz = y * _silu(z)
    outs = []
    for g in range(SSM_GROUPS):
        outs.append(_rms(yz[:, g * GROUP_DIM:(g + 1) * GROUP_DIM]))
    return jnp.concatenate(outs, axis=1) * norm_g


def _conv_silu(xp_ref, rows, convw, convb):
    acc = convb
    for k in range(CONV_WIDTH):
        acc = acc + xp_ref[SUBLANES - (CONV_WIDTH - 1) + k:SUBLANES - (CONV_WIDTH - 1) + k + rows, :] * convw[k:k + 1, :]
    return _silu(acc)


def _ssd_prompt_kernel(x_ref, wz_ref, wxbc_ref, wdt_ref, wg_ref, bg_ref, convw_ref, convb_ref, dtb_ref, alog_ref,
                       dskip_ref, ng_ref, e_ref, wproj_ref,
                       a_out_ref, ssm_out_ref, conv_out_ref,
                       st_ref, xp_ref):
    c = pl.program_id(1)
    rows = x_ref.shape[0]

    @pl.when(c == 0)
    def _():
        st_ref[...] = jnp.zeros_like(st_ref)
        xp_ref[0:SUBLANES, :] = jnp.zeros((SUBLANES, CONV_DIM), F32)

    xb = x_ref[...].astype(BF16)
    z = _dot(xb, wz_ref[...])
    xp_ref[SUBLANES:SUBLANES + rows, :] = _dot(xb, wxbc_ref[...])
    dt = _softplus(_dot(xb, wdt_ref[...]) + dtb_ref[...])
    gate = jax.nn.sigmoid(_dot(xb, wg_ref[...]) + bg_ref[...])
    xc = _conv_silu(xp_ref, rows, convw_ref[...], convb_ref[...])
    xp_ref[0:SUBLANES, :] = xp_ref[rows:rows + SUBLANES, :]
    neg_a = -jnp.exp(alog_ref[...])
    yn = []
    for s in range(rows // SSD_Q):
        sl = slice(s * SSD_Q, (s + 1) * SSD_Q)
        yn.append(_ssd_block(xc[sl], z[sl], dt[sl], neg_a, dskip_ref[...], ng_ref[...], e_ref, st_ref))
    yn = jnp.concatenate(yn, axis=0).astype(BF16)
    a_out_ref[...] = gate * _dot(yn, wproj_ref[...])

    @pl.when(c == pl.num_programs(1) - 1)
    def _():
        conv_out_ref[...] = xp_ref[SUBLANES - (CONV_WIDTH - 1):SUBLANES, :]
        for g in range(SSM_GROUPS):
            ssm_out_ref[g * GROUP_DIM:(g + 1) * GROUP_DIM, :] = st_ref[g].T


def _ssd_sample_kernel(x_ref, state_ref, cstate_ref, wz_ref, wxbc_ref, wdt_ref, wg_ref, bg_ref, convw_ref, convb_ref,
                       dtb_ref, alog_ref, dskip_ref, ng_ref, e_ref, wproj_ref,
                       a_out_ref, ssm_out_ref, conv_out_ref,
                       st_ref, xp_ref, z_scr, xbc_scr, dt_scr, xcp_scr, zp_scr, dtp_scr, yn_scr):
    nseq, seq = cstate_ref.shape[0], x_ref.shape[0] // cstate_ref.shape[0]
    xb = x_ref[...].astype(BF16)
    z_scr[...] = _dot(xb, wz_ref[...])
    xbc_scr[...] = _dot(xb, wxbc_ref[...])
    dt_scr[...] = _softplus(_dot(xb, wdt_ref[...]) + dtb_ref[...])
    gate = jax.nn.sigmoid(_dot(xb, wg_ref[...]) + bg_ref[...])
    xcp_scr[...] = jnp.zeros_like(xcp_scr)
    zp_scr[...] = jnp.zeros_like(zp_scr)
    dtp_scr[...] = jnp.zeros_like(dtp_scr)
    neg_a = -jnp.exp(alog_ref[...])
    lo = SUBLANES - (CONV_WIDTH - 1)

    def body(s, carry):
        r0 = pl.multiple_of(s * seq, seq)
        xp_ref[lo:SUBLANES, :] = cstate_ref[s]
        xp_ref[SUBLANES:SUBLANES + seq, :] = xbc_scr[pl.ds(r0, seq), :]
        xcp_scr[0:seq, :] = _conv_silu(xp_ref, seq, convw_ref[...], convb_ref[...])
        conv_out_ref[s] = xp_ref[seq + lo:seq + SUBLANES, :]
        zp_scr[0:seq, :] = z_scr[pl.ds(r0, seq), :]
        dtp_scr[0:seq, :] = dt_scr[pl.ds(r0, seq), :]
        for g in range(SSM_GROUPS):
            st_ref[g] = state_ref[s, g * GROUP_DIM:(g + 1) * GROUP_DIM, :].T
        yn = _ssd_block(xcp_scr[...], zp_scr[...], dtp_scr[...], neg_a, dskip_ref[...], ng_ref[...], e_ref, st_ref)
        yn_scr[pl.ds(r0, seq), :] = yn[0:seq].astype(BF16)
        for g in range(SSM_GROUPS):
            ssm_out_ref[s, g * GROUP_DIM:(g + 1) * GROUP_DIM, :] = st_ref[g].T
        return carry

    lax.fori_loop(0, nseq, body, 0)
    a_out_ref[...] = gate * _dot(yn_scr[...], wproj_ref[...])


def _ssd_weight_specs(w):
    names = ("w_z", "w_xbc", "w_dt", "w_gs", "b_gs", "conv_w", "conv_b", "dt_bias", "a_log", "d_skip", "norm_g",
             "expand", "w_ssm_proj")
    arrs = [w[n] for n in names]
    return arrs, [_const_spec(a.shape) for a in arrs]


def _ssd_prompt(x, w):
    b, l, d = x.shape
    rows = min(SSD_ROWS, l)
    assert l % rows == 0 and rows % SSD_Q == 0
    arrs, specs = _ssd_weight_specs(w)
    return pl.pallas_call(
        _ssd_prompt_kernel,
        grid=(b, l // rows),
        in_specs=[pl.BlockSpec((None, rows, d), lambda i, c: (i, c, 0))] + specs,
        out_specs=[pl.BlockSpec((None, rows, D_MODEL), lambda i, c: (i, c, 0)),
                   pl.BlockSpec((None, D_INNER, SSM_STATE), lambda i, c: (i, 0, 0)),
                   pl.BlockSpec((None, CONV_WIDTH - 1, CONV_DIM), lambda i, c: (i, 0, 0))],
        out_shape=[jax.ShapeDtypeStruct((b, l, D_MODEL), F32),
                   jax.ShapeDtypeStruct((b, D_INNER, SSM_STATE), F32),
                   jax.ShapeDtypeStruct((b, CONV_WIDTH - 1, CONV_DIM), F32)],
        scratch_shapes=[pltpu.VMEM((SSM_GROUPS, SSM_STATE, GROUP_DIM), F32),
                        pltpu.VMEM((rows + SUBLANES, CONV_DIM), F32)],
        compiler_params=_params(2),
        name="ssd_prompt",
    )(x, *arrs)


def _ssd_sample(x, state, cstate, w):
    b, l, d = x.shape
    nseq = min(SSD_SAMPLE_SEQS, b)
    assert b % nseq == 0 and l % SUBLANES == 0 and l <= SSD_Q
    rows = nseq * l
    arrs, specs = _ssd_weight_specs(w)
    a, ssm, conv = pl.pallas_call(
        _ssd_sample_kernel,
        grid=(b // nseq,),
        in_specs=[pl.BlockSpec((rows, d), lambda i: (i, 0)),
                  pl.BlockSpec((nseq, D_INNER, SSM_STATE), lambda i: (i, 0, 0)),
                  pl.BlockSpec((nseq, CONV_WIDTH - 1, CONV_DIM), lambda i: (i, 0, 0))] + specs,
        out_specs=[pl.BlockSpec((rows, D_MODEL), lambda i: (i, 0)),
                   pl.BlockSpec((nseq, D_INNER, SSM_STATE), lambda i: (i, 0, 0)),
                   pl.BlockSpec((nseq, CONV_WIDTH - 1, CONV_DIM), lambda i: (i, 0, 0))],
        out_shape=[jax.ShapeDtypeStruct((b * l, D_MODEL), F32),
                   jax.ShapeDtypeStruct((b, D_INNER, SSM_STATE), F32),
                   jax.ShapeDtypeStruct((b, CONV_WIDTH - 1, CONV_DIM), F32)],
        scratch_shapes=[pltpu.VMEM((SSM_GROUPS, SSM_STATE, GROUP_DIM), F32),
                        pltpu.VMEM((l + SUBLANES, CONV_DIM), F32),
                        pltpu.VMEM((rows, D_INNER), F32),
                        pltpu.VMEM((rows, CONV_DIM), F32),
                        pltpu.VMEM((rows, LANES), F32),
                        pltpu.VMEM((SSD_Q, CONV_DIM), F32),
                        pltpu.VMEM((SSD_Q, D_INNER), F32),
                        pltpu.VMEM((SSD_Q, LANES), F32),
                        pltpu.VMEM((rows, D_INNER), BF16)],
        compiler_params=_params(1),
        name="ssd_sample",
    )(x.reshape(b * l, d), state.reshape(b, D_INNER, SSM_STATE), cstate, *arrs)
    return a, ssm, conv


def _mla_keys(xb, cos, sin, wkva_ref, kvg_ref):
    kva = _dot(xb, wkva_ref[...])
    ckv = _rms(kva[:, :KV_RANK]) * kvg_ref[...]
    kpe = (kva[:, KV_RANK:KV_RANK + QK_ROPE] * cos[:, :QK_ROPE]
           + kva[:, KV_RANK + LANES:KV_RANK + LANES + QK_ROPE] * sin[:, :QK_ROPE])
    return ckv, kpe


def _mla_queries(xb, cos, sin, wqa_ref, qg_ref, wqb_ref, wukt_ref, store):
    qn = (_rms(_dot(xb, wqa_ref[...])) * qg_ref[...]).astype(BF16)
    q = _dot(qn, wqb_ref[...])
    nope_w = MLA_HEADS * QK_NOPE
    rope_w = MLA_HEADS * QK_ROPE
    for p in range(MLA_HEADS // 2):
        sl = slice(nope_w + p * LANES, nope_w + (p + 1) * LANES)
        sl_sw = slice(nope_w + rope_w + p * LANES, nope_w + rope_w + (p + 1) * LANES)
        pair = (q[:, sl] * cos + q[:, sl_sw] * sin) * ATTN_SCALE
        pair_hi = pltpu.roll(pair, QK_ROPE, 1)
        for h, pe in ((2 * p, pair), (2 * p + 1, pair_hi)):
            q_lat = _dot(q[:, h * QK_NOPE:(h + 1) * QK_NOPE].astype(BF16), wukt_ref[h]) * ATTN_SCALE
            store(h, q_lat.astype(BF16), pe[:, :QK_ROPE].astype(BF16))


def _softmax_update(s, v_b, m_scr, l_scr, acc_scr):
    m_prev = m_scr[...]
    m_new = jnp.maximum(m_prev, jnp.max(s, axis=-1, keepdims=True))
    alpha = jnp.exp(m_prev - m_new)
    p = jnp.exp(s - m_new)
    l_scr[...] = alpha * l_scr[...] + jnp.sum(p, axis=-1, keepdims=True)
    acc_scr[...] = alpha * acc_scr[...] + _dot(p.astype(BF16), v_b)
    m_scr[...] = m_new


def _mla_out(o_of_head, xb, wuv_ref, wmla_ref, wg_ref, bg_ref, o_scr):
    for h in range(MLA_HEADS):
        o_scr[:, h * V_HEAD:(h + 1) * V_HEAD] = _dot(o_of_head(h), wuv_ref[h]).astype(BF16)
    gate = jax.nn.sigmoid(_dot(xb, wg_ref[...]) + bg_ref[...])
    return gate * _dot(o_scr[...], wmla_ref[...])


def _mla_prompt_kernel(x_ref, cos_ref, sin_ref, wqa_ref, qg_ref, wqb_ref, wukt_ref, wkva_ref, kvg_ref, wuv_ref,
                       wmla_ref, wg_ref, bg_ref,
                       a_out_ref, ckv_out_ref, kpe_out_ref,
                       kv_scr, q_scr, m_scr, l_scr, acc_scr, o_scr):
    i = pl.program_id(1)
    tq = x_ref.shape[0]
    rows = MLA_HEADS * tq
    tk = min(MLA_TK, kv_scr.shape[0])

    @pl.when(i == 0)
    def _():
        kv_scr[...] = jnp.zeros_like(kv_scr)

    xb = x_ref[...].astype(BF16)
    cos, sin = cos_ref[...], sin_ref[...]
    ckv, kpe = _mla_keys(xb, cos, sin, wkva_ref, kvg_ref)
    ckv_out_ref[...] = ckv
    kpe_out_ref[...] = kpe
    t0 = pl.multiple_of(i * tq, tq)
    kv_scr[pl.ds(t0, tq), 0:KV_RANK] = ckv.astype(BF16)
    kv_scr[pl.ds(t0, tq), KV_RANK:QK_LAT] = kpe.astype(BF16)

    def store(h, q_lat, q_pe):
        q_scr[h * tq:(h + 1) * tq, 0:KV_RANK] = q_lat
        q_scr[h * tq:(h + 1) * tq, KV_RANK:QK_LAT] = q_pe

    _mla_queries(xb, cos, sin, wqa_ref, qg_ref, wqb_ref, wukt_ref, store)

    m_scr[...] = jnp.full_like(m_scr, NEG_INF)
    l_scr[...] = jnp.zeros_like(l_scr)
    acc_scr[...] = jnp.zeros_like(acc_scr)
    row = lax.broadcasted_iota(jnp.int32, (rows, 1), 0)
    q_chunk = (t0 + (row & (tq - 1))) >> CHUNK_SHIFT
    k_lane = lax.broadcasted_iota(jnp.int32, (1, tk), 1)

    def body(j, carry):
        k0 = pl.multiple_of(j * tk, tk)
        kv = kv_scr[pl.ds(k0, tk), :]
        s = _dot_nt(q_scr[...], kv)
        s = jnp.where(((k0 + k_lane) >> CHUNK_SHIFT) <= q_chunk, s, NEG_INF)
        _softmax_update(s, kv[:, :KV_RANK], m_scr, l_scr, acc_scr)
        return carry

    lax.fori_loop(0, (t0 + tq + tk - 1) // tk, body, 0)
    o_lat = (acc_scr[...] / l_scr[...]).astype(BF16)
    a_out_ref[...] = _mla_out(lambda h: o_lat[h * tq:(h + 1) * tq], xb, wuv_ref, wmla_ref, wg_ref, bg_ref, o_scr)


def _mla_prompt(x, cos, sin, w):
    b, l, d = x.shape
    tq = min(MLA_TQ, l)
    assert l % tq == 0 and tq % CHUNK == 0 and (tq & (tq - 1)) == 0 and l % min(MLA_TK, l) == 0
    names = ("w_qa", "q_norm_g", "w_qb", "w_ukt", "w_kva", "kv_norm_g", "w_uv", "w_mla_proj", "w_gm", "b_gm")
    arrs = [w[n] for n in names]
    rows = MLA_HEADS * tq
    return pl.pallas_call(
        _mla_prompt_kernel,
        grid=(b, l // tq),
        in_specs=[pl.BlockSpec((None, tq, d), lambda i, c: (i, c, 0)),
                  pl.BlockSpec((tq, LANES), lambda i, c: (c, 0)),
                  pl.BlockSpec((tq, LANES), lambda i, c: (c, 0))] + [_const_spec(a.shape) for a in arrs],
        out_specs=[pl.BlockSpec((None, tq, D_MODEL), lambda i, c: (i, c, 0)),
                   pl.BlockSpec((None, tq, KV_RANK), lambda i, c: (i, c, 0)),
                   pl.BlockSpec((None, tq, QK_ROPE), lambda i, c: (i, c, 0))],
        out_shape=[jax.ShapeDtypeStruct((b, l, D_MODEL), F32),
                   jax.ShapeDtypeStruct((b, l, KV_RANK), F32),
                   jax.ShapeDtypeStruct((b, l, QK_ROPE), F32)],
        scratch_shapes=[pltpu.VMEM((l, QK_LAT), BF16),
                        pltpu.VMEM((rows, QK_LAT), BF16),
                        pltpu.VMEM((rows, 1), F32),
                        pltpu.VMEM((rows, 1), F32),
                        pltpu.VMEM((rows, KV_RANK), F32),
                        pltpu.VMEM((tq, MLA_HEADS * V_HEAD), BF16)],
        compiler_params=_params(2),
        name="mla_prompt",
    )(x, cos, sin, *arrs)


def _mla_dec_pre_kernel(x_ref, cos_ref, sin_ref, wqa_ref, qg_ref, wqb_ref, wukt_ref, wkva_ref, kvg_ref,
                        q_out_ref, ckv_out_ref, kpe_out_ref):
    xb = x_ref[...].astype(BF16)
    cos, sin = cos_ref[...], sin_ref[...]
    ckv, kpe = _mla_keys(xb, cos, sin, wkva_ref, kvg_ref)
    ckv_out_ref[...] = ckv
    kpe_out_ref[...] = kpe

    def store(h, q_lat, q_pe):
        q_out_ref[h, :, 0:KV_RANK] = q_lat
        q_out_ref[h, :, KV_RANK:QK_LAT] = q_pe

    _mla_queries(xb, cos, sin, wqa_ref, qg_ref, wqb_ref, wukt_ref, store)


def _mla_dec_attn_kernel(q_ref, cckv_ref, ckpe_ref, nckv_ref, nkpe_ref, o_ref,
                         m_scr, l_scr, acc_scr, nk_scr, *, past_len):
    j = pl.program_id(1)
    seq = q_ref.shape[1]
    rows = MLA_HEADS * seq

    @pl.when(j == 0)
    def _():
        m_scr[...] = jnp.full_like(m_scr, NEG_INF)
        l_scr[...] = jnp.zeros_like(l_scr)
        acc_scr[...] = jnp.zeros_like(acc_scr)

    q = q_ref[...].reshape(rows, QK_LAT)
    q_lat, q_pe = q[:, :KV_RANK], q[:, KV_RANK:]
    ck = cckv_ref[...].astype(BF16)
    s = _dot_nt(q_lat, ck) + _dot_nt(q_pe, ckpe_ref[...].astype(BF16))
    _softmax_update(s, ck, m_scr, l_scr, acc_scr)

    @pl.when(j == pl.num_programs(1) - 1)
    def _():
        nk_scr[...] = jnp.zeros_like(nk_scr)
        nk_scr[0:seq, 0:KV_RANK] = nckv_ref[...].astype(BF16)
        nk_scr[0:seq, KV_RANK:QK_LAT] = nkpe_ref[...].astype(BF16)
        nk = nk_scr[...]
        s2 = _dot_nt(q, nk)
        row = lax.broadcasted_iota(jnp.int32, (rows, 1), 0)
        q_chunk = (past_len + (row & (seq - 1))) >> CHUNK_SHIFT
        k_lane = lax.broadcasted_iota(jnp.int32, (1, LANES), 1)
        visible = (k_lane < seq) & (((past_len + k_lane) >> CHUNK_SHIFT) <= q_chunk)
        _softmax_update(jnp.where(visible, s2, NEG_INF), nk[:, :KV_RANK], m_scr, l_scr, acc_scr)
        o_ref[...] = (acc_scr[...] / l_scr[...]).astype(BF16).reshape(MLA_HEADS, seq, KV_RANK)


def _mla_dec_post_kernel(o_ref, x_ref, wuv_ref, wmla_ref, wg_ref, bg_ref, a_out_ref, o_scr):
    xb = x_ref[...].astype(BF16)
    a_out_ref[...] = _mla_out(lambda h: o_ref[h], xb, wuv_ref, wmla_ref, wg_ref, bg_ref, o_scr)


def _mla_sample(x, cache_ckv, cache_kpe, cos, sin, w):
    b, l, d = x.shape
    t = b * l
    past_len = cache_ckv.shape[1]
    tm = min(ROW_TILE, t)
    tk = min(DEC_TK, past_len)
    assert t % tm == 0 and past_len % tk == 0 and (l & (l - 1)) == 0 and l <= LANES and l % 16 == 0
    x2 = x.reshape(t, d)
    pre_names = ("w_qa", "q_norm_g", "w_qb", "w_ukt", "w_kva", "kv_norm_g")
    pre = [w[n] for n in pre_names]
    q, ckv, kpe = pl.pallas_call(
        _mla_dec_pre_kernel,
        grid=(t // tm,),
        in_specs=[pl.BlockSpec((tm, d), lambda i: (i, 0)),
                  pl.BlockSpec((tm, LANES), lambda i: (i, 0)),
                  pl.BlockSpec((tm, LANES), lambda i: (i, 0))] + [_const_spec(a.shape) for a in pre],
        out_specs=[pl.BlockSpec((MLA_HEADS, tm, QK_LAT), lambda i: (0, i, 0)),
                   pl.BlockSpec((tm, KV_RANK), lambda i: (i, 0)),
                   pl.BlockSpec((tm, QK_ROPE), lambda i: (i, 0))],
        out_shape=[jax.ShapeDtypeStruct((MLA_HEADS, t, QK_LAT), BF16),
                   jax.ShapeDtypeStruct((t, KV_RANK), F32),
                   jax.ShapeDtypeStruct((t, QK_ROPE), F32)],
        compiler_params=_params(1),
        name="mla_dec_pre",
    )(x2, jnp.tile(cos, (b, 1)), jnp.tile(sin, (b, 1)), *pre)
    rows = MLA_HEADS * l
    o = pl.pallas_call(
        functools.partial(_mla_dec_attn_kernel, past_len=past_len),
        grid=(b, past_len // tk),
        in_specs=[pl.BlockSpec((MLA_HEADS, l, QK_LAT), lambda i, j: (0, i, 0)),
                  pl.BlockSpec((None, tk, KV_RANK), lambda i, j: (i, j, 0)),
                  pl.BlockSpec((None, tk, QK_ROPE), lambda i, j: (i, j, 0)),
                  pl.BlockSpec((l, KV_RANK), lambda i, j: (i, 0)),
                  pl.BlockSpec((l, QK_ROPE), lambda i, j: (i, 0))],
        out_specs=pl.BlockSpec((MLA_HEADS, l, KV_RANK), lambda i, j: (0, i, 0)),
        out_shape=jax.ShapeDtypeStruct((MLA_HEADS, t, KV_RANK), BF16),
        scratch_shapes=[pltpu.VMEM((rows, 1), F32),
                        pltpu.VMEM((rows, 1), F32),
                        pltpu.VMEM((rows, KV_RANK), F32),
                        pltpu.VMEM((LANES, QK_LAT), BF16)],
        compiler_params=_params(2),
        name="mla_dec_attn",
    )(q, cache_ckv, cache_kpe, ckv, kpe)
    post_names = ("w_uv", "w_mla_proj", "w_gm", "b_gm")
    post = [w[n] for n in post_names]
    a = pl.pallas_call(
        _mla_dec_post_kernel,
        grid=(t // tm,),
        in_specs=[pl.BlockSpec((MLA_HEADS, tm, KV_RANK), lambda i: (0, i, 0)),
                  pl.BlockSpec((tm, d), lambda i: (i, 0))] + [_const_spec(a.shape) for a in post],
        out_specs=pl.BlockSpec((tm, D_MODEL), lambda i: (i, 0)),
        out_shape=jax.ShapeDtypeStruct((t, D_MODEL), F32),
        scratch_shapes=[pltpu.VMEM((tm, MLA_HEADS * V_HEAD), BF16)],
        compiler_params=_params(1),
        name="mla_dec_post",
    )(o, x2, *post)
    return a, ckv.reshape(b, l, KV_RANK), kpe.reshape(b, l, QK_ROPE)


def _ffn_kernel(x_ref, as_ref, am_ref, wout_ref, g1_ref, b1_ref, wup_ref, wdown_ref, g2_ref, b2_ref, y_ref, *, alpha):
    mix = _dot((as_ref[...] + am_ref[...]).astype(BF16), wout_ref[...])
    h = _layernorm(alpha * x_ref[...] + mix, g1_ref[...], b1_ref[...])
    up = jnp.maximum(_dot(h.astype(BF16), wup_ref[...]), 0.0)
    ff = _dot((up * up).astype(BF16), wdown_ref[...])
    y_ref[...] = _layernorm(alpha * h + ff, g2_ref[...], b2_ref[...])


def _ffn(x2, a_ssm, a_mla, w, alpha):
    t, d = x2.shape
    tm = min(ROW_TILE, t)
    assert t % tm == 0
    names = ("w_out", "ln1_g", "ln1_b", "w_up", "w_down", "ln2_g", "ln2_b")
    arrs = [w[n] for n in names]
    row_spec = pl.BlockSpec((tm, d), lambda i: (i, 0))
    return pl.pallas_call(
        functools.partial(_ffn_kernel, alpha=alpha),
        grid=(t // tm,),
        in_specs=[row_spec, row_spec, row_spec] + [_const_spec(a.shape) for a in arrs],
        out_specs=row_spec,
        out_shape=jax.ShapeDtypeStruct((t, d), F32),
        compiler_params=_params(1),
        name="ffn",
    )(x2, a_ssm, a_mla, *arrs)


def _rope_tables(pos0, n):
    inv = ROPE_THETA ** (-jnp.arange(0, QK_ROPE, 2, dtype=F32) / QK_ROPE)
    ang = (pos0 + jnp.arange(n, dtype=jnp.int32)).astype(F32)[:, None] * inv[None, :]
    cos, sin = jnp.cos(ang), jnp.sin(ang)
    cos2 = jnp.concatenate([cos, cos], axis=-1)
    sin2 = jnp.concatenate([-sin, sin], axis=-1)
    return jnp.tile(cos2, (1, LANES // QK_ROPE)), jnp.tile(sin2, (1, LANES // QK_ROPE))


def _swap_halves(t):
    half = t.shape[-1] // 2
    return jnp.concatenate([t[..., half:], t[..., :half]], axis=-1)


def _layer_weights(w_in, b_gate, conv_w, conv_b, dt_bias, a_log, d_skip, ssm_norm_g, w_ssm_proj, q_norm_g, w_q_b,
                   kv_norm_g, w_uk, w_uv, w_mla_proj, w_out, ln1_g, ln1_b, w_up, w_down, ln2_g, ln2_b):
    d = w_in.shape[0]
    o_z = 2 * D_MODEL
    o_xbc = o_z + D_INNER
    o_dt = o_xbc + CONV_DIM
    o_qa = o_dt + SSM_HEADS
    o_kva = o_qa + Q_RANK
    bf = lambda t: t.astype(BF16)
    row = lambda t: t.reshape(1, -1)
    lane_pad = lambda t: jnp.pad(t, ((0, 0), (0, LANES - t.shape[-1])))
    w_kv = w_in[:, o_kva:]
    w_kpe = w_kv[:, KV_RANK:]
    zpad = jnp.zeros((d, LANES - QK_ROPE), w_in.dtype)
    qb = w_q_b.reshape(Q_RANK, MLA_HEADS, QK_NOPE + QK_ROPE)
    qb_rope = qb[:, :, QK_NOPE:]
    k = jnp.arange(LANES)[:, None]
    c = jnp.arange(D_INNER)[None, :]
    expand = ((k < 3 * SSM_HEADS) & ((k % SSM_HEADS) == (c // SSM_HEAD_DIM))).astype(BF16)
    return dict(
        w_gs=bf(w_in[:, :D_MODEL]), b_gs=row(b_gate[:D_MODEL]),
        w_gm=bf(w_in[:, D_MODEL:o_z]), b_gm=row(b_gate[D_MODEL:]),
        w_z=bf(w_in[:, o_z:o_xbc]), w_xbc=bf(w_in[:, o_xbc:o_dt]), w_dt=bf(lane_pad(w_in[:, o_dt:o_qa])),
        conv_w=conv_w, conv_b=row(conv_b), dt_bias=lane_pad(row(dt_bias)), a_log=lane_pad(row(a_log)),
        d_skip=row(jnp.repeat(d_skip, SSM_HEAD_DIM)), norm_g=row(ssm_norm_g), expand=expand,
        w_ssm_proj=bf(w_ssm_proj),
        w_qa=bf(w_in[:, o_qa:o_kva]), q_norm_g=row(q_norm_g),
        w_qb=bf(jnp.concatenate([qb[:, :, :QK_NOPE].reshape(Q_RANK, -1), qb_rope.reshape(Q_RANK, -1),
                                 _swap_halves(qb_rope).reshape(Q_RANK, -1)], axis=1)),
        w_ukt=bf(jnp.transpose(w_uk, (1, 2, 0))),
        w_kva=bf(jnp.concatenate([w_kv[:, :KV_RANK], w_kpe, zpad, _swap_halves(w_kpe), zpad], axis=1)),
        kv_norm_g=row(kv_norm_g), w_uv=bf(jnp.transpose(w_uv, (1, 0, 2))), w_mla_proj=bf(w_mla_proj),
        w_out=bf(w_out), ln1_g=row(ln1_g), ln1_b=row(ln1_b), w_up=bf(w_up), w_down=bf(w_down),
        ln2_g=row(ln2_g), ln2_b=row(ln2_b))


def _prompt_layer(x, w, alpha):
    b, l, d = x.shape
    cos, sin = _rope_tables(0, l)
    a_ssm, ssm, conv = _ssd_prompt(x, w)
    a_mla, ckv, kpe = _mla_prompt(x, cos, sin, w)
    y = _ffn(x.reshape(b * l, d), a_ssm.reshape(b * l, d), a_mla.reshape(b * l, d), w, alpha)
    return y.reshape(b, l, d), ckv, kpe, ssm.reshape(b, SSM_HEADS, SSM_HEAD_DIM, SSM_STATE), conv


def _sample_layer(x, cache_ckv, cache_kpe, state_ssm, state_conv, w, alpha):
    b, l, d = x.shape
    cos, sin = _rope_tables(cache_ckv.shape[1], l)
    a_ssm, ssm, conv = _ssd_sample(x, state_ssm, state_conv, w)
    a_mla, ckv, kpe = _mla_sample(x, cache_ckv, cache_kpe, cos, sin, w)
    y = _ffn(x.reshape(b * l, d), a_ssm, a_mla, w, alpha)
    return y.reshape(b, l, d), ckv, kpe, ssm.reshape(b, SSM_HEADS, SSM_HEAD_DIM, SSM_STATE), conv


def kernel(x_prompt, x_sample, cache_ckv, cache_kpe, state_ssm, state_conv, w_in, b_gate, conv_w, conv_b, dt_bias, a_log, d_skip, ssm_norm_g, w_ssm_proj, q_norm_g, w_q_b, kv_norm_g, w_uk, w_uv, w_mla_proj, w_out, ln1_g, ln1_b, w_up, w_down, ln2_g, ln2_b):
    depth = w_in.shape[0]
    alpha = (2 * depth) ** 0.25
    layer_params = (w_in, b_gate, conv_w, conv_b, dt_bias, a_log, d_skip, ssm_norm_g, w_ssm_proj, q_norm_g, w_q_b,
                    kv_norm_g, w_uk, w_uv, w_mla_proj, w_out, ln1_g, ln1_b, w_up, w_down, ln2_g, ln2_b)
    yp, ys = x_prompt, x_sample
    st_p, st_s = [], []
    for i in range(depth):
        w = _layer_weights(*(p[i] for p in layer_params))
        yp, *sp = _prompt_layer(yp, w, alpha)
        ys, *ss = _sample_layer(ys, cache_ckv[i], cache_kpe[i], state_ssm[i], state_conv[i], w, alpha)
        st_p.append(sp)
        st_s.append(ss)
    stack = lambda sts, k: jnp.stack([s[k] for s in sts])
    return (yp, ys, stack(st_p, 0), stack(st_p, 1), stack(st_p, 2), stack(st_p, 3),
            stack(st_s, 0), stack(st_s, 1), stack(st_s, 2), stack(st_s, 3))
```

```python
import functools
import math

import jax
import jax.numpy as jnp
from jax import lax
from jax.experimental import pallas as pl
from jax.experimental.pallas import tpu as pltpu

D_MODEL = 1024
D_INNER = 2 * D_MODEL
SSM_HEAD_DIM = 64
SSM_HEADS = D_INNER // SSM_HEAD_DIM
SSM_GROUPS = 8
HEADS_PER_GROUP = SSM_HEADS // SSM_GROUPS
GROUP_DIM = HEADS_PER_GROUP * SSM_HEAD_DIM
SSM_STATE = 128
CONV_WIDTH = 4
CONV_DIM = D_INNER + 2 * SSM_GROUPS * SSM_STATE
MLA_HEADS = 16
QK_NOPE = 128
QK_ROPE = 64
V_HEAD = 128
Q_RANK = 512
KV_RANK = 512
QK_LAT = KV_RANK + QK_ROPE
ROPE_THETA = 10000.0
ATTN_SCALE = (QK_NOPE + QK_ROPE) ** -0.5
D_FF = 4 * D_MODEL
RMS_EPS = 1e-6
LN_EPS = 1e-5
CHUNK = 64
CHUNK_SHIFT = 6

LANES = 128
SUBLANES = 8
VMEM_LIMIT = 62 * 1024 * 1024

SSD_Q = 128
SSD_ROWS = 256
SSD_SAMPLE_SEQS = 4
MLA_TQ = 256
MLA_TK = 512
MLA_SUB_ROWS = 512
DEC_TK = 1024
ROW_TILE = 256

BF16 = jnp.bfloat16
F32 = jnp.float32
NEG_INF = float("-inf")


def _dot(a, b):
    return jnp.dot(a, b, preferred_element_type=F32)


def _dot_nt(a, b):
    return lax.dot_general(a, b, (((1,), (1,)), ((), ())), preferred_element_type=F32)


def _rms(t):
    return t * lax.rsqrt(jnp.mean(t * t, axis=-1, keepdims=True) + RMS_EPS)


def _layernorm(t, g, b):
    mu = jnp.mean(t, axis=-1, keepdims=True)
    d = t - mu
    var = jnp.mean(d * d, axis=-1, keepdims=True)
    return d * lax.rsqrt(var + LN_EPS) * g + b


def _silu(t):
    return t * jax.nn.sigmoid(t)


def _softplus(t):
    return jnp.maximum(t, 0.0) + jnp.log(1.0 + jnp.exp(-jnp.abs(t)))


def _const_spec(shape):
    zeros = (0,) * len(shape)
    return pl.BlockSpec(shape, lambda *_: zeros, pipeline_mode=pl.Buffered(1))


def _params(n_axes):
    return pltpu.CompilerParams(dimension_semantics=("arbitrary",) * n_axes, vmem_limit_bytes=VMEM_LIMIT)


def _expand_heads(v, e_ref):
    hi = v.astype(BF16).astype(F32)
    r1 = v - hi
    mid = r1.astype(BF16).astype(F32)
    lo = r1 - mid
    lane = lax.broadcasted_iota(jnp.int32, v.shape, 1)
    packed = jnp.where(lane < 32, hi,
                       jnp.where(lane < 64, pltpu.roll(mid, 32, 1),
                                 jnp.where(lane < 96, pltpu.roll(lo, 64, 1), 0.0)))
    return _dot(packed.astype(BF16), e_ref[...])


def _ssd_block(xc, z, dt, neg_a, dskip, norm_g, e_ref, st_ref):
    q = xc.shape[0]
    row = lax.broadcasted_iota(jnp.int32, (q, q), 0)
    col = lax.broadcasted_iota(jnp.int32, (q, q), 1)
    causal = row >= col
    da = dt * neg_a
    acum = jnp.dot(causal.astype(F32), da, precision=lax.Precision.HIGHEST, preferred_element_type=F32)
    acum_t = acum.T
    dt_full = _expand_heads(dt, e_ref)
    acum_full = _expand_heads(acum, e_ref)
    decay_full = jnp.exp(acum_full)
    alast_full = acum_full[q - 1:q, :]
    xs = xc[:, :D_INNER]
    xdt = xs * dt_full
    xdt_b = xdt.astype(BF16)
    xw_b = (xdt * jnp.exp(alast_full - acum_full)).astype(BF16)
    state_decay = jnp.exp(alast_full)
    bm_t = xc[:, D_INNER:D_INNER + SSM_GROUPS * SSM_STATE].T.astype(BF16)
    cm_b = xc[:, D_INNER + SSM_GROUPS * SSM_STATE:].astype(BF16)
    lane_g = lax.broadcasted_iota(jnp.int32, (1, GROUP_DIM), 1)
    ys = []
    for g in range(SSM_GROUPS):
        gsl = slice(g * GROUP_DIM, (g + 1) * GROUP_DIM)
        nsl = slice(g * SSM_STATE, (g + 1) * SSM_STATE)
        cg = cm_b[:, nsl]
        bg_t = bm_t[nsl, :]
        cb = _dot(cg, bg_t)
        xg = xdt_b[:, gsl]
        yg = None
        for r in range(HEADS_PER_GROUP):
            h = g * HEADS_PER_GROUP + r
            seg = acum[:, h:h + 1] - acum_t[h:h + 1, :]
            mh = (cb * jnp.exp(jnp.where(causal, seg, NEG_INF))).astype(BF16)
            in_head = (lane_g >= r * SSM_HEAD_DIM) & (lane_g < (r + 1) * SSM_HEAD_DIM)
            t = _dot(mh, jnp.where(in_head, xg, jnp.zeros_like(xg)))
            yg = t if yg is None else yg + t
        st = st_ref[g]
        yg = yg + _dot(cg, st.astype(BF16)) * decay_full[:, gsl]
        st_ref[g] = st * state_decay[:, gsl] + _dot(bg_t, xw_b[:, gsl])
        ys.append(yg)
    y = jnp.concatenate(ys, axis=1) + xs * dskip
    yz = y * _silu(z)
    outs = []
    for g in range(SSM_GROUPS):
        outs.append(_rms(yz[:, g * GROUP_DIM:(g + 1) * GROUP_DIM]))
    return jnp.concatenate(outs, axis=1) * norm_g


def _conv_silu(xp_ref, rows, convw, convb):
    acc = convb
    for k in range(CONV_WIDTH):
        acc = acc + xp_ref[SUBLANES - (CONV_WIDTH - 1) + k:SUBLANES - (CONV_WIDTH - 1) + k + rows, :] * convw[k:k + 1, :]
    return _silu(acc)


def _ssd_prompt_kernel(x_ref, wz_ref, wxbc_ref, wdt_ref, wg_ref, bg_ref, convw_ref, convb_ref, dtb_ref, alog_ref,
                       dskip_ref, ng_ref, e_ref, wproj_ref,
                       a_out_ref, ssm_out_ref, conv_out_ref,
                       st_ref, xp_ref):
    c = pl.program_id(1)
    rows = x_ref.shape[0]

    @pl.when(c == 0)
    def _():
        st_ref[...] = jnp.zeros_like(st_ref)
        xp_ref[0:SUBLANES, :] = jnp.zeros((SUBLANES, CONV_DIM), F32)

    xb = x_ref[...].astype(BF16)
    z = _dot(xb, wz_ref[...])
    xp_ref[SUBLANES:SUBLANES + rows, :] = _dot(xb, wxbc_ref[...])
    dt = _softplus(_dot(xb, wdt_ref[...]) + dtb_ref[...])
    gate = jax.nn.sigmoid(_dot(xb, wg_ref[...]) + bg_ref[...])
    xc = _conv_silu(xp_ref, rows, convw_ref[...], convb_ref[...])
    xp_ref[0:SUBLANES, :] = xp_ref[rows:rows + SUBLANES, :]
    neg_a = -jnp.exp(alog_ref[...])
    yn = []
    for s in range(rows // SSD_Q):
        sl = slice(s * SSD_Q, (s + 1) * SSD_Q)
        yn.append(_ssd_block(xc[sl], z[sl], dt[sl], neg_a, dskip_ref[...], ng_ref[...], e_ref, st_ref))
    yn = jnp.concatenate(yn, axis=0).astype(BF16)
    a_out_ref[...] = gate * _dot(yn, wproj_ref[...])

    @pl.when(c == pl.num_programs(1) - 1)
    def _():
        conv_out_ref[...] = xp_ref[SUBLANES - (CONV_WIDTH - 1):SUBLANES, :]
        for g in range(SSM_GROUPS):
            ssm_out_ref[g * GROUP_DIM:(g + 1) * GROUP_DIM, :] = st_ref[g].T


def _ssd_sample_kernel(x_ref, state_ref, cstate_ref, wz_ref, wxbc_ref, wdt_ref, wg_ref, bg_ref, convw_ref, convb_ref,
                       dtb_ref, alog_ref, dskip_ref, ng_ref, e_ref, wproj_ref,
                       a_out_ref, ssm_out_ref, conv_out_ref,
                       st_ref, xp_ref, z_scr, xbc_scr, dt_scr, xcp_scr, zp_scr, dtp_scr, yn_scr):
    nseq, seq = cstate_ref.shape[0], x_ref.shape[0] // cstate_ref.shape[0]
    xb = x_ref[...].astype(BF16)
    z_scr[...] = _dot(xb, wz_ref[...])
    xbc_scr[...] = _dot(xb, wxbc_ref[...])
    dt_scr[...] = _softplus(_dot(xb, wdt_ref[...]) + dtb_ref[...])
    gate = jax.nn.sigmoid(_dot(xb, wg_ref[...]) + bg_ref[...])
    xcp_scr[...] = jnp.zeros_like(xcp_scr)
    zp_scr[...] = jnp.zeros_like(zp_scr)
    dtp_scr[...] = jnp.zeros_like(dtp_scr)
    neg_a = -jnp.exp(alog_ref[...])
    lo = SUBLANES - (CONV_WIDTH - 1)

    def body(s, carry):
        r0 = pl.multiple_of(s * seq, seq)
        xp_ref[lo:SUBLANES, :] = cstate_ref[s]
        xp_ref[SUBLANES:SUBLANES + seq, :] = xbc_scr[pl.ds(r0, seq), :]
        xcp_scr[0:seq, :] = _conv_silu(xp_ref, seq, convw_ref[...], convb_ref[...])
        conv_out_ref[s] = xp_ref[seq + lo:seq + SUBLANES, :]
        zp_scr[0:seq, :] = z_scr[pl.ds(r0, seq), :]
        dtp_scr[0:seq, :] = dt_scr[pl.ds(r0, seq), :]
        for g in range(SSM_GROUPS):
            st_ref[g] = state_ref[s, g * GROUP_DIM:(g + 1) * GROUP_DIM, :].T
        yn = _ssd_block(xcp_scr[...], zp_scr[...], dtp_scr[...], neg_a, dskip_ref[...], ng_ref[...], e_ref, st_ref)
        yn_scr[pl.ds(r0, seq), :] = yn[0:seq].astype(BF16)
        for g in range(SSM_GROUPS):
            ssm_out_ref[s, g * GROUP_DIM:(g + 1) * GROUP_DIM, :] = st_ref[g].T
        return carry

    lax.fori_loop(0, nseq, body, 0)
    a_out_ref[...] = gate * _dot(yn_scr[...], wproj_ref[...])


def _ssd_weight_specs(w):
    names = ("w_z", "w_xbc", "w_dt", "w_gs", "b_gs", "conv_w", "conv_b", "dt_bias", "a_log", "d_skip", "norm_g",
             "expand", "w_ssm_proj")
    arrs = [w[n] for n in names]
    return arrs, [_const_spec(a.shape) for a in arrs]


def _ssd_prompt(x, w):
    b, l, d = x.shape
    rows = min(SSD_ROWS, l)
    assert l % rows == 0 and rows % SSD_Q == 0
    arrs, specs = _ssd_weight_specs(w)
    return pl.pallas_call(
        _ssd_prompt_kernel,
        grid=(b, l // rows),
        in_specs=[pl.BlockSpec((None, rows, d), lambda i, c: (i, c, 0))] + specs,
        out_specs=[pl.BlockSpec((None, rows, D_MODEL), lambda i, c: (i, c, 0)),
                   pl.BlockSpec((None, D_INNER, SSM_STATE), lambda i, c: (i, 0, 0)),
                   pl.BlockSpec((None, CONV_WIDTH - 1, CONV_DIM), lambda i, c: (i, 0, 0))],
        out_shape=[jax.ShapeDtypeStruct((b, l, D_MODEL), F32),
                   jax.ShapeDtypeStruct((b, D_INNER, SSM_STATE), F32),
                   jax.ShapeDtypeStruct((b, CONV_WIDTH - 1, CONV_DIM), F32)],
        scratch_shapes=[pltpu.VMEM((SSM_GROUPS, SSM_STATE, GROUP_DIM), F32),
                        pltpu.VMEM((rows + SUBLANES, CONV_DIM), F32)],
        compiler_params=_params(2),
        name="ssd_prompt",
    )(x, *arrs)


def _ssd_sample(x, state, cstate, w):
    b, l, d = x.shape
    nseq = min(SSD_SAMPLE_SEQS, b)
    assert b % nseq == 0 and l % SUBLANES == 0 and l <= SSD_Q
    rows = nseq * l
    arrs, specs = _ssd_weight_specs(w)
    a, ssm, conv = pl.pallas_call(
        _ssd_sample_kernel,
        grid=(b // nseq,),
        in_specs=[pl.BlockSpec((rows, d), lambda i: (i, 0)),
                  pl.BlockSpec((nseq, D_INNER, SSM_STATE), lambda i: (i, 0, 0)),
                  pl.BlockSpec((nseq, CONV_WIDTH - 1, CONV_DIM), lambda i: (i, 0, 0))] + specs,
        out_specs=[pl.BlockSpec((rows, D_MODEL), lambda i: (i, 0)),
                   pl.BlockSpec((nseq, D_INNER, SSM_STATE), lambda i: (i, 0, 0)),
                   pl.BlockSpec((nseq, CONV_WIDTH - 1, CONV_DIM), lambda i: (i, 0, 0))],
        out_shape=[jax.ShapeDtypeStruct((b * l, D_MODEL), F32),
                   jax.ShapeDtypeStruct((b, D_INNER, SSM_STATE), F32),
                   jax.ShapeDtypeStruct((b, CONV_WIDTH - 1, CONV_DIM), F32)],
        scratch_shapes=[pltpu.VMEM((SSM_GROUPS, SSM_STATE, GROUP_DIM), F32),
                        pltpu.VMEM((l + SUBLANES, CONV_DIM), F32),
                        pltpu.VMEM((rows, D_INNER), F32),
                        pltpu.VMEM((rows, CONV_DIM), F32),
                        pltpu.VMEM((rows, LANES), F32),
                        pltpu.VMEM((SSD_Q, CONV_DIM), F32),
                        pltpu.VMEM((SSD_Q, D_INNER), F32),
                        pltpu.VMEM((SSD_Q, LANES), F32),
                        pltpu.VMEM((rows, D_INNER), BF16)],
        compiler_params=_params(1),
        name="ssd_sample",
    )(x.reshape(b * l, d), state.reshape(b, D_INNER, SSM_STATE), cstate, *arrs)
    return a, ssm, conv


def _mla_keys(xb, cos, sin, wkva_ref, kvg_ref):
    kva = _dot(xb, wkva_ref[...])
    ckv = _rms(kva[:, :KV_RANK]) * kvg_ref[...]
    kpe = (kva[:, KV_RANK:KV_RANK + QK_ROPE] * cos[:, :QK_ROPE]
           + kva[:, KV_RANK + LANES:KV_RANK + LANES + QK_ROPE] * sin[:, :QK_ROPE])
    return ckv, kpe


def _mla_queries(xb, cos, sin, wqa_ref, qg_ref, wqb_ref, wukt_ref, store):
    qn = (_rms(_dot(xb, wqa_ref[...])) * qg_ref[...]).astype(BF16)
    q = _dot(qn, wqb_ref[...])
    nope_w = MLA_HEADS * QK_NOPE
    rope_w = MLA_HEADS * QK_ROPE
    for p in range(MLA_HEADS // 2):
        sl = slice(nope_w + p * LANES, nope_w + (p + 1) * LANES)
        sl_sw = slice(nope_w + rope_w + p * LANES, nope_w + rope_w + (p + 1) * LANES)
        pair = (q[:, sl] * cos + q[:, sl_sw] * sin) * ATTN_SCALE
        pair_hi = pltpu.roll(pair, QK_ROPE, 1)
        for h, pe in ((2 * p, pair), (2 * p + 1, pair_hi)):
            q_lat = _dot(q[:, h * QK_NOPE:(h + 1) * QK_NOPE].astype(BF16), wukt_ref[h]) * ATTN_SCALE
            store(h, q_lat.astype(BF16), pe[:, :QK_ROPE].astype(BF16))


def _lane_tile(t, width):
    return jnp.concatenate([t] * (width // LANES), axis=1) if width > LANES else t


def _softmax_update(s, v_b, m_scr, l_scr, acc_scr):
    m_prev = m_scr[...]
    m_new = jnp.maximum(m_prev, jnp.max(s, axis=-1, keepdims=True))
    alpha = jnp.exp(m_prev - m_new)
    p = jnp.exp(s - _lane_tile(m_new, s.shape[1]))
    p_lanes = p[:, 0:LANES]
    for c in range(1, s.shape[1] // LANES):
        p_lanes = p_lanes + p[:, c * LANES:(c + 1) * LANES]
    l_scr[...] = alpha * l_scr[...] + p_lanes
    acc_scr[...] = _lane_tile(alpha, acc_scr.shape[1]) * acc_scr[...] + _dot(p.astype(BF16), v_b)
    m_scr[...] = m_new


def _softmax_finish(l_scr, acc_scr):
    return acc_scr[...] / jnp.sum(l_scr[...], axis=-1, keepdims=True)


def _mla_out(o_of_head, xb, wuv_ref, wmla_ref, wg_ref, bg_ref, o_scr):
    for h in range(MLA_HEADS):
        o_scr[:, h * V_HEAD:(h + 1) * V_HEAD] = _dot(o_of_head(h), wuv_ref[h]).astype(BF16)
    gate = jax.nn.sigmoid(_dot(xb, wg_ref[...]) + bg_ref[...])
    return gate * _dot(o_scr[...], wmla_ref[...])


def _mla_prompt_kernel(x_ref, cos_ref, sin_ref, wqa_ref, qg_ref, wqb_ref, wukt_ref, wkva_ref, kvg_ref, wuv_ref,
                       wmla_ref, wg_ref, bg_ref,
                       a_out_ref, ckv_out_ref, kpe_out_ref,
                       kv_scr, q_scr, m_scr, l_scr, acc_scr, o_scr):
    i = pl.program_id(1)
    tq = x_ref.shape[0]
    rows = MLA_HEADS * tq
    tk = min(MLA_TK, kv_scr.shape[0])

    @pl.when(i == 0)
    def _():
        kv_scr[...] = jnp.zeros_like(kv_scr)

    xb = x_ref[...].astype(BF16)
    cos, sin = cos_ref[...], sin_ref[...]
    ckv, kpe = _mla_keys(xb, cos, sin, wkva_ref, kvg_ref)
    ckv_out_ref[...] = ckv
    kpe_out_ref[...] = kpe
    t0 = pl.multiple_of(i * tq, tq)
    kv_scr[pl.ds(t0, tq), 0:KV_RANK] = ckv.astype(BF16)
    kv_scr[pl.ds(t0, tq), KV_RANK:QK_LAT] = kpe.astype(BF16)

    def store(h, q_lat, q_pe):
        q_scr[h * tq:(h + 1) * tq, 0:KV_RANK] = q_lat
        q_scr[h * tq:(h + 1) * tq, KV_RANK:QK_LAT] = q_pe

    _mla_queries(xb, cos, sin, wqa_ref, qg_ref, wqb_ref, wukt_ref, store)

    m_scr[...] = jnp.full_like(m_scr, NEG_INF)
    l_scr[...] = jnp.zeros_like(l_scr)
    acc_scr[...] = jnp.zeros_like(acc_scr)
    sub = min(MLA_SUB_ROWS, rows)
    row = lax.broadcasted_iota(jnp.int32, (sub, 1), 0)
    q_chunk = (t0 + (row & (tq - 1))) >> CHUNK_SHIFT
    k_lane = lax.broadcasted_iota(jnp.int32, (1, tk), 1)

    def tile(j, masked):
        k0 = pl.multiple_of(j * tk, tk)
        kv = kv_scr[pl.ds(k0, tk), :]
        for r0 in range(0, rows, sub):
            rs = pl.ds(r0, sub)
            s = _dot_nt(q_scr[rs, :], kv)
            if masked:
                s = jnp.where(((k0 + k_lane) >> CHUNK_SHIFT) <= q_chunk, s, NEG_INF)
            _softmax_update(s, kv[:, :KV_RANK], m_scr.at[rs], l_scr.at[rs], acc_scr.at[rs])

    def body(j, carry):
        tile(j, False)
        return carry

    last = (t0 + tq - 1) // tk
    lax.fori_loop(0, last, body, 0)
    tile(last, True)
    o_lat = _softmax_finish(l_scr, acc_scr).astype(BF16)
    a_out_ref[...] = _mla_out(lambda h: o_lat[h * tq:(h + 1) * tq], xb, wuv_ref, wmla_ref, wg_ref, bg_ref, o_scr)


def _mla_prompt(x, cos, sin, w):
    b, l, d = x.shape
    tq = min(MLA_TQ, l)
    assert l % tq == 0 and tq % CHUNK == 0 and (tq & (tq - 1)) == 0
    assert l % min(MLA_TK, l) == 0 and min(MLA_TK, l) % tq == 0 and min(MLA_SUB_ROWS, MLA_HEADS * tq) % tq == 0
    names = ("w_qa", "q_norm_g", "w_qb", "w_ukt", "w_kva", "kv_norm_g", "w_uv", "w_mla_proj", "w_gm", "b_gm")
    arrs = [w[n] for n in names]
    rows = MLA_HEADS * tq
    return pl.pallas_call(
        _mla_prompt_kernel,
        grid=(b, l // tq),
        in_specs=[pl.BlockSpec((None, tq, d), lambda i, c: (i, c, 0)),
                  pl.BlockSpec((tq, LANES), lambda i, c: (c, 0)),
                  pl.BlockSpec((tq, LANES), lambda i, c: (c, 0))] + [_const_spec(a.shape) for a in arrs],
        out_specs=[pl.BlockSpec((None, tq, D_MODEL), lambda i, c: (i, c, 0)),
                   pl.BlockSpec((None, tq, KV_RANK), lambda i, c: (i, c, 0)),
                   pl.BlockSpec((None, tq, QK_ROPE), lambda i, c: (i, c, 0))],
        out_shape=[jax.ShapeDtypeStruct((b, l, D_MODEL), F32),
                   jax.ShapeDtypeStruct((b, l, KV_RANK), F32),
                   jax.ShapeDtypeStruct((b, l, QK_ROPE), F32)],
        scratch_shapes=[pltpu.VMEM((l, QK_LAT), BF16),
                        pltpu.VMEM((rows, QK_LAT), BF16),
                        pltpu.VMEM((rows, LANES), F32),
                        pltpu.VMEM((rows, LANES), F32),
                        pltpu.VMEM((rows, KV_RANK), F32),
                        pltpu.VMEM((tq, MLA_HEADS * V_HEAD), BF16)],
        compiler_params=_params(2),
        name="mla_prompt",
    )(x, cos, sin, *arrs)


def _mla_dec_pre_kernel(x_ref, cos_ref, sin_ref, wqa_ref, qg_ref, wqb_ref, wukt_ref, wkva_ref, kvg_ref,
                        q_out_ref, ckv_out_ref, kpe_out_ref):
    xb = x_ref[...].astype(BF16)
    cos, sin = cos_ref[...], sin_ref[...]
    ckv, kpe = _mla_keys(xb, cos, sin, wkva_ref, kvg_ref)
    ckv_out_ref[...] = ckv
    kpe_out_ref[...] = kpe

    def store(h, q_lat, q_pe):
        q_out_ref[h, :, 0:KV_RANK] = q_lat
        q_out_ref[h, :, KV_RANK:QK_LAT] = q_pe

    _mla_queries(xb, cos, sin, wqa_ref, qg_ref, wqb_ref, wukt_ref, store)


def _mla_dec_attn_kernel(q_ref, cckv_ref, ckpe_ref, nckv_ref, nkpe_ref, o_ref,
                         m_scr, l_scr, acc_scr, nk_scr, *, past_len):
    j = pl.program_id(1)
    seq = q_ref.shape[1]
    rows = MLA_HEADS * seq

    @pl.when(j == 0)
    def _():
        m_scr[...] = jnp.full_like(m_scr, NEG_INF)
        l_scr[...] = jnp.zeros_like(l_scr)
        acc_scr[...] = jnp.zeros_like(acc_scr)

    q = q_ref[...].reshape(rows, QK_LAT)
    q_lat, q_pe = q[:, :KV_RANK], q[:, KV_RANK:]
    ck = cckv_ref[...].astype(BF16)
    s = _dot_nt(q_lat, ck) + _dot_nt(q_pe, ckpe_ref[...].astype(BF16))
    _softmax_update(s, ck, m_scr, l_scr, acc_scr)

    @pl.when(j == pl.num_programs(1) - 1)
    def _():
        nk_scr[...] = jnp.zeros_like(nk_scr)
        nk_scr[0:seq, 0:KV_RANK] = nckv_ref[...].astype(BF16)
        nk_scr[0:seq, KV_RANK:QK_LAT] = nkpe_ref[...].astype(BF16)
        nk = nk_scr[...]
        s2 = _dot_nt(q, nk)
        row = lax.broadcasted_iota(jnp.int32, (rows, 1), 0)
        q_chunk = (past_len + (row & (seq - 1))) >> CHUNK_SHIFT
        k_lane = lax.broadcasted_iota(jnp.int32, (1, LANES), 1)
        visible = (k_lane < seq) & (((past_len + k_lane) >> CHUNK_SHIFT) <= q_chunk)
        _softmax_update(jnp.where(visible, s2, NEG_INF), nk[:, :KV_RANK], m_scr, l_scr, acc_scr)
        o_ref[...] = _softmax_finish(l_scr, acc_scr).astype(BF16).reshape(MLA_HEADS, seq, KV_RANK)


def _mla_dec_post_kernel(o_ref, x_ref, wuv_ref, wmla_ref, wg_ref, bg_ref, a_out_ref, o_scr):
    xb = x_ref[...].astype(BF16)
    a_out_ref[...] = _mla_out(lambda h: o_ref[h], xb, wuv_ref, wmla_ref, wg_ref, bg_ref, o_scr)


def _mla_sample(x, cache_ckv, cache_kpe, cos, sin, w):
    b, l, d = x.shape
    t = b * l
    past_len = cache_ckv.shape[1]
    tm = min(ROW_TILE, t)
    tk = min(DEC_TK, past_len)
    assert t % tm == 0 and past_len % tk == 0 and (l & (l - 1)) == 0 and l <= LANES and l % 16 == 0
    x2 = x.reshape(t, d)
    pre_names = ("w_qa", "q_norm_g", "w_qb", "w_ukt", "w_kva", "kv_norm_g")
    pre = [w[n] for n in pre_names]
    q, ckv, kpe = pl.pallas_call(
        _mla_dec_pre_kernel,
        grid=(t // tm,),
        in_specs=[pl.BlockSpec((tm, d), lambda i: (i, 0)),
                  pl.BlockSpec((tm, LANES), lambda i: (i, 0)),
                  pl.BlockSpec((tm, LANES), lambda i: (i, 0))] + [_const_spec(a.shape) for a in pre],
        out_specs=[pl.BlockSpec((MLA_HEADS, tm, QK_LAT), lambda i: (0, i, 0)),
                   pl.BlockSpec((tm, KV_RANK), lambda i: (i, 0)),
                   pl.BlockSpec((tm, QK_ROPE), lambda i: (i, 0))],
        out_shape=[jax.ShapeDtypeStruct((MLA_HEADS, t, QK_LAT), BF16),
                   jax.ShapeDtypeStruct((t, KV_RANK), F32),
                   jax.ShapeDtypeStruct((t, QK_ROPE), F32)],
        compiler_params=_params(1),
        name="mla_dec_pre",
    )(x2, jnp.tile(cos, (b, 1)), jnp.tile(sin, (b, 1)), *pre)
    rows = MLA_HEADS * l
    o = pl.pallas_call(
        functools.partial(_mla_dec_attn_kernel, past_len=past_len),
        grid=(b, past_len // tk),
        in_specs=[pl.BlockSpec((MLA_HEADS, l, QK_LAT), lambda i, j: (0, i, 0)),
                  pl.BlockSpec((None, tk, KV_RANK), lambda i, j: (i, j, 0)),
                  pl.BlockSpec((None, tk, QK_ROPE), lambda i, j: (i, j, 0)),
                  pl.BlockSpec((l, KV_RANK), lambda i, j: (i, 0)),
                  pl.BlockSpec((l, QK_ROPE), lambda i, j: (i, 0))],
        out_specs=pl.BlockSpec((MLA_HEADS, l, KV_RANK), lambda i, j: (0, i, 0)),
        out_shape=jax.ShapeDtypeStruct((MLA_HEADS, t, KV_RANK), BF16),
        scratch_shapes=[pltpu.VMEM((rows, LANES), F32),
                        pltpu.VMEM((rows, LANES), F32),
                        pltpu.VMEM((rows, KV_RANK), F32),
                        pltpu.VMEM((LANES, QK_LAT), BF16)],
        compiler_params=_params(2),
        name="mla_dec_attn",
    )(q, cache_ckv, cache_kpe, ckv, kpe)
    post_names = ("w_uv", "w_mla_proj", "w_gm", "b_gm")
    post = [w[n] for n in post_names]
    a = pl.pallas_call(
        _mla_dec_post_kernel,
        grid=(t // tm,),
        in_specs=[pl.BlockSpec((MLA_HEADS, tm, KV_RANK), lambda i: (0, i, 0)),
                  pl.BlockSpec((tm, d), lambda i: (i, 0))] + [_const_spec(a.shape) for a in post],
        out_specs=pl.BlockSpec((tm, D_MODEL), lambda i: (i, 0)),
        out_shape=jax.ShapeDtypeStruct((t, D_MODEL), F32),
        scratch_shapes=[pltpu.VMEM((tm, MLA_HEADS * V_HEAD), BF16)],
        compiler_params=_params(1),
        name="mla_dec_post",
    )(o, x2, *post)
    return a, ckv.reshape(b, l, KV_RANK), kpe.reshape(b, l, QK_ROPE)


def _ffn_kernel(x_ref, as_ref, am_ref, wout_ref, g1_ref, b1_ref, wup_ref, wdown_ref, g2_ref, b2_ref, y_ref, *, alpha):
    mix = _dot((as_ref[...] + am_ref[...]).astype(BF16), wout_ref[...])
    h = _layernorm(alpha * x_ref[...] + mix, g1_ref[...], b1_ref[...])
    up = jnp.maximum(_dot(h.astype(BF16), wup_ref[...]), 0.0)
    ff = _dot((up * up).astype(BF16), wdown_ref[...])
    y_ref[...] = _layernorm(alpha * h + ff, g2_ref[...], b2_ref[...])


def _ffn(x2, a_ssm, a_mla, w, alpha):
    t, d = x2.shape
    tm = min(ROW_TILE, t)
    assert t % tm == 0
    names = ("w_out", "ln1_g", "ln1_b", "w_up", "w_down", "ln2_g", "ln2_b")
    arrs = [w[n] for n in names]
    row_spec = pl.BlockSpec((tm, d), lambda i: (i, 0))
    return pl.pallas_call(
        functools.partial(_ffn_kernel, alpha=alpha),
        grid=(t // tm,),
        in_specs=[row_spec, row_spec, row_spec] + [_const_spec(a.shape) for a in arrs],
        out_specs=row_spec,
        out_shape=jax.ShapeDtypeStruct((t, d), F32),
        compiler_params=_params(1),
        name="ffn",
    )(x2, a_ssm, a_mla, *arrs)


def _rope_tables(pos0, n):
    inv = ROPE_THETA ** (-jnp.arange(0, QK_ROPE, 2, dtype=F32) / QK_ROPE)
    ang = (pos0 + jnp.arange(n, dtype=jnp.int32)).astype(F32)[:, None] * inv[None, :]
    cos, sin = jnp.cos(ang), jnp.sin(ang)
    cos2 = jnp.concatenate([cos, cos], axis=-1)
    sin2 = jnp.concatenate([-sin, sin], axis=-1)
    return jnp.tile(cos2, (1, LANES // QK_ROPE)), jnp.tile(sin2, (1, LANES // QK_ROPE))


def _swap_halves(t):
    half = t.shape[-1] // 2
    return jnp.concatenate([t[..., half:], t[..., :half]], axis=-1)


def _layer_weights(w_in, b_gate, conv_w, conv_b, dt_bias, a_log, d_skip, ssm_norm_g, w_ssm_proj, q_norm_g, w_q_b,
                   kv_norm_g, w_uk, w_uv, w_mla_proj, w_out, ln1_g, ln1_b, w_up, w_down, ln2_g, ln2_b):
    d = w_in.shape[0]
    o_z = 2 * D_MODEL
    o_xbc = o_z + D_INNER
    o_dt = o_xbc + CONV_DIM
    o_qa = o_dt + SSM_HEADS
    o_kva = o_qa + Q_RANK
    bf = lambda t: t.astype(BF16)
    row = lambda t: t.reshape(1, -1)
    lane_pad = lambda t: jnp.pad(t, ((0, 0), (0, LANES - t.shape[-1])))
    w_kv = w_in[:, o_kva:]
    w_kpe = w_kv[:, KV_RANK:]
    zpad = jnp.zeros((d, LANES - QK_ROPE), w_in.dtype)
    qb = w_q_b.reshape(Q_RANK, MLA_HEADS, QK_NOPE + QK_ROPE)
    qb_rope = qb[:, :, QK_NOPE:]
    k = jnp.arange(LANES)[:, None]
    c = jnp.arange(D_INNER)[None, :]
    expand = ((k < 3 * SSM_HEADS) & ((k % SSM_HEADS) == (c // SSM_HEAD_DIM))).astype(BF16)
    return dict(
        w_gs=bf(w_in[:, :D_MODEL]), b_gs=row(b_gate[:D_MODEL]),
        w_gm=bf(w_in[:, D_MODEL:o_z]), b_gm=row(b_gate[D_MODEL:]),
        w_z=bf(w_in[:, o_z:o_xbc]), w_xbc=bf(w_in[:, o_xbc:o_dt]), w_dt=bf(lane_pad(w_in[:, o_dt:o_qa])),
        conv_w=conv_w, conv_b=row(conv_b), dt_bias=lane_pad(row(dt_bias)), a_log=lane_pad(row(a_log)),
        d_skip=row(jnp.repeat(d_skip, SSM_HEAD_DIM)), norm_g=row(ssm_norm_g), expand=expand,
        w_ssm_proj=bf(w_ssm_proj),
        w_qa=bf(w_in[:, o_qa:o_kva]), q_norm_g=row(q_norm_g),
        w_qb=bf(jnp.concatenate([qb[:, :, :QK_NOPE].reshape(Q_RANK, -1), qb_rope.reshape(Q_RANK, -1),
                                 _swap_halves(qb_rope).reshape(Q_RANK, -1)], axis=1)),
        w_ukt=bf(jnp.transpose(w_uk, (1, 2, 0))),
        w_kva=bf(jnp.concatenate([w_kv[:, :KV_RANK], w_kpe, zpad, _swap_halves(w_kpe), zpad], axis=1)),
        kv_norm_g=row(kv_norm_g), w_uv=bf(jnp.transpose(w_uv, (1, 0, 2))), w_mla_proj=bf(w_mla_proj),
        w_out=bf(w_out), ln1_g=row(ln1_g), ln1_b=row(ln1_b), w_up=bf(w_up), w_down=bf(w_down),
        ln2_g=row(ln2_g), ln2_b=row(ln2_b))


def _prompt_layer(x, w, alpha):
    b, l, d = x.shape
    cos, sin = _rope_tables(0, l)
    a_ssm, ssm, conv = _ssd_prompt(x, w)
    a_mla, ckv, kpe = _mla_prompt(x, cos, sin, w)
    y = _ffn(x.reshape(b * l, d), a_ssm.reshape(b * l, d), a_mla.reshape(b * l, d), w, alpha)
    return y.reshape(b, l, d), ckv, kpe, ssm.reshape(b, SSM_HEADS, SSM_HEAD_DIM, SSM_STATE), conv


def _sample_layer(x, cache_ckv, cache_kpe, state_ssm, state_conv, w, alpha):
    b, l, d = x.shape
    cos, sin = _rope_tables(cache_ckv.shape[1], l)
    a_ssm, ssm, conv = _ssd_sample(x, state_ssm, state_conv, w)
    a_mla, ckv, kpe = _mla_sample(x, cache_ckv, cache_kpe, cos, sin, w)
    y = _ffn(x.reshape(b * l, d), a_ssm, a_mla, w, alpha)
    return y.reshape(b, l, d), ckv, kpe, ssm.reshape(b, SSM_HEADS, SSM_HEAD_DIM, SSM_STATE), conv


def kernel(x_prompt, x_sample, cache_ckv, cache_kpe, state_ssm, state_conv, w_in, b_gate, conv_w, conv_b, dt_bias, a_log, d_skip, ssm_norm_g, w_ssm_proj, q_norm_g, w_q_b, kv_norm_g, w_uk, w_uv, w_mla_proj, w_out, ln1_g, ln1_b, w_up, w_down, ln2_g, ln2_b):
    depth = w_in.shape[0]
    alpha = (2 * depth) ** 0.25
    layer_params = (w_in, b_gate, conv_w, conv_b, dt_bias, a_log, d_skip, ssm_norm_g, w_ssm_proj, q_norm_g, w_q_b,
                    kv_norm_g, w_uk, w_uv, w_mla_proj, w_out, ln1_g, ln1_b, w_up, w_down, ln2_g, ln2_b)
    yp, ys = x_prompt, x_sample
    st_p, st_s = [], []
    for i in range(depth):
        w = _layer_weights(*(p[i] for p in layer_params))
        yp, *sp = _prompt_layer(yp, w, alpha)
        ys, *ss = _sample_layer(ys, cache_ckv[i], cache_kpe[i], state_ssm[i], state_conv[i], w, alpha)
        st_p.append(sp)
        st_s.append(ss)
    stack = lambda sts, k: jnp.stack([s[k] for s in sts])
    return (yp, ys, stack(st_p, 0), stack(st_p, 1), stack(st_p, 2), stack(st_p, 3),
            stack(st_s, 0), stack(st_s, 1), stack(st_s, 2), stack(st_s, 3))
```

```python
import functools
import math

import jax
import jax.numpy as jnp
from jax import lax
from jax.experimental import pallas as pl
from jax.experimental.pallas import tpu as pltpu

D_MODEL = 1024
D_INNER = 2 * D_MODEL
SSM_HEAD_DIM = 64
SSM_HEADS = D_INNER // SSM_HEAD_DIM
SSM_GROUPS = 8
HEADS_PER_GROUP = SSM_HEADS // SSM_GROUPS
GROUP_DIM = HEADS_PER_GROUP * SSM_HEAD_DIM
SSM_STATE = 128
CONV_WIDTH = 4
CONV_DIM = D_INNER + 2 * SSM_GROUPS * SSM_STATE
MLA_HEADS = 16
QK_NOPE = 128
QK_ROPE = 64
V_HEAD = 128
Q_RANK = 512
KV_RANK = 512
QK_LAT = KV_RANK + QK_ROPE
ROPE_THETA = 10000.0
ATTN_SCALE = (QK_NOPE + QK_ROPE) ** -0.5
D_FF = 4 * D_MODEL
RMS_EPS = 1e-6
LN_EPS = 1e-5
CHUNK = 64
CHUNK_SHIFT = 6

LANES = 128
SUBLANES = 8
VMEM_LIMIT = 62 * 1024 * 1024

SSD_Q = 128
SSD_ROWS = 256
SSD_SAMPLE_SEQS = 4
MLA_BLK = 256
DEC_TK = 1024
ROW_TILE = 256
FFN_ROWS = 512
FFN_SUB_ROWS = 256

BF16 = jnp.bfloat16
F32 = jnp.float32
NEG_INF = float("-inf")
LOG2_E = math.log2(math.e)


def _dot(a, b):
    return jnp.dot(a, b, preferred_element_type=F32)


def _dot_nt(a, b):
    return lax.dot_general(a, b, (((1,), (1,)), ((), ())), preferred_element_type=F32)


def _rms(t):
    return t * lax.rsqrt(jnp.mean(t * t, axis=-1, keepdims=True) + RMS_EPS)


def _layernorm(t, g, b):
    mu = jnp.mean(t, axis=-1, keepdims=True)
    d = t - mu
    var = jnp.mean(d * d, axis=-1, keepdims=True)
    return d * lax.rsqrt(var + LN_EPS) * g + b


def _silu(t):
    return t * jax.nn.sigmoid(t)


def _softplus(t):
    return jnp.maximum(t, 0.0) + jnp.log(1.0 + jnp.exp(-jnp.abs(t)))


def _const_spec(shape):
    zeros = (0,) * len(shape)
    return pl.BlockSpec(shape, lambda *_: zeros, pipeline_mode=pl.Buffered(1))


def _params(n_axes):
    return pltpu.CompilerParams(dimension_semantics=("arbitrary",) * n_axes, vmem_limit_bytes=VMEM_LIMIT)


def _expand_heads(v, e_ref):
    hi = v.astype(BF16).astype(F32)
    r1 = v - hi
    mid = r1.astype(BF16).astype(F32)
    lo = r1 - mid
    lane = lax.broadcasted_iota(jnp.int32, v.shape, 1)
    packed = jnp.where(lane < 32, hi,
                       jnp.where(lane < 64, pltpu.roll(mid, 32, 1),
                                 jnp.where(lane < 96, pltpu.roll(lo, 64, 1), 0.0)))
    return _dot(packed.astype(BF16), e_ref[...])


def _ssd_block(xc, z, dt, neg_a, dskip, norm_g, e_ref, st_ref):
    q = xc.shape[0]
    row = lax.broadcasted_iota(jnp.int32, (q, q), 0)
    col = lax.broadcasted_iota(jnp.int32, (q, q), 1)
    causal = row >= col
    da = dt * neg_a
    acum = jnp.dot(causal.astype(F32), da, precision=lax.Precision.HIGHEST, preferred_element_type=F32)
    acum_t = acum.T
    dt_full = _expand_heads(dt, e_ref)
    acum_full = _expand_heads(acum, e_ref)
    decay_full = jnp.exp(acum_full)
    alast_full = acum_full[q - 1:q, :]
    xs = xc[:, :D_INNER]
    xdt = xs * dt_full
    xdt_b = xdt.astype(BF16)
    xw_b = (xdt * jnp.exp(alast_full - acum_full)).astype(BF16)
    state_decay = jnp.exp(alast_full)
    bm_t = xc[:, D_INNER:D_INNER + SSM_GROUPS * SSM_STATE].T.astype(BF16)
    cm_b = xc[:, D_INNER + SSM_GROUPS * SSM_STATE:].astype(BF16)
    lane_g = lax.broadcasted_iota(jnp.int32, (1, GROUP_DIM), 1)
    ys = []
    for g in range(SSM_GROUPS):
        gsl = slice(g * GROUP_DIM, (g + 1) * GROUP_DIM)
        nsl = slice(g * SSM_STATE, (g + 1) * SSM_STATE)
        cg = cm_b[:, nsl]
        bg_t = bm_t[nsl, :]
        cb = _dot(cg, bg_t)
        xg = xdt_b[:, gsl]
        yg = None
        for r in range(HEADS_PER_GROUP):
            h = g * HEADS_PER_GROUP + r
            seg = acum[:, h:h + 1] - acum_t[h:h + 1, :]
            mh = (cb * jnp.exp(jnp.where(causal, seg, NEG_INF))).astype(BF16)
            in_head = (lane_g >= r * SSM_HEAD_DIM) & (lane_g < (r + 1) * SSM_HEAD_DIM)
            t = _dot(mh, jnp.where(in_head, xg, jnp.zeros_like(xg)))
            yg = t if yg is None else yg + t
        st = st_ref[g]
        yg = yg + _dot(cg, st.astype(BF16)) * decay_full[:, gsl]
        st_ref[g] = st * state_decay[:, gsl] + _dot(bg_t, xw_b[:, gsl])
        ys.append(yg)
    y = jnp.concatenate(ys, axis=1) + xs * dskip
    yz = y * _silu(z)
    outs = []
    for g in range(SSM_GROUPS):
        outs.append(_rms(yz[:, g * GROUP_DIM:(g + 1) * GROUP_DIM]))
    return jnp.concatenate(outs, axis=1) * norm_g


def _conv_silu(xp_ref, rows, convw, convb):
    acc = convb
    for k in range(CONV_WIDTH):
        acc = acc + xp_ref[SUBLANES - (CONV_WIDTH - 1) + k:SUBLANES - (CONV_WIDTH - 1) + k + rows, :] * convw[k:k + 1, :]
    return _silu(acc)


def _ssd_prompt_kernel(x_ref, wz_ref, wxbc_ref, wdt_ref, wg_ref, bg_ref, convw_ref, convb_ref, dtb_ref, alog_ref,
                       dskip_ref, ng_ref, e_ref, wproj_ref,
                       a_out_ref, ssm_out_ref, conv_out_ref,
                       st_ref, xp_ref):
    c = pl.program_id(1)
    rows = x_ref.shape[0]

    @pl.when(c == 0)
    def _():
        st_ref[...] = jnp.zeros_like(st_ref)
        xp_ref[0:SUBLANES, :] = jnp.zeros((SUBLANES, CONV_DIM), F32)

    xb = x_ref[...].astype(BF16)
    z = _dot(xb, wz_ref[...])
    xp_ref[SUBLANES:SUBLANES + rows, :] = _dot(xb, wxbc_ref[...])
    dt = _softplus(_dot(xb, wdt_ref[...]) + dtb_ref[...])
    gate = jax.nn.sigmoid(_dot(xb, wg_ref[...]) + bg_ref[...])
    xc = _conv_silu(xp_ref, rows, convw_ref[...], convb_ref[...])
    xp_ref[0:SUBLANES, :] = xp_ref[rows:rows + SUBLANES, :]
    neg_a = -jnp.exp(alog_ref[...])
    yn = []
    for s in range(rows // SSD_Q):
        sl = slice(s * SSD_Q, (s + 1) * SSD_Q)
        yn.append(_ssd_block(xc[sl], z[sl], dt[sl], neg_a, dskip_ref[...], ng_ref[...], e_ref, st_ref))
    yn = jnp.concatenate(yn, axis=0).astype(BF16)
    a_out_ref[...] = gate * _dot(yn, wproj_ref[...])

    @pl.when(c == pl.num_programs(1) - 1)
    def _():
        conv_out_ref[...] = xp_ref[SUBLANES - (CONV_WIDTH - 1):SUBLANES, :]
        for g in range(SSM_GROUPS):
            ssm_out_ref[g * GROUP_DIM:(g + 1) * GROUP_DIM, :] = st_ref[g].T


def _ssd_sample_kernel(x_ref, state_ref, cstate_ref, wz_ref, wxbc_ref, wdt_ref, wg_ref, bg_ref, convw_ref, convb_ref,
                       dtb_ref, alog_ref, dskip_ref, ng_ref, e_ref, wproj_ref,
                       a_out_ref, ssm_out_ref, conv_out_ref,
                       st_ref, xp_ref, z_scr, xbc_scr, dt_scr, xcp_scr, zp_scr, dtp_scr, yn_scr):
    nseq, seq = cstate_ref.shape[0], x_ref.shape[0] // cstate_ref.shape[0]
    xb = x_ref[...].astype(BF16)
    z_scr[...] = _dot(xb, wz_ref[...])
    xbc_scr[...] = _dot(xb, wxbc_ref[...])
    dt_scr[...] = _softplus(_dot(xb, wdt_ref[...]) + dtb_ref[...])
    gate = jax.nn.sigmoid(_dot(xb, wg_ref[...]) + bg_ref[...])
    xcp_scr[...] = jnp.zeros_like(xcp_scr)
    zp_scr[...] = jnp.zeros_like(zp_scr)
    dtp_scr[...] = jnp.zeros_like(dtp_scr)
    neg_a = -jnp.exp(alog_ref[...])
    lo = SUBLANES - (CONV_WIDTH - 1)

    def body(s, carry):
        r0 = pl.multiple_of(s * seq, seq)
        xp_ref[lo:SUBLANES, :] = cstate_ref[s]
        xp_ref[SUBLANES:SUBLANES + seq, :] = xbc_scr[pl.ds(r0, seq), :]
        xcp_scr[0:seq, :] = _conv_silu(xp_ref, seq, convw_ref[...], convb_ref[...])
        conv_out_ref[s] = xp_ref[seq + lo:seq + SUBLANES, :]
        zp_scr[0:seq, :] = z_scr[pl.ds(r0, seq), :]
        dtp_scr[0:seq, :] = dt_scr[pl.ds(r0, seq), :]
        for g in range(SSM_GROUPS):
            st_ref[g] = state_ref[s, g * GROUP_DIM:(g + 1) * GROUP_DIM, :].T
        yn = _ssd_block(xcp_scr[...], zp_scr[...], dtp_scr[...], neg_a, dskip_ref[...], ng_ref[...], e_ref, st_ref)
        yn_scr[pl.ds(r0, seq), :] = yn[0:seq].astype(BF16)
        for g in range(SSM_GROUPS):
            ssm_out_ref[s, g * GROUP_DIM:(g + 1) * GROUP_DIM, :] = st_ref[g].T
        return carry

    lax.fori_loop(0, nseq, body, 0)
    a_out_ref[...] = gate * _dot(yn_scr[...], wproj_ref[...])


def _ssd_weight_specs(w):
    names = ("w_z", "w_xbc", "w_dt", "w_gs", "b_gs", "conv_w", "conv_b", "dt_bias", "a_log", "d_skip", "norm_g",
             "expand", "w_ssm_proj")
    arrs = [w[n] for n in names]
    return arrs, [_const_spec(a.shape) for a in arrs]


def _ssd_prompt(x, w):
    b, l, d = x.shape
    rows = min(SSD_ROWS, l)
    assert l % rows == 0 and rows % SSD_Q == 0
    arrs, specs = _ssd_weight_specs(w)
    return pl.pallas_call(
        _ssd_prompt_kernel,
        grid=(b, l // rows),
        in_specs=[pl.BlockSpec((None, rows, d), lambda i, c: (i, c, 0))] + specs,
        out_specs=[pl.BlockSpec((None, rows, D_MODEL), lambda i, c: (i, c, 0)),
                   pl.BlockSpec((None, D_INNER, SSM_STATE), lambda i, c: (i, 0, 0)),
                   pl.BlockSpec((None, CONV_WIDTH - 1, CONV_DIM), lambda i, c: (i, 0, 0))],
        out_shape=[jax.ShapeDtypeStruct((b, l, D_MODEL), F32),
                   jax.ShapeDtypeStruct((b, D_INNER, SSM_STATE), F32),
                   jax.ShapeDtypeStruct((b, CONV_WIDTH - 1, CONV_DIM), F32)],
        scratch_shapes=[pltpu.VMEM((SSM_GROUPS, SSM_STATE, GROUP_DIM), F32),
                        pltpu.VMEM((rows + SUBLANES, CONV_DIM), F32)],
        compiler_params=_params(2),
        name="ssd_prompt",
    )(x, *arrs)


def _ssd_sample(x, state, cstate, w):
    b, l, d = x.shape
    nseq = min(SSD_SAMPLE_SEQS, b)
    assert b % nseq == 0 and l % SUBLANES == 0 and l <= SSD_Q
    rows = nseq * l
    arrs, specs = _ssd_weight_specs(w)
    a, ssm, conv = pl.pallas_call(
        _ssd_sample_kernel,
        grid=(b // nseq,),
        in_specs=[pl.BlockSpec((rows, d), lambda i: (i, 0)),
                  pl.BlockSpec((nseq, D_INNER, SSM_STATE), lambda i: (i, 0, 0)),
                  pl.BlockSpec((nseq, CONV_WIDTH - 1, CONV_DIM), lambda i: (i, 0, 0))] + specs,
        out_specs=[pl.BlockSpec((rows, D_MODEL), lambda i: (i, 0)),
                   pl.BlockSpec((nseq, D_INNER, SSM_STATE), lambda i: (i, 0, 0)),
                   pl.BlockSpec((nseq, CONV_WIDTH - 1, CONV_DIM), lambda i: (i, 0, 0))],
        out_shape=[jax.ShapeDtypeStruct((b * l, D_MODEL), F32),
                   jax.ShapeDtypeStruct((b, D_INNER, SSM_STATE), F32),
                   jax.ShapeDtypeStruct((b, CONV_WIDTH - 1, CONV_DIM), F32)],
        scratch_shapes=[pltpu.VMEM((SSM_GROUPS, SSM_STATE, GROUP_DIM), F32),
                        pltpu.VMEM((l + SUBLANES, CONV_DIM), F32),
                        pltpu.VMEM((rows, D_INNER), F32),
                        pltpu.VMEM((rows, CONV_DIM), F32),
                        pltpu.VMEM((rows, LANES), F32),
                        pltpu.VMEM((SSD_Q, CONV_DIM), F32),
                        pltpu.VMEM((SSD_Q, D_INNER), F32),
                        pltpu.VMEM((SSD_Q, LANES), F32),
                        pltpu.VMEM((rows, D_INNER), BF16)],
        compiler_params=_params(1),
        name="ssd_sample",
    )(x.reshape(b * l, d), state.reshape(b, D_INNER, SSM_STATE), cstate, *arrs)
    return a, ssm, conv


def _mla_keys(xb, cos, sin, wkva_ref, kvg_ref):
    kva = _dot(xb, wkva_ref[...])
    ckv = _rms(kva[:, :KV_RANK]) * kvg_ref[...]
    kpe = (kva[:, KV_RANK:KV_RANK + QK_ROPE] * cos[:, :QK_ROPE]
           + kva[:, KV_RANK + LANES:KV_RANK + LANES + QK_ROPE] * sin[:, :QK_ROPE])
    return ckv, kpe


def _mla_queries(xb, cos, sin, wqa_ref, qg_ref, wqb_ref, wukt_ref, store):
    qn = (_rms(_dot(xb, wqa_ref[...])) * qg_ref[...]).astype(BF16)
    q = _dot(qn, wqb_ref[...])
    nope_w = MLA_HEADS * QK_NOPE
    rope_w = MLA_HEADS * QK_ROPE
    for p in range(MLA_HEADS // 2):
        sl = slice(nope_w + p * LANES, nope_w + (p + 1) * LANES)
        sl_sw = slice(nope_w + rope_w + p * LANES, nope_w + rope_w + (p + 1) * LANES)
        pair = (q[:, sl] * cos + q[:, sl_sw] * sin) * ATTN_SCALE
        pair_hi = pltpu.roll(pair, QK_ROPE, 1)
        for h, pe in ((2 * p, pair), (2 * p + 1, pair_hi)):
            q_lat = _dot(q[:, h * QK_NOPE:(h + 1) * QK_NOPE].astype(BF16), wukt_ref[h]) * ATTN_SCALE
            store(h, q_lat.astype(BF16), pe[:, :QK_ROPE].astype(BF16))


def _lane_tile(t, width):
    return jnp.concatenate([t] * (width // LANES), axis=1) if width > LANES else t


def _softmax_update(s, v_b, m_scr, l_scr, acc_scr):
    m_prev = m_scr[...]
    m_new = jnp.maximum(m_prev, jnp.max(s, axis=-1, keepdims=True))
    alpha = jnp.exp(m_prev - m_new)
    p = jnp.exp(s - _lane_tile(m_new, s.shape[1]))
    p_lanes = p[:, 0:LANES]
    for c in range(1, s.shape[1] // LANES):
        p_lanes = p_lanes + p[:, c * LANES:(c + 1) * LANES]
    l_scr[...] = alpha * l_scr[...] + p_lanes
    acc_scr[...] = _lane_tile(alpha, acc_scr.shape[1]) * acc_scr[...] + _dot(p.astype(BF16), v_b)
    m_scr[...] = m_new


def _softmax_finish(l_scr, acc_scr):
    return acc_scr[...] / jnp.sum(l_scr[...], axis=-1, keepdims=True)


def _mla_out(o_of_head, xb, wuv_ref, wmla_ref, wg_ref, bg_ref, o_scr):
    for h in range(MLA_HEADS):
        o_scr[:, h * V_HEAD:(h + 1) * V_HEAD] = _dot(o_of_head(h), wuv_ref[h]).astype(BF16)
    gate = jax.nn.sigmoid(_dot(xb, wg_ref[...]) + bg_ref[...])
    return gate * _dot(o_scr[...], wmla_ref[...])


def _mla_pre_kernel(x_ref, cos_ref, sin_ref, wqa_ref, qg_ref, wkva_ref, kvg_ref, qn_out_ref, ckv_out_ref, kpe_out_ref):
    xb = x_ref[...].astype(BF16)
    ckv, kpe = _mla_keys(xb, cos_ref[...], sin_ref[...], wkva_ref, kvg_ref)
    ckv_out_ref[...] = ckv
    kpe_out_ref[...] = kpe
    qn_out_ref[...] = (_rms(_dot(xb, wqa_ref[...])) * qg_ref[...]).astype(BF16)


def _mla_prompt_attn_kernel(qn_ref, ckv_ref, kpe_ref, cos_ref, sin_ref, wqp_ref, wkv_ref, o_ref,
                            qa_scr, qb_scr, ka_scr, kb_scr, vbd_scr, m_scr, l_scr, acc_scr):
    l = qn_ref.shape[0]
    g = min(MLA_BLK, l)
    nb = l // g
    half = QK_ROPE
    zpad = jnp.zeros((l, LANES - half), BF16)
    kv = _dot(ckv_ref[...].astype(BF16), wkv_ref[...])
    kpe_b = kpe_ref[...].astype(BF16)
    for scr, c0 in ((ka_scr, 0), (kb_scr, QK_NOPE)):
        scr[:, 0:QK_NOPE] = kv[:, c0:c0 + QK_NOPE].astype(BF16)
        scr[:, QK_NOPE:QK_NOPE + half] = kpe_b
        scr[:, QK_NOPE + half:] = zpad
    zv = jnp.zeros((g, V_HEAD), BF16)
    for j in range(nb):
        rs = slice(j * g, (j + 1) * g)
        vbd_scr[j, 0:g, 0:V_HEAD] = kv[rs, 2 * QK_NOPE:2 * QK_NOPE + V_HEAD].astype(BF16)
        vbd_scr[j, 0:g, V_HEAD:] = zv
        vbd_scr[j, g:, 0:V_HEAD] = zv
        vbd_scr[j, g:, V_HEAD:] = kv[rs, 2 * QK_NOPE + V_HEAD:].astype(BF16)
    q = _dot(qn_ref[...], wqp_ref[...])
    c = ATTN_SCALE * LOG2_E
    pair = (q[:, 2 * QK_NOPE:3 * QK_NOPE] * cos_ref[...] + q[:, 3 * QK_NOPE:] * sin_ref[...]) * c
    pair_hi = pltpu.roll(pair, half, 1)
    for scr, c0, pe in ((qa_scr, 0, pair), (qb_scr, QK_NOPE, pair_hi)):
        scr[:, 0:QK_NOPE] = (q[:, c0:c0 + QK_NOPE] * c).astype(BF16)
        scr[:, QK_NOPE:QK_NOPE + half] = pe[:, :half].astype(BF16)
        scr[:, QK_NOPE + half:] = zpad
    row = lax.broadcasted_iota(jnp.int32, (g, g), 0)
    col = lax.broadcasted_iota(jnp.int32, (g, g), 1)
    diag_visible = (row >> CHUNK_SHIFT) >= (col >> CHUNK_SHIFT)
    for j in range(nb):
        r0 = j * g
        ps, alphas = [], []
        for hi, (q_scr, k_scr) in enumerate(((qa_scr, ka_scr), (qb_scr, kb_scr))):
            s = _dot_nt(q_scr[r0:, :], k_scr[r0:r0 + g, :])
            top = jnp.where(diag_visible, s[:g], NEG_INF)
            s = top if j == nb - 1 else jnp.concatenate([top, s[g:]], axis=0)
            m_cur = jnp.max(s, axis=-1, keepdims=True)
            if j == 0:
                m_new = jnp.broadcast_to(m_cur, (l, LANES))
            else:
                m_prev = m_scr[hi, r0:, :]
                m_new = jnp.maximum(m_prev, m_cur)
                alphas.append(jnp.exp2(m_prev - m_new))
            p = jnp.exp2(s - _lane_tile(m_new, g))
            p_lanes = p[:, 0:LANES]
            for cc in range(1, g // LANES):
                p_lanes = p_lanes + p[:, cc * LANES:(cc + 1) * LANES]
            l_scr[hi, r0:, :] = p_lanes if j == 0 else alphas[-1] * l_scr[hi, r0:, :] + p_lanes
            m_scr[hi, r0:, :] = m_new
            ps.append(p.astype(BF16))
        pv = _dot(jnp.concatenate(ps, axis=1), vbd_scr[j])
        if j == 0:
            acc_scr[...] = pv
        else:
            acc_scr[r0:, :] = jnp.concatenate(alphas, axis=1) * acc_scr[r0:, :] + pv
    inv = [1.0 / jnp.sum(l_scr[hi], axis=-1, keepdims=True) for hi in range(2)]
    scale = jnp.concatenate([jnp.broadcast_to(t, (l, V_HEAD)) for t in inv], axis=1)
    o_ref[...] = (acc_scr[...] * scale).astype(BF16)


def _mla_post_kernel(o_ref, x_ref, wmla_ref, wg_ref, bg_ref, a_out_ref):
    gate = jax.nn.sigmoid(_dot(x_ref[...].astype(BF16), wg_ref[...]) + bg_ref[...])
    a_out_ref[...] = gate * _dot(o_ref[...], wmla_ref[...])


def _mla_prompt(x, cos, sin, w):
    b, l, d = x.shape
    g = min(MLA_BLK, l)
    tm = min(FFN_ROWS, l)
    npair = MLA_HEADS // 2
    assert l % g == 0 and g % CHUNK == 0 and g % LANES == 0 and l % tm == 0
    nt = l // tm
    pre = [w[n] for n in ("w_qa", "q_norm_g", "w_kva", "kv_norm_g")]
    qn, ckv, kpe = pl.pallas_call(
        _mla_pre_kernel,
        grid=(b, nt),
        in_specs=[pl.BlockSpec((None, tm, d), lambda i, c: (i, c, 0)),
                  pl.BlockSpec((tm, LANES), lambda i, c: (c, 0)),
                  pl.BlockSpec((tm, LANES), lambda i, c: (c, 0))] + [_const_spec(a.shape) for a in pre],
        out_specs=[pl.BlockSpec((None, tm, Q_RANK), lambda i, c: (i, c, 0)),
                   pl.BlockSpec((None, tm, KV_RANK), lambda i, c: (i, c, 0)),
                   pl.BlockSpec((None, tm, QK_ROPE), lambda i, c: (i, c, 0))],
        out_shape=[jax.ShapeDtypeStruct((b, l, Q_RANK), BF16),
                   jax.ShapeDtypeStruct((b, l, KV_RANK), F32),
                   jax.ShapeDtypeStruct((b, l, QK_ROPE), F32)],
        compiler_params=_params(2),
        name="mla_pre",
    )(x, cos, sin, *pre)
    o = pl.pallas_call(
        _mla_prompt_attn_kernel,
        grid=(b, npair),
        in_specs=[pl.BlockSpec((None, l, Q_RANK), lambda i, p: (i, 0, 0)),
                  pl.BlockSpec((None, l, KV_RANK), lambda i, p: (i, 0, 0)),
                  pl.BlockSpec((None, l, QK_ROPE), lambda i, p: (i, 0, 0)),
                  _const_spec(cos.shape), _const_spec(sin.shape),
                  pl.BlockSpec((None, Q_RANK, 4 * QK_NOPE), lambda i, p: (p, 0, 0)),
                  pl.BlockSpec((None, KV_RANK, 2 * QK_NOPE + 2 * V_HEAD), lambda i, p: (p, 0, 0))],
        out_specs=pl.BlockSpec((None, l, 2 * V_HEAD), lambda i, p: (i, 0, p)),
        out_shape=jax.ShapeDtypeStruct((b, l, MLA_HEADS * V_HEAD), BF16),
        scratch_shapes=[pltpu.VMEM((l, 2 * LANES), BF16)] * 4
        + [pltpu.VMEM((l // g, 2 * g, 2 * V_HEAD), BF16),
           pltpu.VMEM((2, l, LANES), F32),
           pltpu.VMEM((2, l, LANES), F32),
           pltpu.VMEM((l, 2 * V_HEAD), F32)],
        compiler_params=_params(2),
        name="mla_prompt_attn",
    )(qn, ckv, kpe, cos, sin, w["w_qpair"], w["w_kvpair"])
    post = [w[n] for n in ("w_mla_proj", "w_gm", "b_gm")]
    a = pl.pallas_call(
        _mla_post_kernel,
        grid=(b, nt),
        in_specs=[pl.BlockSpec((None, tm, MLA_HEADS * V_HEAD), lambda i, c: (i, c, 0)),
                  pl.BlockSpec((None, tm, d), lambda i, c: (i, c, 0))] + [_const_spec(t.shape) for t in post],
        out_specs=pl.BlockSpec((None, tm, D_MODEL), lambda i, c: (i, c, 0)),
        out_shape=jax.ShapeDtypeStruct((b, l, D_MODEL), F32),
        compiler_params=_params(2),
        name="mla_post",
    )(o, x, *post)
    return a, ckv, kpe


def _mla_dec_pre_kernel(x_ref, cos_ref, sin_ref, wqa_ref, qg_ref, wqb_ref, wukt_ref, wkva_ref, kvg_ref,
                        q_out_ref, ckv_out_ref, kpe_out_ref):
    xb = x_ref[...].astype(BF16)
    cos, sin = cos_ref[...], sin_ref[...]
    ckv, kpe = _mla_keys(xb, cos, sin, wkva_ref, kvg_ref)
    ckv_out_ref[...] = ckv
    kpe_out_ref[...] = kpe

    def store(h, q_lat, q_pe):
        q_out_ref[h, :, 0:KV_RANK] = q_lat
        q_out_ref[h, :, KV_RANK:QK_LAT] = q_pe

    _mla_queries(xb, cos, sin, wqa_ref, qg_ref, wqb_ref, wukt_ref, store)


def _mla_dec_attn_kernel(q_ref, cckv_ref, ckpe_ref, nckv_ref, nkpe_ref, o_ref,
                         m_scr, l_scr, acc_scr, nk_scr, *, past_len):
    j = pl.program_id(1)
    seq = q_ref.shape[1]
    rows = MLA_HEADS * seq

    @pl.when(j == 0)
    def _():
        m_scr[...] = jnp.full_like(m_scr, NEG_INF)
        l_scr[...] = jnp.zeros_like(l_scr)
        acc_scr[...] = jnp.zeros_like(acc_scr)

    q = q_ref[...].reshape(rows, QK_LAT)
    q_lat, q_pe = q[:, :KV_RANK], q[:, KV_RANK:]
    ck = cckv_ref[...].astype(BF16)
    s = _dot_nt(q_lat, ck) + _dot_nt(q_pe, ckpe_ref[...].astype(BF16))
    _softmax_update(s, ck, m_scr, l_scr, acc_scr)

    @pl.when(j == pl.num_programs(1) - 1)
    def _():
        nk_scr[...] = jnp.zeros_like(nk_scr)
        nk_scr[0:seq, 0:KV_RANK] = nckv_ref[...].astype(BF16)
        nk_scr[0:seq, KV_RANK:QK_LAT] = nkpe_ref[...].astype(BF16)
        nk = nk_scr[...]
        s2 = _dot_nt(q, nk)
        row = lax.broadcasted_iota(jnp.int32, (rows, 1), 0)
        q_chunk = (past_len + (row & (seq - 1))) >> CHUNK_SHIFT
        k_lane = lax.broadcasted_iota(jnp.int32, (1, LANES), 1)
        visible = (k_lane < seq) & (((past_len + k_lane) >> CHUNK_SHIFT) <= q_chunk)
        _softmax_update(jnp.where(visible, s2, NEG_INF), nk[:, :KV_RANK], m_scr, l_scr, acc_scr)
        o_ref[...] = _softmax_finish(l_scr, acc_scr).astype(BF16).reshape(MLA_HEADS, seq, KV_RANK)


def _mla_dec_post_kernel(o_ref, x_ref, wuv_ref, wmla_ref, wg_ref, bg_ref, a_out_ref, o_scr):
    xb = x_ref[...].astype(BF16)
    a_out_ref[...] = _mla_out(lambda h: o_ref[h], xb, wuv_ref, wmla_ref, wg_ref, bg_ref, o_scr)


def _mla_sample(x, cache_ckv, cache_kpe, cos, sin, w):
    b, l, d = x.shape
    t = b * l
    past_len = cache_ckv.shape[1]
    tm = min(ROW_TILE, t)
    tk = min(DEC_TK, past_len)
    assert t % tm == 0 and past_len % tk == 0 and (l & (l - 1)) == 0 and l <= LANES and l % 16 == 0
    x2 = x.reshape(t, d)
    pre_names = ("w_qa", "q_norm_g", "w_qb", "w_ukt", "w_kva", "kv_norm_g")
    pre = [w[n] for n in pre_names]
    q, ckv, kpe = pl.pallas_call(
        _mla_dec_pre_kernel,
        grid=(t // tm,),
        in_specs=[pl.BlockSpec((tm, d), lambda i: (i, 0)),
                  pl.BlockSpec((tm, LANES), lambda i: (i, 0)),
                  pl.BlockSpec((tm, LANES), lambda i: (i, 0))] + [_const_spec(a.shape) for a in pre],
        out_specs=[pl.BlockSpec((MLA_HEADS, tm, QK_LAT), lambda i: (0, i, 0)),
                   pl.BlockSpec((tm, KV_RANK), lambda i: (i, 0)),
                   pl.BlockSpec((tm, QK_ROPE), lambda i: (i, 0))],
        out_shape=[jax.ShapeDtypeStruct((MLA_HEADS, t, QK_LAT), BF16),
                   jax.ShapeDtypeStruct((t, KV_RANK), F32),
                   jax.ShapeDtypeStruct((t, QK_ROPE), F32)],
        compiler_params=_params(1),
        name="mla_dec_pre",
    )(x2, jnp.tile(cos, (b, 1)), jnp.tile(sin, (b, 1)), *pre)
    rows = MLA_HEADS * l
    o = pl.pallas_call(
        functools.partial(_mla_dec_attn_kernel, past_len=past_len),
        grid=(b, past_len // tk),
        in_specs=[pl.BlockSpec((MLA_HEADS, l, QK_LAT), lambda i, j: (0, i, 0)),
                  pl.BlockSpec((None, tk, KV_RANK), lambda i, j: (i, j, 0)),
                  pl.BlockSpec((None, tk, QK_ROPE), lambda i, j: (i, j, 0)),
                  pl.BlockSpec((l, KV_RANK), lambda i, j: (i, 0)),
                  pl.BlockSpec((l, QK_ROPE), lambda i, j: (i, 0))],
        out_specs=pl.BlockSpec((MLA_HEADS, l, KV_RANK), lambda i, j: (0, i, 0)),
        out_shape=jax.ShapeDtypeStruct((MLA_HEADS, t, KV_RANK), BF16),
        scratch_shapes=[pltpu.VMEM((rows, LANES), F32),
                        pltpu.VMEM((rows, LANES), F32),
                        pltpu.VMEM((rows, KV_RANK), F32),
                        pltpu.VMEM((LANES, QK_LAT), BF16)],
        compiler_params=_params(2),
        name="mla_dec_attn",
    )(q, cache_ckv, cache_kpe, ckv, kpe)
    post_names = ("w_uv", "w_mla_proj", "w_gm", "b_gm")
    post = [w[n] for n in post_names]
    a = pl.pallas_call(
        _mla_dec_post_kernel,
        grid=(t // tm,),
        in_specs=[pl.BlockSpec((MLA_HEADS, tm, KV_RANK), lambda i: (0, i, 0)),
                  pl.BlockSpec((tm, d), lambda i: (i, 0))] + [_const_spec(a.shape) for a in post],
        out_specs=pl.BlockSpec((tm, D_MODEL), lambda i: (i, 0)),
        out_shape=jax.ShapeDtypeStruct((t, D_MODEL), F32),
        scratch_shapes=[pltpu.VMEM((tm, MLA_HEADS * V_HEAD), BF16)],
        compiler_params=_params(1),
        name="mla_dec_post",
    )(o, x2, *post)
    return a, ckv.reshape(b, l, KV_RANK), kpe.reshape(b, l, QK_ROPE)


def _ffn_kernel(x_ref, as_ref, am_ref, wout_ref, g1_ref, b1_ref, wup_ref, wdown_ref, g2_ref, b2_ref, y_ref, *, alpha):
    sub = min(FFN_SUB_ROWS, x_ref.shape[0])
    for r0 in range(0, x_ref.shape[0], sub):
        rs = pl.ds(r0, sub)
        mix = _dot((as_ref[rs, :] + am_ref[rs, :]).astype(BF16), wout_ref[...])
        h = _layernorm(alpha * x_ref[rs, :] + mix, g1_ref[...], b1_ref[...])
        up = jnp.maximum(_dot(h.astype(BF16), wup_ref[...]), 0.0)
        ff = _dot((up * up).astype(BF16), wdown_ref[...])
        y_ref[rs, :] = _layernorm(alpha * h + ff, g2_ref[...], b2_ref[...])


def _ffn(x2, a_ssm, a_mla, w, alpha):
    t, d = x2.shape
    tm = min(FFN_ROWS, t)
    assert t % tm == 0 and tm % min(FFN_SUB_ROWS, tm) == 0
    names = ("w_out", "ln1_g", "ln1_b", "w_up", "w_down", "ln2_g", "ln2_b")
    arrs = [w[n] for n in names]
    row_spec = pl.BlockSpec((tm, d), lambda i: (i, 0))
    return pl.pallas_call(
        functools.partial(_ffn_kernel, alpha=alpha),
        grid=(t // tm,),
        in_specs=[row_spec, row_spec, row_spec] + [_const_spec(a.shape) for a in arrs],
        out_specs=row_spec,
        out_shape=jax.ShapeDtypeStruct((t, d), F32),
        compiler_params=_params(1),
        name="ffn",
    )(x2, a_ssm, a_mla, *arrs)


def _rope_tables(pos0, n):
    inv = ROPE_THETA ** (-jnp.arange(0, QK_ROPE, 2, dtype=F32) / QK_ROPE)
    ang = (pos0 + jnp.arange(n, dtype=jnp.int32)).astype(F32)[:, None] * inv[None, :]
    cos, sin = jnp.cos(ang), jnp.sin(ang)
    cos2 = jnp.concatenate([cos, cos], axis=-1)
    sin2 = jnp.concatenate([-sin, sin], axis=-1)
    return jnp.tile(cos2, (1, LANES // QK_ROPE)), jnp.tile(sin2, (1, LANES // QK_ROPE))


def _swap_halves(t):
    half = t.shape[-1] // 2
    return jnp.concatenate([t[..., half:], t[..., :half]], axis=-1)


def _layer_weights(w_in, b_gate, conv_w, conv_b, dt_bias, a_log, d_skip, ssm_norm_g, w_ssm_proj, q_norm_g, w_q_b,
                   kv_norm_g, w_uk, w_uv, w_mla_proj, w_out, ln1_g, ln1_b, w_up, w_down, ln2_g, ln2_b):
    d = w_in.shape[0]
    o_z = 2 * D_MODEL
    o_xbc = o_z + D_INNER
    o_dt = o_xbc + CONV_DIM
    o_qa = o_dt + SSM_HEADS
    o_kva = o_qa + Q_RANK
    bf = lambda t: t.astype(BF16)
    row = lambda t: t.reshape(1, -1)
    lane_pad = lambda t: jnp.pad(t, ((0, 0), (0, LANES - t.shape[-1])))
    w_kv = w_in[:, o_kva:]
    w_kpe = w_kv[:, KV_RANK:]
    zpad = jnp.zeros((d, LANES - QK_ROPE), w_in.dtype)
    qb = w_q_b.reshape(Q_RANK, MLA_HEADS, QK_NOPE + QK_ROPE)
    qb_rope = qb[:, :, QK_NOPE:]
    npair = MLA_HEADS // 2
    k = jnp.arange(LANES)[:, None]
    c = jnp.arange(D_INNER)[None, :]
    expand = ((k < 3 * SSM_HEADS) & ((k % SSM_HEADS) == (c // SSM_HEAD_DIM))).astype(BF16)
    return dict(
        w_gs=bf(w_in[:, :D_MODEL]), b_gs=row(b_gate[:D_MODEL]),
        w_gm=bf(w_in[:, D_MODEL:o_z]), b_gm=row(b_gate[D_MODEL:]),
        w_z=bf(w_in[:, o_z:o_xbc]), w_xbc=bf(w_in[:, o_xbc:o_dt]), w_dt=bf(lane_pad(w_in[:, o_dt:o_qa])),
        conv_w=conv_w, conv_b=row(conv_b), dt_bias=lane_pad(row(dt_bias)), a_log=lane_pad(row(a_log)),
        d_skip=row(jnp.repeat(d_skip, SSM_HEAD_DIM)), norm_g=row(ssm_norm_g), expand=expand,
        w_ssm_proj=bf(w_ssm_proj),
        w_qa=bf(w_in[:, o_qa:o_kva]), q_norm_g=row(q_norm_g),
        w_qb=bf(jnp.concatenate([qb[:, :, :QK_NOPE].reshape(Q_RANK, -1), qb_rope.reshape(Q_RANK, -1),
                                 _swap_halves(qb_rope).reshape(Q_RANK, -1)], axis=1)),
        w_ukt=bf(jnp.transpose(w_uk, (1, 2, 0))),
        w_qpair=bf(jnp.transpose(jnp.concatenate(
            [qb[:, :, :QK_NOPE].reshape(Q_RANK, npair, 2 * QK_NOPE), qb_rope.reshape(Q_RANK, npair, 2 * QK_ROPE),
             _swap_halves(qb_rope).reshape(Q_RANK, npair, 2 * QK_ROPE)], axis=2), (1, 0, 2))),
        w_kvpair=bf(jnp.transpose(jnp.concatenate(
            [w_uk.reshape(KV_RANK, npair, 2 * QK_NOPE), w_uv.reshape(KV_RANK, npair, 2 * V_HEAD)], axis=2), (1, 0, 2))),
        w_kva=bf(jnp.concatenate([w_kv[:, :KV_RANK], w_kpe, zpad, _swap_halves(w_kpe), zpad], axis=1)),
        kv_norm_g=row(kv_norm_g), w_uv=bf(jnp.transpose(w_uv, (1, 0, 2))), w_mla_proj=bf(w_mla_proj),
        w_out=bf(w_out), ln1_g=row(ln1_g), ln1_b=row(ln1_b), w_up=bf(w_up), w_down=bf(w_down),
        ln2_g=row(ln2_g), ln2_b=row(ln2_b))


def _prompt_layer(x, w, alpha):
    b, l, d = x.shape
    cos, sin = _rope_tables(0, l)
    a_ssm, ssm, conv = _ssd_prompt(x, w)
    a_mla, ckv, kpe = _mla_prompt(x, cos, sin, w)
    y = _ffn(x.reshape(b * l, d), a_ssm.reshape(b * l, d), a_mla.reshape(b * l, d), w, alpha)
    return y.reshape(b, l, d), ckv, kpe, ssm.reshape(b, SSM_HEADS, SSM_HEAD_DIM, SSM_STATE), conv


def _sample_layer(x, cache_ckv, cache_kpe, state_ssm, state_conv, w, alpha):
    b, l, d = x.shape
    cos, sin = _rope_tables(cache_ckv.shape[1], l)
    a_ssm, ssm, conv = _ssd_sample(x, state_ssm, state_conv, w)
    a_mla, ckv, kpe = _mla_sample(x, cache_ckv, cache_kpe, cos, sin, w)
    y = _ffn(x.reshape(b * l, d), a_ssm, a_mla, w, alpha)
    return y.reshape(b, l, d), ckv, kpe, ssm.reshape(b, SSM_HEADS, SSM_HEAD_DIM, SSM_STATE), conv


def kernel(x_prompt, x_sample, cache_ckv, cache_kpe, state_ssm, state_conv, w_in, b_gate, conv_w, conv_b, dt_bias, a_log, d_skip, ssm_norm_g, w_ssm_proj, q_norm_g, w_q_b, kv_norm_g, w_uk, w_uv, w_mla_proj, w_out, ln1_g, ln1_b, w_up, w_down, ln2_g, ln2_b):
    depth = w_in.shape[0]
    alpha = (2 * depth) ** 0.25
    layer_params = (w_in, b_gate, conv_w, conv_b, dt_bias, a_log, d_skip, ssm_norm_g, w_ssm_proj, q_norm_g, w_q_b,
                    kv_norm_g, w_uk, w_uv, w_mla_proj, w_out, ln1_g, ln1_b, w_up, w_down, ln2_g, ln2_b)
    yp, ys = x_prompt, x_sample
    st_p, st_s = [], []
    for i in range(depth):
        w = _layer_weights(*(p[i] for p in layer_params))
        yp, *sp = _prompt_layer(yp, w, alpha)
        ys, *ss = _sample_layer(ys, cache_ckv[i], cache_kpe[i], state_ssm[i], state_conv[i], w, alpha)
        st_p.append(sp)
        st_s.append(ss)
    stack = lambda sts, k: jnp.stack([s[k] for s in sts])
    return (yp, ys, stack(st_p, 0), stack(st_p, 1), stack(st_p, 2), stack(st_p, 3),
            stack(st_s, 0), stack(st_s, 1), stack(st_s, 2), stack(st_s, 3))
```

```python
import functools
import math

import jax
import jax.numpy as jnp
from jax import lax
from jax.experimental import pallas as pl
from jax.experimental.pallas import tpu as pltpu

D_MODEL = 1024
D_INNER = 2 * D_MODEL
SSM_HEAD_DIM = 64
SSM_HEADS = D_INNER // SSM_HEAD_DIM
SSM_GROUPS = 8
HEADS_PER_GROUP = SSM_HEADS // SSM_GROUPS
GROUP_DIM = HEADS_PER_GROUP * SSM_HEAD_DIM
SSM_STATE = 128
CONV_WIDTH = 4
CONV_DIM = D_INNER + 2 * SSM_GROUPS * SSM_STATE
MLA_HEADS = 16
QK_NOPE = 128
QK_ROPE = 64
V_HEAD = 128
Q_RANK = 512
KV_RANK = 512
QK_LAT = KV_RANK + QK_ROPE
ROPE_THETA = 10000.0
ATTN_SCALE = (QK_NOPE + QK_ROPE) ** -0.5
D_FF = 4 * D_MODEL
RMS_EPS = 1e-6
LN_EPS = 1e-5
CHUNK = 64
CHUNK_SHIFT = 6

LANES = 128
SUBLANES = 8
VMEM_LIMIT = 62 * 1024 * 1024

SSD_Q = 128
SSD_ROWS = 256
SSD_PROJ_TILE = 512
SSD_SAMPLE_SEQS = 4
MLA_BLK = 256
DEC_TK = 1024
DEC_SUB_ROWS = 256
ROW_TILE = 256
FFN_ROWS = 512
FFN_SUB_ROWS = 256

BF16 = jnp.bfloat16
F32 = jnp.float32
NEG_INF = float("-inf")
LOG2_E = math.log2(math.e)


def _dot(a, b):
    return jnp.dot(a, b, preferred_element_type=F32)


def _dot_nt(a, b):
    return lax.dot_general(a, b, (((1,), (1,)), ((), ())), preferred_element_type=F32)


def _rms(t):
    return t * lax.rsqrt(jnp.mean(t * t, axis=-1, keepdims=True) + RMS_EPS)


def _layernorm(t, g, b):
    mu = jnp.mean(t, axis=-1, keepdims=True)
    d = t - mu
    var = jnp.mean(d * d, axis=-1, keepdims=True)
    return d * lax.rsqrt(var + LN_EPS) * g + b


def _silu(t):
    return t * jax.nn.sigmoid(t)


def _softplus(t):
    return jnp.maximum(t, 0.0) + jnp.log(1.0 + jnp.exp(-jnp.abs(t)))


def _const_spec(shape):
    zeros = (0,) * len(shape)
    return pl.BlockSpec(shape, lambda *_: zeros, pipeline_mode=pl.Buffered(1))


def _params(n_axes):
    return pltpu.CompilerParams(dimension_semantics=("arbitrary",) * n_axes, vmem_limit_bytes=VMEM_LIMIT)


def _expand_heads(v, e_ref):
    hi = v.astype(BF16).astype(F32)
    r1 = v - hi
    mid = r1.astype(BF16).astype(F32)
    lo = r1 - mid
    lane = lax.broadcasted_iota(jnp.int32, v.shape, 1)
    packed = jnp.where(lane < 32, hi,
                       jnp.where(lane < 64, pltpu.roll(mid, 32, 1),
                                 jnp.where(lane < 96, pltpu.roll(lo, 64, 1), 0.0)))
    return _dot(packed.astype(BF16), e_ref[...])


def _ssd_block(xc, z, dt, neg_a, dskip, norm_g, e_ref, st_ref, side_jobs=None):
    q = xc.shape[0]
    row = lax.broadcasted_iota(jnp.int32, (q, q), 0)
    col = lax.broadcasted_iota(jnp.int32, (q, q), 1)
    causal = row >= col
    da = dt * neg_a
    acum = jnp.dot(causal.astype(F32), da, precision=lax.Precision.HIGHEST, preferred_element_type=F32)
    acum_t = acum.T
    dt_full = _expand_heads(dt, e_ref)
    acum_full = _expand_heads(acum, e_ref)
    decay_full = jnp.exp(acum_full)
    alast_full = acum_full[q - 1:q, :]
    xs = xc[:, :D_INNER]
    xdt = xs * dt_full
    xdt_b = xdt.astype(BF16)
    xw_b = (xdt * jnp.exp(alast_full - acum_full)).astype(BF16)
    state_decay = jnp.exp(alast_full)
    bm_t = xc[:, D_INNER:D_INNER + SSM_GROUPS * SSM_STATE].T.astype(BF16)
    cm_b = xc[:, D_INNER + SSM_GROUPS * SSM_STATE:].astype(BF16)
    lane_g = lax.broadcasted_iota(jnp.int32, (1, GROUP_DIM), 1)
    ys = []
    for g in range(SSM_GROUPS):
        if side_jobs:
            side_jobs.pop(0)()
        gsl = slice(g * GROUP_DIM, (g + 1) * GROUP_DIM)
        nsl = slice(g * SSM_STATE, (g + 1) * SSM_STATE)
        cg = cm_b[:, nsl]
        bg_t = bm_t[nsl, :]
        cb = _dot(cg, bg_t)
        xg = xdt_b[:, gsl]
        yg = None
        for r in range(HEADS_PER_GROUP):
            h = g * HEADS_PER_GROUP + r
            seg = acum[:, h:h + 1] - acum_t[h:h + 1, :]
            mh = (cb * jnp.exp(jnp.where(causal, seg, NEG_INF))).astype(BF16)
            in_head = (lane_g >= r * SSM_HEAD_DIM) & (lane_g < (r + 1) * SSM_HEAD_DIM)
            t = _dot(mh, jnp.where(in_head, xg, jnp.zeros_like(xg)))
            yg = t if yg is None else yg + t
        st = st_ref[g]
        yg = yg + _dot(cg, st.astype(BF16)) * decay_full[:, gsl]
        st_ref[g] = st * state_decay[:, gsl] + _dot(bg_t, xw_b[:, gsl])
        ys.append(yg)
    y = jnp.concatenate(ys, axis=1) + xs * dskip
    yz = y * _silu(z)
    outs = []
    for g in range(SSM_GROUPS):
        outs.append(_rms(yz[:, g * GROUP_DIM:(g + 1) * GROUP_DIM]))
    return jnp.concatenate(outs, axis=1) * norm_g


def _conv_silu(xp_ref, rows, convw, convb):
    acc = convb
    for k in range(CONV_WIDTH):
        acc = acc + xp_ref[SUBLANES - (CONV_WIDTH - 1) + k:SUBLANES - (CONV_WIDTH - 1) + k + rows, :] * convw[k:k + 1, :]
    return _silu(acc)


def _ssd_project(xb, wz_ref, wxbc_ref, wdt_ref, wg_ref, bg_ref, dtb_ref, z_ref, xp_ref, dt_ref, gate_ref):
    for job in _ssd_project_jobs(lambda: xb, wz_ref, wxbc_ref, wdt_ref, wg_ref, bg_ref, dtb_ref,
                                 z_ref, xp_ref, dt_ref, gate_ref):
        job()


def _ssd_project_jobs(get_xb, wz_ref, wxbc_ref, wdt_ref, wg_ref, bg_ref, dtb_ref, z_ref, xp_ref, dt_ref, gate_ref):
    rows = z_ref.shape[0]
    tile = SSD_PROJ_TILE

    def xbc_job(c0):
        def job():
            xp_ref[SUBLANES:SUBLANES + rows, c0:c0 + tile] = _dot(get_xb(), wxbc_ref[:, c0:c0 + tile])
        return job

    def z_job(c0):
        def job():
            z_ref[:, c0:c0 + tile] = _dot(get_xb(), wz_ref[:, c0:c0 + tile])
        return job

    def gate_job(c0):
        def job():
            gate_ref[:, c0:c0 + tile] = jax.nn.sigmoid(_dot(get_xb(), wg_ref[:, c0:c0 + tile]) + bg_ref[:, c0:c0 + tile])
        return job

    def dt_job():
        dt_ref[...] = _softplus(_dot(get_xb(), wdt_ref[...]) + dtb_ref[...])

    return ([xbc_job(c0) for c0 in range(0, CONV_DIM, tile)] + [z_job(c0) for c0 in range(0, D_INNER, tile)]
            + [gate_job(c0) for c0 in range(0, D_MODEL, tile)] + [dt_job])


def _ssd_prompt_kernel(x0_ref, xn_ref, wz_ref, wxbc_ref, wdt_ref, wg_ref, bg_ref, convw_ref, convb_ref, dtb_ref,
                       alog_ref, dskip_ref, ng_ref, e_ref, wproj_ref,
                       a_out_ref, ssm_out_ref, conv_out_ref,
                       st_ref, hist_ref, xbn_ref, z_a, xp_a, dt_a, gate_a, z_b, xp_b, dt_b, gate_b, *, steps_per_row):
    g = pl.program_id(0)
    c = g % steps_per_row
    rows = xn_ref.shape[0]
    proj_w = (wz_ref, wxbc_ref, wdt_ref, wg_ref, bg_ref, dtb_ref)

    @pl.when(g == 0)
    def _():
        _ssd_project(x0_ref[...].astype(BF16), *proj_w, z_a, xp_a, dt_a, gate_a)

    @pl.when(c == 0)
    def _():
        st_ref[...] = jnp.zeros_like(st_ref)
        hist_ref[...] = jnp.zeros_like(hist_ref)

    def step(cur, nxt):
        z_ref, xp_ref, dt_ref, gate_ref = cur
        xbn_ref[...] = xn_ref[...].astype(BF16)
        jobs = _ssd_project_jobs(lambda: xbn_ref[...], *proj_w, *nxt)
        xp_ref[0:SUBLANES, :] = hist_ref[...]
        xc = _conv_silu(xp_ref, rows, convw_ref[...], convb_ref[...])
        hist_ref[...] = xp_ref[rows:rows + SUBLANES, :]
        neg_a = -jnp.exp(alog_ref[...])
        yn = []
        for s in range(rows // SSD_Q):
            sl = slice(s * SSD_Q, (s + 1) * SSD_Q)
            yn.append(_ssd_block(xc[sl], z_ref[sl, :], dt_ref[sl, :], neg_a, dskip_ref[...], ng_ref[...], e_ref,
                                 st_ref, jobs))
        for job in jobs:
            job()
        yn = jnp.concatenate(yn, axis=0).astype(BF16)
        a_out_ref[...] = gate_ref[...] * _dot(yn, wproj_ref[...])

    slot_a, slot_b = (z_a, xp_a, dt_a, gate_a), (z_b, xp_b, dt_b, gate_b)

    @pl.when(g % 2 == 0)
    def _():
        step(slot_a, slot_b)

    @pl.when(g % 2 == 1)
    def _():
        step(slot_b, slot_a)

    @pl.when(c == steps_per_row - 1)
    def _():
        conv_out_ref[...] = hist_ref[SUBLANES - (CONV_WIDTH - 1):SUBLANES, :]
        for k in range(SSM_GROUPS):
            ssm_out_ref[k * GROUP_DIM:(k + 1) * GROUP_DIM, :] = st_ref[k].T


def _ssd_sample_kernel(x_ref, state_ref, cstate_ref, wz_ref, wxbc_ref, wdt_ref, wg_ref, bg_ref, convw_ref, convb_ref,
                       dtb_ref, alog_ref, dskip_ref, ng_ref, e_ref, wproj_ref,
                       a_out_ref, ssm_out_ref, conv_out_ref,
                       st_ref, xp_ref, z_scr, xbc_scr, dt_scr, xcp_scr, zp_scr, dtp_scr, yn_scr):
    nseq, seq = cstate_ref.shape[0], x_ref.shape[0] // cstate_ref.shape[0]
    xb = x_ref[...].astype(BF16)
    z_scr[...] = _dot(xb, wz_ref[...])
    xbc_scr[...] = _dot(xb, wxbc_ref[...])
    dt_scr[...] = _softplus(_dot(xb, wdt_ref[...]) + dtb_ref[...])
    gate = jax.nn.sigmoid(_dot(xb, wg_ref[...]) + bg_ref[...])
    xcp_scr[...] = jnp.zeros_like(xcp_scr)
    zp_scr[...] = jnp.zeros_like(zp_scr)
    dtp_scr[...] = jnp.zeros_like(dtp_scr)
    neg_a = -jnp.exp(alog_ref[...])
    lo = SUBLANES - (CONV_WIDTH - 1)

    def body(s, carry):
        r0 = pl.multiple_of(s * seq, seq)
        xp_ref[lo:SUBLANES, :] = cstate_ref[s]
        xp_ref[SUBLANES:SUBLANES + seq, :] = xbc_scr[pl.ds(r0, seq), :]
        xcp_scr[0:seq, :] = _conv_silu(xp_ref, seq, convw_ref[...], convb_ref[...])
        conv_out_ref[s] = xp_ref[seq + lo:seq + SUBLANES, :]
        zp_scr[0:seq, :] = z_scr[pl.ds(r0, seq), :]
        dtp_scr[0:seq, :] = dt_scr[pl.ds(r0, seq), :]
        for g in range(SSM_GROUPS):
            st_ref[g] = state_ref[s, g * GROUP_DIM:(g + 1) * GROUP_DIM, :].T
        yn = _ssd_block(xcp_scr[...], zp_scr[...], dtp_scr[...], neg_a, dskip_ref[...], ng_ref[...], e_ref, st_ref)
        yn_scr[pl.ds(r0, seq), :] = yn[0:seq].astype(BF16)
        for g in range(SSM_GROUPS):
            ssm_out_ref[s, g * GROUP_DIM:(g + 1) * GROUP_DIM, :] = st_ref[g].T
        return carry

    lax.fori_loop(0, nseq, body, 0)
    a_out_ref[...] = gate * _dot(yn_scr[...], wproj_ref[...])


def _ssd_weight_specs(w):
    names = ("w_z", "w_xbc", "w_dt", "w_gs", "b_gs", "conv_w", "conv_b", "dt_bias", "a_log", "d_skip", "norm_g",
             "expand", "w_ssm_proj")
    arrs = [w[n] for n in names]
    return arrs, [_const_spec(a.shape) for a in arrs]


def _ssd_prompt(x, w):
    b, l, d = x.shape
    rows = min(SSD_ROWS, l)
    assert l % rows == 0 and rows % SSD_Q == 0
    arrs, specs = _ssd_weight_specs(w)
    nc = l // rows
    last = b * nc - 1
    slot = [pltpu.VMEM((rows, D_INNER), F32), pltpu.VMEM((rows + SUBLANES, CONV_DIM), F32),
            pltpu.VMEM((rows, LANES), F32), pltpu.VMEM((rows, D_MODEL), F32)]

    def next_block(g):
        n = jnp.minimum(g + 1, last)
        return (n // nc, n % nc, 0)

    return pl.pallas_call(
        functools.partial(_ssd_prompt_kernel, steps_per_row=nc),
        grid=(b * nc,),
        in_specs=[pl.BlockSpec((None, rows, d), lambda g: (0, 0, 0)),
                  pl.BlockSpec((None, rows, d), next_block)] + specs,
        out_specs=[pl.BlockSpec((None, rows, D_MODEL), lambda g: (g // nc, g % nc, 0)),
                   pl.BlockSpec((None, D_INNER, SSM_STATE), lambda g: (g // nc, 0, 0)),
                   pl.BlockSpec((None, CONV_WIDTH - 1, CONV_DIM), lambda g: (g // nc, 0, 0))],
        out_shape=[jax.ShapeDtypeStruct((b, l, D_MODEL), F32),
                   jax.ShapeDtypeStruct((b, D_INNER, SSM_STATE), F32),
                   jax.ShapeDtypeStruct((b, CONV_WIDTH - 1, CONV_DIM), F32)],
        scratch_shapes=[pltpu.VMEM((SSM_GROUPS, SSM_STATE, GROUP_DIM), F32),
                        pltpu.VMEM((SUBLANES, CONV_DIM), F32),
                        pltpu.VMEM((rows, d), BF16)] + slot + slot,
        compiler_params=_params(1),
        name="ssd_prompt",
    )(x, x, *arrs)


def _ssd_sample(x, state, cstate, w):
    b, l, d = x.shape
    nseq = min(SSD_SAMPLE_SEQS, b)
    assert b % nseq == 0 and l % SUBLANES == 0 and l <= SSD_Q
    rows = nseq * l
    arrs, specs = _ssd_weight_specs(w)
    a, ssm, conv = pl.pallas_call(
        _ssd_sample_kernel,
        grid=(b // nseq,),
        in_specs=[pl.BlockSpec((rows, d), lambda i: (i, 0)),
                  pl.BlockSpec((nseq, D_INNER, SSM_STATE), lambda i: (i, 0, 0)),
                  pl.BlockSpec((nseq, CONV_WIDTH - 1, CONV_DIM), lambda i: (i, 0, 0))] + specs,
        out_specs=[pl.BlockSpec((rows, D_MODEL), lambda i: (i, 0)),
                   pl.BlockSpec((nseq, D_INNER, SSM_STATE), lambda i: (i, 0, 0)),
                   pl.BlockSpec((nseq, CONV_WIDTH - 1, CONV_DIM), lambda i: (i, 0, 0))],
        out_shape=[jax.ShapeDtypeStruct((b * l, D_MODEL), F32),
                   jax.ShapeDtypeStruct((b, D_INNER, SSM_STATE), F32),
                   jax.ShapeDtypeStruct((b, CONV_WIDTH - 1, CONV_DIM), F32)],
        scratch_shapes=[pltpu.VMEM((SSM_GROUPS, SSM_STATE, GROUP_DIM), F32),
                        pltpu.VMEM((l + SUBLANES, CONV_DIM), F32),
                        pltpu.VMEM((rows, D_INNER), F32),
                        pltpu.VMEM((rows, CONV_DIM), F32),
                        pltpu.VMEM((rows, LANES), F32),
                        pltpu.VMEM((SSD_Q, CONV_DIM), F32),
                        pltpu.VMEM((SSD_Q, D_INNER), F32),
                        pltpu.VMEM((SSD_Q, LANES), F32),
                        pltpu.VMEM((rows, D_INNER), BF16)],
        compiler_params=_params(1),
        name="ssd_sample",
    )(x.reshape(b * l, d), state.reshape(b, D_INNER, SSM_STATE), cstate, *arrs)
    return a, ssm, conv


def _mla_keys(xb, cos, sin, wkva_ref, kvg_ref):
    kva = _dot(xb, wkva_ref[...])
    ckv = _rms(kva[:, :KV_RANK]) * kvg_ref[...]
    kpe = (kva[:, KV_RANK:KV_RANK + QK_ROPE] * cos[:, :QK_ROPE]
           + kva[:, KV_RANK + LANES:KV_RANK + LANES + QK_ROPE] * sin[:, :QK_ROPE])
    return ckv, kpe


def _mla_queries(xb, cos, sin, wqa_ref, qg_ref, wqb_ref, wukt_ref, store):
    qn = (_rms(_dot(xb, wqa_ref[...])) * qg_ref[...]).astype(BF16)
    q = _dot(qn, wqb_ref[...])
    nope_w = MLA_HEADS * QK_NOPE
    rope_w = MLA_HEADS * QK_ROPE
    for p in range(MLA_HEADS // 2):
        sl = slice(nope_w + p * LANES, nope_w + (p + 1) * LANES)
        sl_sw = slice(nope_w + rope_w + p * LANES, nope_w + rope_w + (p + 1) * LANES)
        pair = (q[:, sl] * cos + q[:, sl_sw] * sin) * ATTN_SCALE
        pair_hi = pltpu.roll(pair, QK_ROPE, 1)
        for h, pe in ((2 * p, pair), (2 * p + 1, pair_hi)):
            q_lat = _dot(q[:, h * QK_NOPE:(h + 1) * QK_NOPE].astype(BF16), wukt_ref[h]) * ATTN_SCALE
            store(h, q_lat.astype(BF16), pe[:, :QK_ROPE].astype(BF16))


def _lane_tile(t, width):
    return jnp.concatenate([t] * (width // LANES), axis=1) if width > LANES else t


def _softmax_update(s, v_b, m_scr, l_scr, acc_scr):
    m_prev = m_scr[...]
    m_new = jnp.maximum(m_prev, jnp.max(s, axis=-1, keepdims=True))
    alpha = jnp.exp(m_prev - m_new)
    p = jnp.exp(s - _lane_tile(m_new, s.shape[1]))
    p_lanes = p[:, 0:LANES]
    for c in range(1, s.shape[1] // LANES):
        p_lanes = p_lanes + p[:, c * LANES:(c + 1) * LANES]
    l_scr[...] = alpha * l_scr[...] + p_lanes
    acc_scr[...] = _lane_tile(alpha, acc_scr.shape[1]) * acc_scr[...] + _dot(p.astype(BF16), v_b)
    m_scr[...] = m_new


def _softmax_finish(l_scr, acc_scr):
    return acc_scr[...] / jnp.sum(l_scr[...], axis=-1, keepdims=True)


def _mla_out(o_of_head, xb, wuv_ref, wmla_ref, wg_ref, bg_ref, o_scr):
    for h in range(MLA_HEADS):
        o_scr[:, h * V_HEAD:(h + 1) * V_HEAD] = _dot(o_of_head(h), wuv_ref[h]).astype(BF16)
    gate = jax.nn.sigmoid(_dot(xb, wg_ref[...]) + bg_ref[...])
    return gate * _dot(o_scr[...], wmla_ref[...])


def _mla_pre_kernel(x_ref, cos_ref, sin_ref, wqa_ref, qg_ref, wkva_ref, kvg_ref, qn_out_ref, ckv_out_ref, kpe_out_ref):
    xb = x_ref[...].astype(BF16)
    ckv, kpe = _mla_keys(xb, cos_ref[...], sin_ref[...], wkva_ref, kvg_ref)
    ckv_out_ref[...] = ckv
    kpe_out_ref[...] = kpe
    qn_out_ref[...] = (_rms(_dot(xb, wqa_ref[...])) * qg_ref[...]).astype(BF16)


def _mla_prompt_attn_kernel(qn_ref, ckv_ref, kpe_ref, cos_ref, sin_ref, wqp_ref, wkv_ref, o_ref,
                            qa_scr, qb_scr, ka_scr, kb_scr, vbd_scr, m_scr, l_scr, acc_scr):
    l = qn_ref.shape[0]
    g = min(MLA_BLK, l)
    nb = l // g
    half = QK_ROPE
    zpad = jnp.zeros((l, LANES - half), BF16)
    kv = _dot(ckv_ref[...].astype(BF16), wkv_ref[...])
    kpe_b = kpe_ref[...].astype(BF16)
    for scr, c0 in ((ka_scr, 0), (kb_scr, QK_NOPE)):
        scr[:, 0:QK_NOPE] = kv[:, c0:c0 + QK_NOPE].astype(BF16)
        scr[:, QK_NOPE:QK_NOPE + half] = kpe_b
        scr[:, QK_NOPE + half:] = zpad
    zv = jnp.zeros((g, V_HEAD), BF16)
    for j in range(nb):
        rs = slice(j * g, (j + 1) * g)
        vbd_scr[j, 0:g, 0:V_HEAD] = kv[rs, 2 * QK_NOPE:2 * QK_NOPE + V_HEAD].astype(BF16)
        vbd_scr[j, 0:g, V_HEAD:] = zv
        vbd_scr[j, g:, 0:V_HEAD] = zv
        vbd_scr[j, g:, V_HEAD:] = kv[rs, 2 * QK_NOPE + V_HEAD:].astype(BF16)
    q = _dot(qn_ref[...], wqp_ref[...])
    c = ATTN_SCALE * LOG2_E
    pair = (q[:, 2 * QK_NOPE:3 * QK_NOPE] * cos_ref[...] + q[:, 3 * QK_NOPE:] * sin_ref[...]) * c
    pair_hi = pltpu.roll(pair, half, 1)
    for scr, c0, pe in ((qa_scr, 0, pair), (qb_scr, QK_NOPE, pair_hi)):
        scr[:, 0:QK_NOPE] = (q[:, c0:c0 + QK_NOPE] * c).astype(BF16)
        scr[:, QK_NOPE:QK_NOPE + half] = pe[:, :half].astype(BF16)
        scr[:, QK_NOPE + half:] = zpad
    row = lax.broadcasted_iota(jnp.int32, (g, g), 0)
    col = lax.broadcasted_iota(jnp.int32, (g, g), 1)
    diag_visible = (row >> CHUNK_SHIFT) >= (col >> CHUNK_SHIFT)
    for j in range(nb):
        r0 = j * g
        ps, alphas = [], []
        for hi, (q_scr, k_scr) in enumerate(((qa_scr, ka_scr), (qb_scr, kb_scr))):
            s = _dot_nt(q_scr[r0:, :], k_scr[r0:r0 + g, :])
            top = jnp.where(diag_visible, s[:g], NEG_INF)
            s = top if j == nb - 1 else jnp.concatenate([top, s[g:]], axis=0)
            m_cur = jnp.max(s, axis=-1, keepdims=True)
            if j == 0:
                m_new = jnp.broadcast_to(m_cur, (l, LANES))
            else:
                m_prev = m_scr[hi, r0:, :]
                m_new = jnp.maximum(m_prev, m_cur)
                alphas.append(jnp.exp2(m_prev - m_new))
            p = jnp.exp2(s - _lane_tile(m_new, g))
            p_lanes = p[:, 0:LANES]
            for cc in range(1, g // LANES):
                p_lanes = p_lanes + p[:, cc * LANES:(cc + 1) * LANES]
            l_scr[hi, r0:, :] = p_lanes if j == 0 else alphas[-1] * l_scr[hi, r0:, :] + p_lanes
            m_scr[hi, r0:, :] = m_new
            ps.append(p.astype(BF16))
        pv = _dot(jnp.concatenate(ps, axis=1), vbd_scr[j])
        if j == 0:
            acc_scr[...] = pv
        else:
            acc_scr[r0:, :] = jnp.concatenate(alphas, axis=1) * acc_scr[r0:, :] + pv
    inv = [1.0 / jnp.sum(l_scr[hi], axis=-1, keepdims=True) for hi in range(2)]
    scale = jnp.concatenate([jnp.broadcast_to(t, (l, V_HEAD)) for t in inv], axis=1)
    o_ref[...] = (acc_scr[...] * scale).astype(BF16)


def _mla_post_kernel(o_ref, x_ref, wmla_ref, wg_ref, bg_ref, a_out_ref):
    gate = jax.nn.sigmoid(_dot(x_ref[...].astype(BF16), wg_ref[...]) + bg_ref[...])
    a_out_ref[...] = gate * _dot(o_ref[...], wmla_ref[...])


def _mla_prompt(x, cos, sin, w):
    b, l, d = x.shape
    g = min(MLA_BLK, l)
    tm = min(FFN_ROWS, l)
    npair = MLA_HEADS // 2
    assert l % g == 0 and g % CHUNK == 0 and g % LANES == 0 and l % tm == 0
    nt = l // tm
    pre = [w[n] for n in ("w_qa", "q_norm_g", "w_kva", "kv_norm_g")]
    qn, ckv, kpe = pl.pallas_call(
        _mla_pre_kernel,
        grid=(b, nt),
        in_specs=[pl.BlockSpec((None, tm, d), lambda i, c: (i, c, 0)),
                  pl.BlockSpec((tm, LANES), lambda i, c: (c, 0)),
                  pl.BlockSpec((tm, LANES), lambda i, c: (c, 0))] + [_const_spec(a.shape) for a in pre],
        out_specs=[pl.BlockSpec((None, tm, Q_RANK), lambda i, c: (i, c, 0)),
                   pl.BlockSpec((None, tm, KV_RANK), lambda i, c: (i, c, 0)),
                   pl.BlockSpec((None, tm, QK_ROPE), lambda i, c: (i, c, 0))],
        out_shape=[jax.ShapeDtypeStruct((b, l, Q_RANK), BF16),
                   jax.ShapeDtypeStruct((b, l, KV_RANK), F32),
                   jax.ShapeDtypeStruct((b, l, QK_ROPE), F32)],
        compiler_params=_params(2),
        name="mla_pre",
    )(x, cos, sin, *pre)
    o = pl.pallas_call(
        _mla_prompt_attn_kernel,
        grid=(b, npair),
        in_specs=[pl.BlockSpec((None, l, Q_RANK), lambda i, p: (i, 0, 0)),
                  pl.BlockSpec((None, l, KV_RANK), lambda i, p: (i, 0, 0)),
                  pl.BlockSpec((None, l, QK_ROPE), lambda i, p: (i, 0, 0)),
                  _const_spec(cos.shape), _const_spec(sin.shape),
                  pl.BlockSpec((None, Q_RANK, 4 * QK_NOPE), lambda i, p: (p, 0, 0)),
                  pl.BlockSpec((None, KV_RANK, 2 * QK_NOPE + 2 * V_HEAD), lambda i, p: (p, 0, 0))],
        out_specs=pl.BlockSpec((None, l, 2 * V_HEAD), lambda i, p: (i, 0, p)),
        out_shape=jax.ShapeDtypeStruct((b, l, MLA_HEADS * V_HEAD), BF16),
        scratch_shapes=[pltpu.VMEM((l, 2 * LANES), BF16)] * 4
        + [pltpu.VMEM((l // g, 2 * g, 2 * V_HEAD), BF16),
           pltpu.VMEM((2, l, LANES), F32),
           pltpu.VMEM((2, l, LANES), F32),
           pltpu.VMEM((l, 2 * V_HEAD), F32)],
        compiler_params=_params(2),
        name="mla_prompt_attn",
    )(qn, ckv, kpe, cos, sin, w["w_qpair"], w["w_kvpair"])
    post = [w[n] for n in ("w_mla_proj", "w_gm", "b_gm")]
    a = pl.pallas_call(
        _mla_post_kernel,
        grid=(b, nt),
        in_specs=[pl.BlockSpec((None, tm, MLA_HEADS * V_HEAD), lambda i, c: (i, c, 0)),
                  pl.BlockSpec((None, tm, d), lambda i, c: (i, c, 0))] + [_const_spec(t.shape) for t in post],
        out_specs=pl.BlockSpec((None, tm, D_MODEL), lambda i, c: (i, c, 0)),
        out_shape=jax.ShapeDtypeStruct((b, l, D_MODEL), F32),
        compiler_params=_params(2),
        name="mla_post",
    )(o, x, *post)
    return a, ckv, kpe


def _mla_dec_pre_kernel(x_ref, cos_ref, sin_ref, wqa_ref, qg_ref, wqb_ref, wukt_ref, wkva_ref, kvg_ref,
                        q_out_ref, ckv_out_ref, kpe_out_ref):
    xb = x_ref[...].astype(BF16)
    cos, sin = cos_ref[...], sin_ref[...]
    ckv, kpe = _mla_keys(xb, cos, sin, wkva_ref, kvg_ref)
    ckv_out_ref[...] = ckv
    kpe_out_ref[...] = kpe

    def store(h, q_lat, q_pe):
        q_out_ref[h, :, 0:KV_RANK] = q_lat
        q_out_ref[h, :, KV_RANK:QK_LAT] = q_pe

    _mla_queries(xb, cos, sin, wqa_ref, qg_ref, wqb_ref, wukt_ref, store)


def _mla_dec_attn_kernel(q_ref, cckv_ref, ckpe_t_ref, nckv_ref, nkpe_ref, o_ref,
                         m_scr, l_scr, acc_scr, nk_scr, *, past_len):
    j = pl.program_id(1)
    seq = q_ref.shape[1]
    rows = MLA_HEADS * seq

    @pl.when(j == 0)
    def _():
        m_scr[...] = jnp.full_like(m_scr, NEG_INF)
        l_scr[...] = jnp.zeros_like(l_scr)
        acc_scr[...] = jnp.zeros_like(acc_scr)

    q = q_ref[...].reshape(rows, QK_LAT)
    ck = cckv_ref[...].astype(BF16)
    kp_t = ckpe_t_ref[...].astype(BF16)
    sub = min(DEC_SUB_ROWS, rows)
    for r0 in range(0, rows, sub):
        rs = pl.ds(r0, sub)
        s = _dot_nt(q[r0:r0 + sub, :KV_RANK], ck) + _dot(q[r0:r0 + sub, KV_RANK:], kp_t)
        _softmax_update(s, ck, m_scr.at[rs], l_scr.at[rs], acc_scr.at[rs])

    @pl.when(j == pl.num_programs(1) - 1)
    def _():
        nk_scr[...] = jnp.zeros_like(nk_scr)
        nk_scr[0:seq, 0:KV_RANK] = nckv_ref[...].astype(BF16)
        nk_scr[0:seq, KV_RANK:QK_LAT] = nkpe_ref[...].astype(BF16)
        nk = nk_scr[...]
        s2 = _dot_nt(q, nk)
        row = lax.broadcasted_iota(jnp.int32, (rows, 1), 0)
        q_chunk = (past_len + (row & (seq - 1))) >> CHUNK_SHIFT
        k_lane = lax.broadcasted_iota(jnp.int32, (1, LANES), 1)
        visible = (k_lane < seq) & (((past_len + k_lane) >> CHUNK_SHIFT) <= q_chunk)
        _softmax_update(jnp.where(visible, s2, NEG_INF), nk[:, :KV_RANK], m_scr, l_scr, acc_scr)
        o_ref[...] = _softmax_finish(l_scr, acc_scr).astype(BF16).reshape(MLA_HEADS, seq, KV_RANK)


def _mla_dec_post_kernel(o_ref, x_ref, wuv_ref, wmla_ref, wg_ref, bg_ref, a_out_ref, o_scr):
    xb = x_ref[...].astype(BF16)
    a_out_ref[...] = _mla_out(lambda h: o_ref[h], xb, wuv_ref, wmla_ref, wg_ref, bg_ref, o_scr)


def _mla_sample(x, cache_ckv, cache_kpe, cos, sin, w):
    b, l, d = x.shape
    t = b * l
    past_len = cache_ckv.shape[1]
    tm = min(ROW_TILE, t)
    tk = min(DEC_TK, past_len)
    assert t % tm == 0 and past_len % tk == 0 and (l & (l - 1)) == 0 and l <= LANES and l % 16 == 0
    x2 = x.reshape(t, d)
    pre_names = ("w_qa", "q_norm_g", "w_qb", "w_ukt", "w_kva", "kv_norm_g")
    pre = [w[n] for n in pre_names]
    q, ckv, kpe = pl.pallas_call(
        _mla_dec_pre_kernel,
        grid=(t // tm,),
        in_specs=[pl.BlockSpec((tm, d), lambda i: (i, 0)),
                  pl.BlockSpec((tm, LANES), lambda i: (i, 0)),
                  pl.BlockSpec((tm, LANES), lambda i: (i, 0))] + [_const_spec(a.shape) for a in pre],
        out_specs=[pl.BlockSpec((MLA_HEADS, tm, QK_LAT), lambda i: (0, i, 0)),
                   pl.BlockSpec((tm, KV_RANK), lambda i: (i, 0)),
                   pl.BlockSpec((tm, QK_ROPE), lambda i: (i, 0))],
        out_shape=[jax.ShapeDtypeStruct((MLA_HEADS, t, QK_LAT), BF16),
                   jax.ShapeDtypeStruct((t, KV_RANK), F32),
                   jax.ShapeDtypeStruct((t, QK_ROPE), F32)],
        compiler_params=_params(1),
        name="mla_dec_pre",
    )(x2, jnp.tile(cos, (b, 1)), jnp.tile(sin, (b, 1)), *pre)
    rows = MLA_HEADS * l
    o = pl.pallas_call(
        functools.partial(_mla_dec_attn_kernel, past_len=past_len),
        grid=(b, past_len // tk),
        in_specs=[pl.BlockSpec((MLA_HEADS, l, QK_LAT), lambda i, j: (0, i, 0)),
                  pl.BlockSpec((None, tk, KV_RANK), lambda i, j: (i, j, 0)),
                  pl.BlockSpec((None, QK_ROPE, tk), lambda i, j: (i, 0, j)),
                  pl.BlockSpec((l, KV_RANK), lambda i, j: (i, 0)),
                  pl.BlockSpec((l, QK_ROPE), lambda i, j: (i, 0))],
        out_specs=pl.BlockSpec((MLA_HEADS, l, KV_RANK), lambda i, j: (0, i, 0)),
        out_shape=jax.ShapeDtypeStruct((MLA_HEADS, t, KV_RANK), BF16),
        scratch_shapes=[pltpu.VMEM((rows, LANES), F32),
                        pltpu.VMEM((rows, LANES), F32),
                        pltpu.VMEM((rows, KV_RANK), F32),
                        pltpu.VMEM((LANES, QK_LAT), BF16)],
        compiler_params=_params(2),
        name="mla_dec_attn",
    )(q, cache_ckv, jnp.swapaxes(cache_kpe, 1, 2), ckv, kpe)
    post_names = ("w_uv", "w_mla_proj", "w_gm", "b_gm")
    post = [w[n] for n in post_names]
    a = pl.pallas_call(
        _mla_dec_post_kernel,
        grid=(t // tm,),
        in_specs=[pl.BlockSpec((MLA_HEADS, tm, KV_RANK), lambda i: (0, i, 0)),
                  pl.BlockSpec((tm, d), lambda i: (i, 0))] + [_const_spec(a.shape) for a in post],
        out_specs=pl.BlockSpec((tm, D_MODEL), lambda i: (i, 0)),
        out_shape=jax.ShapeDtypeStruct((t, D_MODEL), F32),
        scratch_shapes=[pltpu.VMEM((tm, MLA_HEADS * V_HEAD), BF16)],
        compiler_params=_params(1),
        name="mla_dec_post",
    )(o, x2, *post)
    return a, ckv.reshape(b, l, KV_RANK), kpe.reshape(b, l, QK_ROPE)


def _ffn_kernel(x_ref, as_ref, am_ref, wout_ref, g1_ref, b1_ref, wup_ref, wdown_ref, g2_ref, b2_ref, y_ref, *, alpha):
    sub = min(FFN_SUB_ROWS, x_ref.shape[0])
    for r0 in range(0, x_ref.shape[0], sub):
        rs = pl.ds(r0, sub)
        mix = _dot((as_ref[rs, :] + am_ref[rs, :]).astype(BF16), wout_ref[...])
        h = _layernorm(alpha * x_ref[rs, :] + mix, g1_ref[...], b1_ref[...])
        up = jnp.maximum(_dot(h.astype(BF16), wup_ref[...]), 0.0)
        ff = _dot((up * up).astype(BF16), wdown_ref[...])
        y_ref[rs, :] = _layernorm(alpha * h + ff, g2_ref[...], b2_ref[...])


def _ffn(x2, a_ssm, a_mla, w, alpha):
    t, d = x2.shape
    tm = min(FFN_ROWS, t)
    assert t % tm == 0 and tm % min(FFN_SUB_ROWS, tm) == 0
    names = ("w_out", "ln1_g", "ln1_b", "w_up", "w_down", "ln2_g", "ln2_b")
    arrs = [w[n] for n in names]
    row_spec = pl.BlockSpec((tm, d), lambda i: (i, 0))
    return pl.pallas_call(
        functools.partial(_ffn_kernel, alpha=alpha),
        grid=(t // tm,),
        in_specs=[row_spec, row_spec, row_spec] + [_const_spec(a.shape) for a in arrs],
        out_specs=row_spec,
        out_shape=jax.ShapeDtypeStruct((t, d), F32),
        compiler_params=_params(1),
        name="ffn",
    )(x2, a_ssm, a_mla, *arrs)


def _rope_tables(pos0, n):
    inv = ROPE_THETA ** (-jnp.arange(0, QK_ROPE, 2, dtype=F32) / QK_ROPE)
    ang = (pos0 + jnp.arange(n, dtype=jnp.int32)).astype(F32)[:, None] * inv[None, :]
    cos, sin = jnp.cos(ang), jnp.sin(ang)
    cos2 = jnp.concatenate([cos, cos], axis=-1)
    sin2 = jnp.concatenate([-sin, sin], axis=-1)
    return jnp.tile(cos2, (1, LANES // QK_ROPE)), jnp.tile(sin2, (1, LANES // QK_ROPE))


def _swap_halves(t):
    half = t.shape[-1] // 2
    return jnp.concatenate([t[..., half:], t[..., :half]], axis=-1)


def _layer_weights(w_in, b_gate, conv_w, conv_b, dt_bias, a_log, d_skip, ssm_norm_g, w_ssm_proj, q_norm_g, w_q_b,
                   kv_norm_g, w_uk, w_uv, w_mla_proj, w_out, ln1_g, ln1_b, w_up, w_down, ln2_g, ln2_b):
    d = w_in.shape[0]
    o_z = 2 * D_MODEL
    o_xbc = o_z + D_INNER
    o_dt = o_xbc + CONV_DIM
    o_qa = o_dt + SSM_HEADS
    o_kva = o_qa + Q_RANK
    bf = lambda t: t.astype(BF16)
    row = lambda t: t.reshape(1, -1)
    lane_pad = lambda t: jnp.pad(t, ((0, 0), (0, LANES - t.shape[-1])))
    w_kv = w_in[:, o_kva:]
    w_kpe = w_kv[:, KV_RANK:]
    zpad = jnp.zeros((d, LANES - QK_ROPE), w_in.dtype)
    qb = w_q_b.reshape(Q_RANK, MLA_HEADS, QK_NOPE + QK_ROPE)
    qb_rope = qb[:, :, QK_NOPE:]
    npair = MLA_HEADS // 2
    k = jnp.arange(LANES)[:, None]
    c = jnp.arange(D_INNER)[None, :]
    expand = ((k < 3 * SSM_HEADS) & ((k % SSM_HEADS) == (c // SSM_HEAD_DIM))).astype(BF16)
    return dict(
        w_gs=bf(w_in[:, :D_MODEL]), b_gs=row(b_gate[:D_MODEL]),
        w_gm=bf(w_in[:, D_MODEL:o_z]), b_gm=row(b_gate[D_MODEL:]),
        w_z=bf(w_in[:, o_z:o_xbc]), w_xbc=bf(w_in[:, o_xbc:o_dt]), w_dt=bf(lane_pad(w_in[:, o_dt:o_qa])),
        conv_w=conv_w, conv_b=row(conv_b), dt_bias=lane_pad(row(dt_bias)), a_log=lane_pad(row(a_log)),
        d_skip=row(jnp.repeat(d_skip, SSM_HEAD_DIM)), norm_g=row(ssm_norm_g), expand=expand,
        w_ssm_proj=bf(w_ssm_proj),
        w_qa=bf(w_in[:, o_qa:o_kva]), q_norm_g=row(q_norm_g),
        w_qb=bf(jnp.concatenate([qb[:, :, :QK_NOPE].reshape(Q_RANK, -1), qb_rope.reshape(Q_RANK, -1),
                                 _swap_halves(qb_rope).reshape(Q_RANK, -1)], axis=1)),
        w_ukt=bf(jnp.transpose(w_uk, (1, 2, 0))),
        w_qpair=bf(jnp.transpose(jnp.concatenate(
            [qb[:, :, :QK_NOPE].reshape(Q_RANK, npair, 2 * QK_NOPE), qb_rope.reshape(Q_RANK, npair, 2 * QK_ROPE),
             _swap_halves(qb_rope).reshape(Q_RANK, npair, 2 * QK_ROPE)], axis=2), (1, 0, 2))),
        w_kvpair=bf(jnp.transpose(jnp.concatenate(
            [w_uk.reshape(KV_RANK, npair, 2 * QK_NOPE), w_uv.reshape(KV_RANK, npair, 2 * V_HEAD)], axis=2), (1, 0, 2))),
        w_kva=bf(jnp.concatenate([w_kv[:, :KV_RANK], w_kpe, zpad, _swap_halves(w_kpe), zpad], axis=1)),
        kv_norm_g=row(kv_norm_g), w_uv=bf(jnp.transpose(w_uv, (1, 0, 2))), w_mla_proj=bf(w_mla_proj),
        w_out=bf(w_out), ln1_g=row(ln1_g), ln1_b=row(ln1_b), w_up=bf(w_up), w_down=bf(w_down),
        ln2_g=row(ln2_g), ln2_b=row(ln2_b))


def _prompt_layer(x, w, alpha):
    b, l, d = x.shape
    cos, sin = _rope_tables(0, l)
    a_ssm, ssm, conv = _ssd_prompt(x, w)
    a_mla, ckv, kpe = _mla_prompt(x, cos, sin, w)
    y = _ffn(x.reshape(b * l, d), a_ssm.reshape(b * l, d), a_mla.reshape(b * l, d), w, alpha)
    return y.reshape(b, l, d), ckv, kpe, ssm.reshape(b, SSM_HEADS, SSM_HEAD_DIM, SSM_STATE), conv


def _sample_layer(x, cache_ckv, cache_kpe, state_ssm, state_conv, w, alpha):
    b, l, d = x.shape
    cos, sin = _rope_tables(cache_ckv.shape[1], l)
    a_ssm, ssm, conv = _ssd_sample(x, state_ssm, state_conv, w)
    a_mla, ckv, kpe = _mla_sample(x, cache_ckv, cache_kpe, cos, sin, w)
    y = _ffn(x.reshape(b * l, d), a_ssm, a_mla, w, alpha)
    return y.reshape(b, l, d), ckv, kpe, ssm.reshape(b, SSM_HEADS, SSM_HEAD_DIM, SSM_STATE), conv


def kernel(x_prompt, x_sample, cache_ckv, cache_kpe, state_ssm, state_conv, w_in, b_gate, conv_w, conv_b, dt_bias, a_log, d_skip, ssm_norm_g, w_ssm_proj, q_norm_g, w_q_b, kv_norm_g, w_uk, w_uv, w_mla_proj, w_out, ln1_g, ln1_b, w_up, w_down, ln2_g, ln2_b):
    depth = w_in.shape[0]
    alpha = (2 * depth) ** 0.25
    layer_params = (w_in, b_gate, conv_w, conv_b, dt_bias, a_log, d_skip, ssm_norm_g, w_ssm_proj, q_norm_g, w_q_b,
                    kv_norm_g, w_uk, w_uv, w_mla_proj, w_out, ln1_g, ln1_b, w_up, w_down, ln2_g, ln2_b)
    yp, ys = x_prompt, x_sample
    st_p, st_s = [], []
    for i in range(depth):
        w = _layer_weights(*(p[i] for p in layer_params))
        yp, *sp = _prompt_layer(yp, w, alpha)
        ys, *ss = _sample_layer(ys, cache_ckv[i], cache_kpe[i], state_ssm[i], state_conv[i], w, alpha)
        st_p.append(sp)
        st_s.append(ss)
    stack = lambda sts, k: jnp.stack([s[k] for s in sts])
    return (yp, ys, stack(st_p, 0), stack(st_p, 1), stack(st_p, 2), stack(st_p, 3),
            stack(st_s, 0), stack(st_s, 1), stack(st_s, 2), stack(st_s, 3))
```

```python
import functools
import math

import jax
import jax.numpy as jnp
from jax import lax
from jax.experimental import pallas as pl
from jax.experimental.pallas import tpu as pltpu

D_MODEL = 1024
D_INNER = 2 * D_MODEL
SSM_HEAD_DIM = 64
SSM_HEADS = D_INNER // SSM_HEAD_DIM
SSM_GROUPS = 8
HEADS_PER_GROUP = SSM_HEADS // SSM_GROUPS
GROUP_DIM = HEADS_PER_GROUP * SSM_HEAD_DIM
SSM_STATE = 128
CONV_WIDTH = 4
CONV_DIM = D_INNER + 2 * SSM_GROUPS * SSM_STATE
MLA_HEADS = 16
QK_NOPE = 128
QK_ROPE = 64
V_HEAD = 128
Q_RANK = 512
KV_RANK = 512
QK_LAT = KV_RANK + QK_ROPE
ROPE_THETA = 10000.0
ATTN_SCALE = (QK_NOPE + QK_ROPE) ** -0.5
D_FF = 4 * D_MODEL
RMS_EPS = 1e-6
LN_EPS = 1e-5
CHUNK = 64
CHUNK_SHIFT = 6

LANES = 128
SUBLANES = 8
VMEM_LIMIT = 62 * 1024 * 1024

SSD_Q = 128
SSD_ROWS = 256
SSD_PROJ_TILE = 512
CONV_HALO = (CONV_WIDTH - 1) * SUBLANES
SSD_SAMPLE_SEQS = 4
MLA_BLK = 256
DEC_TK = 1024
ROW_TILE = 256
FFN_ROWS = 512
FFN_SUB_ROWS = 256

BF16 = jnp.bfloat16
F32 = jnp.float32
NEG_INF = float("-inf")
LOG2_E = math.log2(math.e)


def _dot(a, b):
    return jnp.dot(a, b, preferred_element_type=F32)


def _dot_nt(a, b):
    return lax.dot_general(a, b, (((1,), (1,)), ((), ())), preferred_element_type=F32)


def _rms(t):
    return t * lax.rsqrt(jnp.mean(t * t, axis=-1, keepdims=True) + RMS_EPS)


def _layernorm(t, g, b):
    mu = jnp.mean(t, axis=-1, keepdims=True)
    d = t - mu
    var = jnp.mean(d * d, axis=-1, keepdims=True)
    return d * lax.rsqrt(var + LN_EPS) * g + b


def _silu(t):
    return t * jax.nn.sigmoid(t)


def _softplus(t):
    return jnp.maximum(t, 0.0) + jnp.log(1.0 + jnp.exp(-jnp.abs(t)))


def _const_spec(shape):
    zeros = (0,) * len(shape)
    return pl.BlockSpec(shape, lambda *_: zeros, pipeline_mode=pl.Buffered(1))


def _params(n_axes):
    return pltpu.CompilerParams(dimension_semantics=("arbitrary",) * n_axes, vmem_limit_bytes=VMEM_LIMIT)


def _expand_heads(v, e_ref):
    hi = v.astype(BF16).astype(F32)
    r1 = v - hi
    mid = r1.astype(BF16).astype(F32)
    lo = r1 - mid
    lane = lax.broadcasted_iota(jnp.int32, v.shape, 1)
    packed = jnp.where(lane < 32, hi,
                       jnp.where(lane < 64, pltpu.roll(mid, 32, 1),
                                 jnp.where(lane < 96, pltpu.roll(lo, 64, 1), 0.0)))
    return _dot(packed.astype(BF16), e_ref[...])


def _time_of_row(r, q):
    return (r & (SUBLANES - 1)) * (q // SUBLANES) + (r >> 3)


def _ssd_block(xc, z, dt, neg_a2, dskip, norm_g, e_ref, st_ref, bd_ref, side_jobs=None, interleaved=False):
    q = xc.shape[0]
    row = lax.broadcasted_iota(jnp.int32, (q, q), 0)
    col = lax.broadcasted_iota(jnp.int32, (q, q), 1)
    if interleaved:
        row, col = _time_of_row(row, q), _time_of_row(col, q)
    causal = row >= col
    da = dt * neg_a2
    acum = jnp.dot(causal.astype(F32), da, precision=lax.Precision.HIGHEST, preferred_element_type=F32)
    acum_t = acum.T
    dt_full = _expand_heads(dt, e_ref)
    acum_full = _expand_heads(acum, e_ref)
    decay_full = jnp.exp2(acum_full)
    alast_full = acum_full[q - 1:q, :]
    xs = xc[:, :D_INNER]
    xdt = xs * dt_full
    xdt_b = xdt.astype(BF16)
    xw_b = (xdt * jnp.exp2(alast_full - acum_full)).astype(BF16)
    state_decay = jnp.exp2(alast_full)
    bm_t = xc[:, D_INNER:D_INNER + SSM_GROUPS * SSM_STATE].T.astype(BF16)
    cm_b = xc[:, D_INNER + SSM_GROUPS * SSM_STATE:].astype(BF16)
    ys = []
    for g in range(SSM_GROUPS):
        if side_jobs:
            side_jobs.pop(0)()
        gsl = slice(g * GROUP_DIM, (g + 1) * GROUP_DIM)
        nsl = slice(g * SSM_STATE, (g + 1) * SSM_STATE)
        cg = cm_b[:, nsl]
        bg_t = bm_t[nsl, :]
        cb = _dot(cg, bg_t)
        ms = []
        for r in range(HEADS_PER_GROUP):
            h = g * HEADS_PER_GROUP + r
            hsl = slice(r * SSM_HEAD_DIM, (r + 1) * SSM_HEAD_DIM)
            bd_ref[g, r * q:(r + 1) * q, hsl] = xdt_b[:, g * GROUP_DIM + r * SSM_HEAD_DIM:g * GROUP_DIM + (r + 1) * SSM_HEAD_DIM]
            seg = acum[:, h:h + 1] - acum_t[h:h + 1, :]
            ms.append((cb * jnp.exp2(jnp.where(causal, seg, NEG_INF))).astype(BF16))
        yg = _dot(jnp.concatenate(ms, axis=1), bd_ref[g])
        st = st_ref[g]
        yg = yg + _dot(cg, st.astype(BF16)) * decay_full[:, gsl]
        st_ref[g] = st * state_decay[:, gsl] + _dot(bg_t, xw_b[:, gsl])
        ys.append(yg)
    y = jnp.concatenate(ys, axis=1) + xs * dskip
    yz = y * _silu(z)
    outs = []
    for g in range(SSM_GROUPS):
        outs.append(_rms(yz[:, g * GROUP_DIM:(g + 1) * GROUP_DIM]))
    return jnp.concatenate(outs, axis=1) * norm_g


def _conv_silu(xp_ref, rows, convw, convb):
    acc = convb
    for k in range(CONV_WIDTH):
        acc = acc + xp_ref[SUBLANES - (CONV_WIDTH - 1) + k:SUBLANES - (CONV_WIDTH - 1) + k + rows, :] * convw[k:k + 1, :]
    return _silu(acc)


def _interleave_rows(xb, q, inverse=False):
    row = lax.broadcasted_iota(jnp.int32, (q, q), 0)
    col = lax.broadcasted_iota(jnp.int32, (q, q), 1)
    hit = (row == _time_of_row(col, q)) if inverse else (col == _time_of_row(row, q))
    perm = jnp.where(hit, 1.0, 0.0).astype(BF16)
    return jnp.concatenate([_dot(perm, xb[b0:b0 + q]).astype(BF16) for b0 in range(0, xb.shape[0], q)], axis=0)


def _conv_silu_interleaved(xp_ref, blk, prev, convw, convb):
    q = xp_ref.shape[1] - CONV_HALO
    first = lax.broadcasted_iota(jnp.int32, (SUBLANES, CONV_DIM), 0) == 0
    for j in range(CONV_WIDTH - 1):
        js = slice(j * SUBLANES, (j + 1) * SUBLANES)
        cur_j = xp_ref[blk, q + j * SUBLANES:q + (j + 1) * SUBLANES, :]
        xp_ref[blk, js, :] = jnp.where(first, pltpu.roll(prev[js], 1, 0), pltpu.roll(cur_j, 1, 0))
    acc = convb
    for k in range(CONV_WIDTH):
        acc = acc + xp_ref[blk, k * SUBLANES:k * SUBLANES + q, :] * convw[k:k + 1, :]
    return _silu(acc)


def _ssd_project_jobs(get_xb, get_xb_tokens, wz_ref, wxbc_ref, wdt_ref, wg_ref, bg_ref, dtb_ref,
                      z_ref, xp_ref, dt_ref, gate_ref):
    tile = SSD_PROJ_TILE
    q = xp_ref.shape[1] - CONV_HALO

    def xbc_job(c0):
        def job():
            res = _dot(get_xb(), wxbc_ref[:, c0:c0 + tile])
            for b in range(xp_ref.shape[0]):
                xp_ref[b, CONV_HALO:, c0:c0 + tile] = res[b * q:(b + 1) * q]
        return job

    def z_job(c0):
        def job():
            z_ref[:, c0:c0 + tile] = _dot(get_xb(), wz_ref[:, c0:c0 + tile])
        return job

    def gate_job(c0):
        def job():
            gate_ref[:, c0:c0 + tile] = jax.nn.sigmoid(_dot(get_xb_tokens(), wg_ref[:, c0:c0 + tile])
                                                       + bg_ref[:, c0:c0 + tile])
        return job

    def dt_job():
        dt_ref[...] = _softplus(_dot(get_xb(), wdt_ref[...]) + dtb_ref[...])

    return ([xbc_job(c0) for c0 in range(0, CONV_DIM, tile)] + [z_job(c0) for c0 in range(0, D_INNER, tile)]
            + [gate_job(c0) for c0 in range(0, D_MODEL, tile)] + [dt_job])


def _ssd_prompt_kernel(x0_ref, xn_ref, wz_ref, wxbc_ref, wdt_ref, wg_ref, bg_ref, convw_ref, convb_ref, dtb_ref,
                       alog_ref, dskip_ref, ng_ref, e_ref, wproj_ref,
                       a_out_ref, ssm_out_ref, conv_out_ref,
                       st_ref, bd_ref, hist_ref, xbn_ref, z_a, xp_a, dt_a, gate_a, z_b, xp_b, dt_b, gate_b, *,
                       steps_per_row):
    g = pl.program_id(0)
    c = g % steps_per_row
    nblk = xp_a.shape[0]
    proj_w = (wz_ref, wxbc_ref, wdt_ref, wg_ref, bg_ref, dtb_ref)

    @pl.when(g == 0)
    def _():
        bd_ref[...] = jnp.zeros_like(bd_ref)
        xb0 = x0_ref[...].astype(BF16)
        xbn_ref[...] = _interleave_rows(xb0, SSD_Q)
        for job in _ssd_project_jobs(lambda: xbn_ref[...], lambda: xb0, *proj_w, z_a, xp_a, dt_a, gate_a):
            job()

    @pl.when(c == 0)
    def _():
        st_ref[...] = jnp.zeros_like(st_ref)
        hist_ref[...] = jnp.zeros_like(hist_ref)

    def step(cur, nxt):
        z_ref, xp_ref, dt_ref, gate_ref = cur
        xbn_ref[...] = _interleave_rows(xn_ref[...].astype(BF16), SSD_Q)
        jobs = _ssd_project_jobs(lambda: xbn_ref[...], lambda: xn_ref[...].astype(BF16), *proj_w, *nxt)
        neg_a2 = -jnp.exp(alog_ref[...]) * LOG2_E
        yn = []
        for s in range(nblk):
            sl = slice(s * SSD_Q, (s + 1) * SSD_Q)
            prev = hist_ref[...] if s == 0 else xp_ref[s - 1, SSD_Q:, :]
            xc = _conv_silu_interleaved(xp_ref, s, prev, convw_ref[...], convb_ref[...])
            yn.append(_ssd_block(xc, z_ref[sl, :], dt_ref[sl, :], neg_a2, dskip_ref[...], ng_ref[...], e_ref,
                                 st_ref, bd_ref, jobs, interleaved=True))
        hist_ref[...] = xp_ref[nblk - 1, SSD_Q:, :]
        for job in jobs:
            job()
        yn = _interleave_rows(jnp.concatenate(yn, axis=0).astype(BF16), SSD_Q, inverse=True)
        a_out_ref[...] = gate_ref[...] * _dot(yn, wproj_ref[...])

    slot_a, slot_b = (z_a, xp_a, dt_a, gate_a), (z_b, xp_b, dt_b, gate_b)

    @pl.when(g % 2 == 0)
    def _():
        step(slot_a, slot_b)

    @pl.when(g % 2 == 1)
    def _():
        step(slot_b, slot_a)

    @pl.when(c == steps_per_row - 1)
    def _():
        for j in range(CONV_WIDTH - 1):
            conv_out_ref[j:j + 1, :] = hist_ref[(j + 1) * SUBLANES - 1:(j + 1) * SUBLANES, :]
        for k in range(SSM_GROUPS):
            ssm_out_ref[k * GROUP_DIM:(k + 1) * GROUP_DIM, :] = st_ref[k].T


def _ssd_sample_kernel(x_ref, state_ref, cstate_ref, wz_ref, wxbc_ref, wdt_ref, wg_ref, bg_ref, convw_ref, convb_ref,
                       dtb_ref, alog_ref, dskip_ref, ng_ref, e_ref, wproj_ref,
                       a_out_ref, ssm_out_ref, conv_out_ref,
                       st_ref, bd_ref, xp_ref, z_scr, xbc_scr, dt_scr, xcp_scr, zp_scr, dtp_scr, yn_scr):
    nseq, seq = cstate_ref.shape[0], x_ref.shape[0] // cstate_ref.shape[0]
    xb = x_ref[...].astype(BF16)
    z_scr[...] = _dot(xb, wz_ref[...])
    xbc_scr[...] = _dot(xb, wxbc_ref[...])
    dt_scr[...] = _softplus(_dot(xb, wdt_ref[...]) + dtb_ref[...])
    gate = jax.nn.sigmoid(_dot(xb, wg_ref[...]) + bg_ref[...])
    xcp_scr[...] = jnp.zeros_like(xcp_scr)
    zp_scr[...] = jnp.zeros_like(zp_scr)
    dtp_scr[...] = jnp.zeros_like(dtp_scr)
    bd_ref[...] = jnp.zeros_like(bd_ref)
    neg_a2 = -jnp.exp(alog_ref[...]) * LOG2_E
    lo = SUBLANES - (CONV_WIDTH - 1)

    def body(s, carry):
        r0 = pl.multiple_of(s * seq, seq)
        xp_ref[lo:SUBLANES, :] = cstate_ref[s]
        xp_ref[SUBLANES:SUBLANES + seq, :] = xbc_scr[pl.ds(r0, seq), :]
        xcp_scr[0:seq, :] = _conv_silu(xp_ref, seq, convw_ref[...], convb_ref[...])
        conv_out_ref[s] = xp_ref[seq + lo:seq + SUBLANES, :]
        zp_scr[0:seq, :] = z_scr[pl.ds(r0, seq), :]
        dtp_scr[0:seq, :] = dt_scr[pl.ds(r0, seq), :]
        for g in range(SSM_GROUPS):
            st_ref[g] = state_ref[s, g * GROUP_DIM:(g + 1) * GROUP_DIM, :].T
        yn = _ssd_block(xcp_scr[...], zp_scr[...], dtp_scr[...], neg_a2, dskip_ref[...], ng_ref[...], e_ref, st_ref,
                        bd_ref)
        yn_scr[pl.ds(r0, seq), :] = yn[0:seq].astype(BF16)
        for g in range(SSM_GROUPS):
            ssm_out_ref[s, g * GROUP_DIM:(g + 1) * GROUP_DIM, :] = st_ref[g].T
        return carry

    lax.fori_loop(0, nseq, body, 0)
    a_out_ref[...] = gate * _dot(yn_scr[...], wproj_ref[...])


def _ssd_weight_specs(w):
    names = ("w_z", "w_xbc", "w_dt", "w_gs", "b_gs", "conv_w", "conv_b", "dt_bias", "a_log", "d_skip", "norm_g",
             "expand", "w_ssm_proj")
    arrs = [w[n] for n in names]
    return arrs, [_const_spec(a.shape) for a in arrs]


def _ssd_prompt(x, w):
    b, l, d = x.shape
    rows = min(SSD_ROWS, l)
    assert l % rows == 0 and rows % SSD_Q == 0
    arrs, specs = _ssd_weight_specs(w)
    nc = l // rows
    last = b * nc - 1
    slot = [pltpu.VMEM((rows, D_INNER), F32), pltpu.VMEM((rows // SSD_Q, CONV_HALO + SSD_Q, CONV_DIM), F32),
            pltpu.VMEM((rows, LANES), F32), pltpu.VMEM((rows, D_MODEL), F32)]

    def next_block(g):
        n = jnp.minimum(g + 1, last)
        return (n // nc, n % nc, 0)

    return pl.pallas_call(
        functools.partial(_ssd_prompt_kernel, steps_per_row=nc),
        grid=(b * nc,),
        in_specs=[pl.BlockSpec((None, rows, d), lambda g: (0, 0, 0)),
                  pl.BlockSpec((None, rows, d), next_block)] + specs,
        out_specs=[pl.BlockSpec((None, rows, D_MODEL), lambda g: (g // nc, g % nc, 0)),
                   pl.BlockSpec((None, D_INNER, SSM_STATE), lambda g: (g // nc, 0, 0)),
                   pl.BlockSpec((None, CONV_WIDTH - 1, CONV_DIM), lambda g: (g // nc, 0, 0))],
        out_shape=[jax.ShapeDtypeStruct((b, l, D_MODEL), F32),
                   jax.ShapeDtypeStruct((b, D_INNER, SSM_STATE), F32),
                   jax.ShapeDtypeStruct((b, CONV_WIDTH - 1, CONV_DIM), F32)],
        scratch_shapes=[pltpu.VMEM((SSM_GROUPS, SSM_STATE, GROUP_DIM), F32),
                        pltpu.VMEM((SSM_GROUPS, HEADS_PER_GROUP * SSD_Q, GROUP_DIM), BF16),
                        pltpu.VMEM((CONV_HALO, CONV_DIM), F32),
                        pltpu.VMEM((rows, d), BF16)] + slot + slot,
        compiler_params=_params(1),
        name="ssd_prompt",
    )(x, x, *arrs)


def _ssd_sample(x, state, cstate, w):
    b, l, d = x.shape
    nseq = min(SSD_SAMPLE_SEQS, b)
    assert b % nseq == 0 and l % SUBLANES == 0 and l <= SSD_Q
    rows = nseq * l
    arrs, specs = _ssd_weight_specs(w)
    a, ssm, conv = pl.pallas_call(
        _ssd_sample_kernel,
        grid=(b // nseq,),
        in_specs=[pl.BlockSpec((rows, d), lambda i: (i, 0)),
                  pl.BlockSpec((nseq, D_INNER, SSM_STATE), lambda i: (i, 0, 0)),
                  pl.BlockSpec((nseq, CONV_WIDTH - 1, CONV_DIM), lambda i: (i, 0, 0))] + specs,
        out_specs=[pl.BlockSpec((rows, D_MODEL), lambda i: (i, 0)),
                   pl.BlockSpec((nseq, D_INNER, SSM_STATE), lambda i: (i, 0, 0)),
                   pl.BlockSpec((nseq, CONV_WIDTH - 1, CONV_DIM), lambda i: (i, 0, 0))],
        out_shape=[jax.ShapeDtypeStruct((b * l, D_MODEL), F32),
                   jax.ShapeDtypeStruct((b, D_INNER, SSM_STATE), F32),
                   jax.ShapeDtypeStruct((b, CONV_WIDTH - 1, CONV_DIM), F32)],
        scratch_shapes=[pltpu.VMEM((SSM_GROUPS, SSM_STATE, GROUP_DIM), F32),
                        pltpu.VMEM((SSM_GROUPS, HEADS_PER_GROUP * SSD_Q, GROUP_DIM), BF16),
                        pltpu.VMEM((l + SUBLANES, CONV_DIM), F32),
                        pltpu.VMEM((rows, D_INNER), F32),
                        pltpu.VMEM((rows, CONV_DIM), F32),
                        pltpu.VMEM((rows, LANES), F32),
                        pltpu.VMEM((SSD_Q, CONV_DIM), F32),
                        pltpu.VMEM((SSD_Q, D_INNER), F32),
                        pltpu.VMEM((SSD_Q, LANES), F32),
                        pltpu.VMEM((rows, D_INNER), BF16)],
        compiler_params=_params(1),
        name="ssd_sample",
    )(x.reshape(b * l, d), state.reshape(b, D_INNER, SSM_STATE), cstate, *arrs)
    return a, ssm, conv


def _mla_keys(xb, cos, sin, wkva_ref, kvg_ref):
    kva = _dot(xb, wkva_ref[...])
    ckv = _rms(kva[:, :KV_RANK]) * kvg_ref[...]
    kpe = (kva[:, KV_RANK:KV_RANK + QK_ROPE] * cos[:, :QK_ROPE]
           + kva[:, KV_RANK + LANES:KV_RANK + LANES + QK_ROPE] * sin[:, :QK_ROPE])
    return ckv, kpe


def _mla_queries(xb, cos, sin, wqa_ref, qg_ref, wqb_ref, wukt_ref, store):
    qn = (_rms(_dot(xb, wqa_ref[...])) * qg_ref[...]).astype(BF16)
    q = _dot(qn, wqb_ref[...])
    nope_w = MLA_HEADS * QK_NOPE
    rope_w = MLA_HEADS * QK_ROPE
    for p in range(MLA_HEADS // 2):
        sl = slice(nope_w + p * LANES, nope_w + (p + 1) * LANES)
        sl_sw = slice(nope_w + rope_w + p * LANES, nope_w + rope_w + (p + 1) * LANES)
        pair = (q[:, sl] * cos + q[:, sl_sw] * sin) * ATTN_SCALE
        pair_hi = pltpu.roll(pair, QK_ROPE, 1)
        for h, pe in ((2 * p, pair), (2 * p + 1, pair_hi)):
            q_lat = _dot(q[:, h * QK_NOPE:(h + 1) * QK_NOPE].astype(BF16), wukt_ref[h]) * ATTN_SCALE
            store(h, q_lat.astype(BF16), pe[:, :QK_ROPE].astype(BF16))


def _lane_tile(t, width):
    return jnp.concatenate([t] * (width // LANES), axis=1) if width > LANES else t


def _softmax_update(s, v_b, m_scr, l_scr, acc_scr):
    m_prev = m_scr[...]
    m_new = jnp.maximum(m_prev, jnp.max(s, axis=-1, keepdims=True))
    alpha = jnp.exp(m_prev - m_new)
    p = jnp.exp(s - _lane_tile(m_new, s.shape[1]))
    p_lanes = p[:, 0:LANES]
    for c in range(1, s.shape[1] // LANES):
        p_lanes = p_lanes + p[:, c * LANES:(c + 1) * LANES]
    l_scr[...] = alpha * l_scr[...] + p_lanes
    acc_scr[...] = _lane_tile(alpha, acc_scr.shape[1]) * acc_scr[...] + _dot(p.astype(BF16), v_b)
    m_scr[...] = m_new


def _softmax_finish(l_scr, acc_scr):
    return acc_scr[...] / jnp.sum(l_scr[...], axis=-1, keepdims=True)


def _mla_out(o_of_head, xb, wuv_ref, wmla_ref, wg_ref, bg_ref, o_scr):
    for h in range(MLA_HEADS):
        o_scr[:, h * V_HEAD:(h + 1) * V_HEAD] = _dot(o_of_head(h), wuv_ref[h]).astype(BF16)
    gate = jax.nn.sigmoid(_dot(xb, wg_ref[...]) + bg_ref[...])
    return gate * _dot(o_scr[...], wmla_ref[...])


def _mla_pre_kernel(x_ref, cos_ref, sin_ref, wqa_ref, qg_ref, wkva_ref, kvg_ref, qn_out_ref, ckv_out_ref, kpe_out_ref):
    xb = x_ref[...].astype(BF16)
    ckv, kpe = _mla_keys(xb, cos_ref[...], sin_ref[...], wkva_ref, kvg_ref)
    ckv_out_ref[...] = ckv
    kpe_out_ref[...] = kpe
    qn_out_ref[...] = (_rms(_dot(xb, wqa_ref[...])) * qg_ref[...]).astype(BF16)


def _mla_prompt_attn_kernel(qn_ref, ckv_ref, kpe_ref, cos_ref, sin_ref, wqp_ref, wkv_ref, o_ref,
                            qa_scr, qb_scr, ka_scr, kb_scr, vbd_scr, m_scr, l_scr, acc_scr):
    l = qn_ref.shape[0]
    g = min(MLA_BLK, l)
    nb = l // g
    half = QK_ROPE
    zpad = jnp.zeros((l, LANES - half), BF16)
    kv = _dot(ckv_ref[...].astype(BF16), wkv_ref[...])
    kpe_b = kpe_ref[...].astype(BF16)
    for scr, c0 in ((ka_scr, 0), (kb_scr, QK_NOPE)):
        scr[:, 0:QK_NOPE] = kv[:, c0:c0 + QK_NOPE].astype(BF16)
        scr[:, QK_NOPE:QK_NOPE + half] = kpe_b
        scr[:, QK_NOPE + half:] = zpad
    zv = jnp.zeros((g, V_HEAD), BF16)
    for j in range(nb):
        rs = slice(j * g, (j + 1) * g)
        vbd_scr[j, 0:g, 0:V_HEAD] = kv[rs, 2 * QK_NOPE:2 * QK_NOPE + V_HEAD].astype(BF16)
        vbd_scr[j, 0:g, V_HEAD:] = zv
        vbd_scr[j, g:, 0:V_HEAD] = zv
        vbd_scr[j, g:, V_HEAD:] = kv[rs, 2 * QK_NOPE + V_HEAD:].astype(BF16)
    q = _dot(qn_ref[...], wqp_ref[...])
    c = ATTN_SCALE * LOG2_E
    pair = (q[:, 2 * QK_NOPE:3 * QK_NOPE] * cos_ref[...] + q[:, 3 * QK_NOPE:] * sin_ref[...]) * c
    pair_hi = pltpu.roll(pair, half, 1)
    for scr, c0, pe in ((qa_scr, 0, pair), (qb_scr, QK_NOPE, pair_hi)):
        scr[:, 0:QK_NOPE] = (q[:, c0:c0 + QK_NOPE] * c).astype(BF16)
        scr[:, QK_NOPE:QK_NOPE + half] = pe[:, :half].astype(BF16)
        scr[:, QK_NOPE + half:] = zpad
    row = lax.broadcasted_iota(jnp.int32, (g, g), 0)
    col = lax.broadcasted_iota(jnp.int32, (g, g), 1)
    diag_visible = (row >> CHUNK_SHIFT) >= (col >> CHUNK_SHIFT)
    for j in range(nb):
        r0 = j * g
        ps, alphas = [], []
        for hi, (q_scr, k_scr) in enumerate(((qa_scr, ka_scr), (qb_scr, kb_scr))):
            s = _dot_nt(q_scr[r0:, :], k_scr[r0:r0 + g, :])
            top = jnp.where(diag_visible, s[:g], NEG_INF)
            s = top if j == nb - 1 else jnp.concatenate([top, s[g:]], axis=0)
            m_cur = jnp.max(s, axis=-1, keepdims=True)
            if j == 0:
                m_new = jnp.broadcast_to(m_cur, (l, LANES))
            else:
                m_prev = m_scr[hi, r0:, :]
                m_new = jnp.maximum(m_prev, m_cur)
                alphas.append(jnp.exp2(m_prev - m_new))
            p = jnp.exp2(s - _lane_tile(m_new, g))
            p_lanes = p[:, 0:LANES]
            for cc in range(1, g // LANES):
                p_lanes = p_lanes + p[:, cc * LANES:(cc + 1) * LANES]
            l_scr[hi, r0:, :] = p_lanes if j == 0 else alphas[-1] * l_scr[hi, r0:, :] + p_lanes
            m_scr[hi, r0:, :] = m_new
            ps.append(p.astype(BF16))
        pv = _dot(jnp.concatenate(ps, axis=1), vbd_scr[j])
        if j == 0:
            acc_scr[...] = pv
        else:
            acc_scr[r0:, :] = jnp.concatenate(alphas, axis=1) * acc_scr[r0:, :] + pv
    inv = [1.0 / jnp.sum(l_scr[hi], axis=-1, keepdims=True) for hi in range(2)]
    scale = jnp.concatenate([jnp.broadcast_to(t, (l, V_HEAD)) for t in inv], axis=1)
    o_ref[...] = (acc_scr[...] * scale).astype(BF16)


def _mla_post_kernel(o_ref, x_ref, wmla_ref, wg_ref, bg_ref, a_out_ref):
    gate = jax.nn.sigmoid(_dot(x_ref[...].astype(BF16), wg_ref[...]) + bg_ref[...])
    a_out_ref[...] = gate * _dot(o_ref[...], wmla_ref[...])


def _mla_prompt(x, cos, sin, w):
    b, l, d = x.shape
    g = min(MLA_BLK, l)
    tm = min(FFN_ROWS, l)
    npair = MLA_HEADS // 2
    assert l % g == 0 and g % CHUNK == 0 and g % LANES == 0 and l % tm == 0
    nt = l // tm
    pre = [w[n] for n in ("w_qa", "q_norm_g", "w_kva", "kv_norm_g")]
    qn, ckv, kpe = pl.pallas_call(
        _mla_pre_kernel,
        grid=(b, nt),
        in_specs=[pl.BlockSpec((None, tm, d), lambda i, c: (i, c, 0)),
                  pl.BlockSpec((tm, LANES), lambda i, c: (c, 0)),
                  pl.BlockSpec((tm, LANES), lambda i, c: (c, 0))] + [_const_spec(a.shape) for a in pre],
        out_specs=[pl.BlockSpec((None, tm, Q_RANK), lambda i, c: (i, c, 0)),
                   pl.BlockSpec((None, tm, KV_RANK), lambda i, c: (i, c, 0)),
                   pl.BlockSpec((None, tm, QK_ROPE), lambda i, c: (i, c, 0))],
        out_shape=[jax.ShapeDtypeStruct((b, l, Q_RANK), BF16),
                   jax.ShapeDtypeStruct((b, l, KV_RANK), F32),
                   jax.ShapeDtypeStruct((b, l, QK_ROPE), F32)],
        compiler_params=_params(2),
        name="mla_pre",
    )(x, cos, sin, *pre)
    o = pl.pallas_call(
        _mla_prompt_attn_kernel,
        grid=(b, npair),
        in_specs=[pl.BlockSpec((None, l, Q_RANK), lambda i, p: (i, 0, 0)),
                  pl.BlockSpec((None, l, KV_RANK), lambda i, p: (i, 0, 0)),
                  pl.BlockSpec((None, l, QK_ROPE), lambda i, p: (i, 0, 0)),
                  _const_spec(cos.shape), _const_spec(sin.shape),
                  pl.BlockSpec((None, Q_RANK, 4 * QK_NOPE), lambda i, p: (p, 0, 0)),
                  pl.BlockSpec((None, KV_RANK, 2 * QK_NOPE + 2 * V_HEAD), lambda i, p: (p, 0, 0))],
        out_specs=pl.BlockSpec((None, l, 2 * V_HEAD), lambda i, p: (i, 0, p)),
        out_shape=jax.ShapeDtypeStruct((b, l, MLA_HEADS * V_HEAD), BF16),
        scratch_shapes=[pltpu.VMEM((l, 2 * LANES), BF16)] * 4
        + [pltpu.VMEM((l // g, 2 * g, 2 * V_HEAD), BF16),
           pltpu.VMEM((2, l, LANES), F32),
           pltpu.VMEM((2, l, LANES), F32),
           pltpu.VMEM((l, 2 * V_HEAD), F32)],
        compiler_params=_params(2),
        name="mla_prompt_attn",
    )(qn, ckv, kpe, cos, sin, w["w_qpair"], w["w_kvpair"])
    post = [w[n] for n in ("w_mla_proj", "w_gm", "b_gm")]
    a = pl.pallas_call(
        _mla_post_kernel,
        grid=(b, nt),
        in_specs=[pl.BlockSpec((None, tm, MLA_HEADS * V_HEAD), lambda i, c: (i, c, 0)),
                  pl.BlockSpec((None, tm, d), lambda i, c: (i, c, 0))] + [_const_spec(t.shape) for t in post],
        out_specs=pl.BlockSpec((None, tm, D_MODEL), lambda i, c: (i, c, 0)),
        out_shape=jax.ShapeDtypeStruct((b, l, D_MODEL), F32),
        compiler_params=_params(2),
        name="mla_post",
    )(o, x, *post)
    return a, ckv, kpe


def _mla_dec_pre_kernel(x_ref, cos_ref, sin_ref, wqa_ref, qg_ref, wqb_ref, wukt_ref, wkva_ref, kvg_ref,
                        q_out_ref, ckv_out_ref, kpe_out_ref):
    xb = x_ref[...].astype(BF16)
    cos, sin = cos_ref[...], sin_ref[...]
    ckv, kpe = _mla_keys(xb, cos, sin, wkva_ref, kvg_ref)
    ckv_out_ref[...] = ckv
    kpe_out_ref[...] = kpe

    def store(h, q_lat, q_pe):
        q_out_ref[h, :, 0:KV_RANK] = q_lat
        q_out_ref[h, :, KV_RANK:QK_LAT] = q_pe

    _mla_queries(xb, cos, sin, wqa_ref, qg_ref, wqb_ref, wukt_ref, store)


def _mla_dec_attn_kernel(q_ref, cckv_ref, ckpe_t_ref, nckv_ref, nkpe_ref, o_ref,
                         m_scr, l_scr, acc_scr, nk_scr, *, past_len):
    j = pl.program_id(1)
    seq = q_ref.shape[1]
    rows = MLA_HEADS * seq

    @pl.when(j == 0)
    def _():
        m_scr[...] = jnp.full_like(m_scr, NEG_INF)
        l_scr[...] = jnp.zeros_like(l_scr)
        acc_scr[...] = jnp.zeros_like(acc_scr)

    q = q_ref[...].reshape(rows, QK_LAT)
    ck = cckv_ref[...].astype(BF16)
    kp_t = ckpe_t_ref[...].astype(BF16)
    s = _dot_nt(q[:, :KV_RANK], ck) + _dot(q[:, KV_RANK:], kp_t)
    _softmax_update(s, ck, m_scr, l_scr, acc_scr)

    @pl.when(j == pl.num_programs(1) - 1)
    def _():
        nk_scr[...] = jnp.zeros_like(nk_scr)
        nk_scr[0:seq, 0:KV_RANK] = nckv_ref[...].astype(BF16)
        nk_scr[0:seq, KV_RANK:QK_LAT] = nkpe_ref[...].astype(BF16)
        nk = nk_scr[...]
        s2 = _dot_nt(q, nk)
        row = lax.broadcasted_iota(jnp.int32, (rows, 1), 0)
        q_chunk = (past_len + (row & (seq - 1))) >> CHUNK_SHIFT
        k_lane = lax.broadcasted_iota(jnp.int32, (1, LANES), 1)
        visible = (k_lane < seq) & (((past_len + k_lane) >> CHUNK_SHIFT) <= q_chunk)
        _softmax_update(jnp.where(visible, s2, NEG_INF), nk[:, :KV_RANK], m_scr, l_scr, acc_scr)
        o_ref[...] = _softmax_finish(l_scr, acc_scr).astype(BF16).reshape(MLA_HEADS, seq, KV_RANK)


def _mla_dec_post_kernel(o_ref, x_ref, wuv_ref, wmla_ref, wg_ref, bg_ref, a_out_ref, o_scr):
    xb = x_ref[...].astype(BF16)
    a_out_ref[...] = _mla_out(lambda h: o_ref[h], xb, wuv_ref, wmla_ref, wg_ref, bg_ref, o_scr)


def _mla_sample(x, cache_ckv, cache_kpe, cos, sin, w):
    b, l, d = x.shape
    t = b * l
    past_len = cache_ckv.shape[1]
    tm = min(ROW_TILE, t)
    tk = min(DEC_TK, past_len)
    assert t % tm == 0 and past_len % tk == 0 and (l & (l - 1)) == 0 and l <= LANES and l % 16 == 0
    x2 = x.reshape(t, d)
    pre_names = ("w_qa", "q_norm_g", "w_qb", "w_ukt", "w_kva", "kv_norm_g")
    pre = [w[n] for n in pre_names]
    q, ckv, kpe = pl.pallas_call(
        _mla_dec_pre_kernel,
        grid=(t // tm,),
        in_specs=[pl.BlockSpec((tm, d), lambda i: (i, 0)),
                  pl.BlockSpec((tm, LANES), lambda i: (i, 0)),
                  pl.BlockSpec((tm, LANES), lambda i: (i, 0))] + [_const_spec(a.shape) for a in pre],
        out_specs=[pl.BlockSpec((MLA_HEADS, tm, QK_LAT), lambda i: (0, i, 0)),
                   pl.BlockSpec((tm, KV_RANK), lambda i: (i, 0)),
                   pl.BlockSpec((tm, QK_ROPE), lambda i: (i, 0))],
        out_shape=[jax.ShapeDtypeStruct((MLA_HEADS, t, QK_LAT), BF16),
                   jax.ShapeDtypeStruct((t, KV_RANK), F32),
                   jax.ShapeDtypeStruct((t, QK_ROPE), F32)],
        compiler_params=_params(1),
        name="mla_dec_pre",
    )(x2, jnp.tile(cos, (b, 1)), jnp.tile(sin, (b, 1)), *pre)
    rows = MLA_HEADS * l
    o = pl.pallas_call(
        functools.partial(_mla_dec_attn_kernel, past_len=past_len),
        grid=(b, past_len // tk),
        in_specs=[pl.BlockSpec((MLA_HEADS, l, QK_LAT), lambda i, j: (0, i, 0)),
                  pl.BlockSpec((None, tk, KV_RANK), lambda i, j: (i, j, 0)),
                  pl.BlockSpec((None, QK_ROPE, tk), lambda i, j: (i, 0, j)),
                  pl.BlockSpec((l, KV_RANK), lambda i, j: (i, 0)),
                  pl.BlockSpec((l, QK_ROPE), lambda i, j: (i, 0))],
        out_specs=pl.BlockSpec((MLA_HEADS, l, KV_RANK), lambda i, j: (0, i, 0)),
        out_shape=jax.ShapeDtypeStruct((MLA_HEADS, t, KV_RANK), BF16),
        scratch_shapes=[pltpu.VMEM((rows, LANES), F32),
                        pltpu.VMEM((rows, LANES), F32),
                        pltpu.VMEM((rows, KV_RANK), F32),
                        pltpu.VMEM((LANES, QK_LAT), BF16)],
        compiler_params=_params(2),
        name="mla_dec_attn",
    )(q, cache_ckv, jnp.swapaxes(cache_kpe, 1, 2), ckv, kpe)
    post_names = ("w_uv", "w_mla_proj", "w_gm", "b_gm")
    post = [w[n] for n in post_names]
    a = pl.pallas_call(
        _mla_dec_post_kernel,
        grid=(t // tm,),
        in_specs=[pl.BlockSpec((MLA_HEADS, tm, KV_RANK), lambda i: (0, i, 0)),
                  pl.BlockSpec((tm, d), lambda i: (i, 0))] + [_const_spec(a.shape) for a in post],
        out_specs=pl.BlockSpec((tm, D_MODEL), lambda i: (i, 0)),
        out_shape=jax.ShapeDtypeStruct((t, D_MODEL), F32),
        scratch_shapes=[pltpu.VMEM((tm, MLA_HEADS * V_HEAD), BF16)],
        compiler_params=_params(1),
        name="mla_dec_post",
    )(o, x2, *post)
    return a, ckv.reshape(b, l, KV_RANK), kpe.reshape(b, l, QK_ROPE)


def _ffn_kernel(x_ref, as_ref, am_ref, wout_ref, g1_ref, b1_ref, wup_ref, wdown_ref, g2_ref, b2_ref, y_ref, *, alpha):
    sub = min(FFN_SUB_ROWS, x_ref.shape[0])
    for r0 in range(0, x_ref.shape[0], sub):
        rs = pl.ds(r0, sub)
        mix = _dot((as_ref[rs, :] + am_ref[rs, :]).astype(BF16), wout_ref[...])
        h = _layernorm(alpha * x_ref[rs, :] + mix, g1_ref[...], b1_ref[...])
        up = jnp.maximum(_dot(h.astype(BF16), wup_ref[...]), 0.0)
        ff = _dot((up * up).astype(BF16), wdown_ref[...])
        y_ref[rs, :] = _layernorm(alpha * h + ff, g2_ref[...], b2_ref[...])


def _ffn(x2, a_ssm, a_mla, w, alpha):
    t, d = x2.shape
    tm = min(FFN_ROWS, t)
    assert t % tm == 0 and tm % min(FFN_SUB_ROWS, tm) == 0
    names = ("w_out", "ln1_g", "ln1_b", "w_up", "w_down", "ln2_g", "ln2_b")
    arrs = [w[n] for n in names]
    row_spec = pl.BlockSpec((tm, d), lambda i: (i, 0))
    return pl.pallas_call(
        functools.partial(_ffn_kernel, alpha=alpha),
        grid=(t // tm,),
        in_specs=[row_spec, row_spec, row_spec] + [_const_spec(a.shape) for a in arrs],
        out_specs=row_spec,
        out_shape=jax.ShapeDtypeStruct((t, d), F32),
        compiler_params=_params(1),
        name="ffn",
    )(x2, a_ssm, a_mla, *arrs)


def _rope_tables(pos0, n):
    inv = ROPE_THETA ** (-jnp.arange(0, QK_ROPE, 2, dtype=F32) / QK_ROPE)
    ang = (pos0 + jnp.arange(n, dtype=jnp.int32)).astype(F32)[:, None] * inv[None, :]
    cos, sin = jnp.cos(ang), jnp.sin(ang)
    cos2 = jnp.concatenate([cos, cos], axis=-1)
    sin2 = jnp.concatenate([-sin, sin], axis=-1)
    return jnp.tile(cos2, (1, LANES // QK_ROPE)), jnp.tile(sin2, (1, LANES // QK_ROPE))


def _swap_halves(t):
    half = t.shape[-1] // 2
    return jnp.concatenate([t[..., half:], t[..., :half]], axis=-1)


def _layer_weights(w_in, b_gate, conv_w, conv_b, dt_bias, a_log, d_skip, ssm_norm_g, w_ssm_proj, q_norm_g, w_q_b,
                   kv_norm_g, w_uk, w_uv, w_mla_proj, w_out, ln1_g, ln1_b, w_up, w_down, ln2_g, ln2_b):
    d = w_in.shape[0]
    o_z = 2 * D_MODEL
    o_xbc = o_z + D_INNER
    o_dt = o_xbc + CONV_DIM
    o_qa = o_dt + SSM_HEADS
    o_kva = o_qa + Q_RANK
    bf = lambda t: t.astype(BF16)
    row = lambda t: t.reshape(1, -1)
    lane_pad = lambda t: jnp.pad(t, ((0, 0), (0, LANES - t.shape[-1])))
    w_kv = w_in[:, o_kva:]
    w_kpe = w_kv[:, KV_RANK:]
    zpad = jnp.zeros((d, LANES - QK_ROPE), w_in.dtype)
    qb = w_q_b.reshape(Q_RANK, MLA_HEADS, QK_NOPE + QK_ROPE)
    qb_rope = qb[:, :, QK_NOPE:]
    npair = MLA_HEADS // 2
    k = jnp.arange(LANES)[:, None]
    c = jnp.arange(D_INNER)[None, :]
    expand = ((k < 3 * SSM_HEADS) & ((k % SSM_HEADS) == (c // SSM_HEAD_DIM))).astype(BF16)
    return dict(
        w_gs=bf(w_in[:, :D_MODEL]), b_gs=row(b_gate[:D_MODEL]),
        w_gm=bf(w_in[:, D_MODEL:o_z]), b_gm=row(b_gate[D_MODEL:]),
        w_z=bf(w_in[:, o_z:o_xbc]), w_xbc=bf(w_in[:, o_xbc:o_dt]), w_dt=bf(lane_pad(w_in[:, o_dt:o_qa])),
        conv_w=conv_w, conv_b=row(conv_b), dt_bias=lane_pad(row(dt_bias)), a_log=lane_pad(row(a_log)),
        d_skip=row(jnp.repeat(d_skip, SSM_HEAD_DIM)), norm_g=row(ssm_norm_g), expand=expand,
        w_ssm_proj=bf(w_ssm_proj),
        w_qa=bf(w_in[:, o_qa:o_kva]), q_norm_g=row(q_norm_g),
        w_qb=bf(jnp.concatenate([qb[:, :, :QK_NOPE].reshape(Q_RANK, -1), qb_rope.reshape(Q_RANK, -1),
                                 _swap_halves(qb_rope).reshape(Q_RANK, -1)], axis=1)),
        w_ukt=bf(jnp.transpose(w_uk, (1, 2, 0))),
        w_qpair=bf(jnp.transpose(jnp.concatenate(
            [qb[:, :, :QK_NOPE].reshape(Q_RANK, npair, 2 * QK_NOPE), qb_rope.reshape(Q_RANK, npair, 2 * QK_ROPE),
             _swap_halves(qb_rope).reshape(Q_RANK, npair, 2 * QK_ROPE)], axis=2), (1, 0, 2))),
        w_kvpair=bf(jnp.transpose(jnp.concatenate(
            [w_uk.reshape(KV_RANK, npair, 2 * QK_NOPE), w_uv.reshape(KV_RANK, npair, 2 * V_HEAD)], axis=2), (1, 0, 2))),
        w_kva=bf(jnp.concatenate([w_kv[:, :KV_RANK], w_kpe, zpad, _swap_halves(w_kpe), zpad], axis=1)),
        kv_norm_g=row(kv_norm_g), w_uv=bf(jnp.transpose(w_uv, (1, 0, 2))), w_mla_proj=bf(w_mla_proj),
        w_out=bf(w_out), ln1_g=row(ln1_g), ln1_b=row(ln1_b), w_up=bf(w_up), w_down=bf(w_down),
        ln2_g=row(ln2_g), ln2_b=row(ln2_b))


def _prompt_layer(x, w, alpha):
    b, l, d = x.shape
    cos, sin = _rope_tables(0, l)
    a_ssm, ssm, conv = _ssd_prompt(x, w)
    a_mla, ckv, kpe = _mla_prompt(x, cos, sin, w)
    y = _ffn(x.reshape(b * l, d), a_ssm.reshape(b * l, d), a_mla.reshape(b * l, d), w, alpha)
    return y.reshape(b, l, d), ckv, kpe, ssm.reshape(b, SSM_HEADS, SSM_HEAD_DIM, SSM_STATE), conv


def _sample_layer(x, cache_ckv, cache_kpe, state_ssm, state_conv, w, alpha):
    b, l, d = x.shape
    cos, sin = _rope_tables(cache_ckv.shape[1], l)
    a_ssm, ssm, conv = _ssd_sample(x, state_ssm, state_conv, w)
    a_mla, ckv, kpe = _mla_sample(x, cache_ckv, cache_kpe, cos, sin, w)
    y = _ffn(x.reshape(b * l, d), a_ssm, a_mla, w, alpha)
    return y.reshape(b, l, d), ckv, kpe, ssm.reshape(b, SSM_HEADS, SSM_HEAD_DIM, SSM_STATE), conv


def kernel(x_prompt, x_sample, cache_ckv, cache_kpe, state_ssm, state_conv, w_in, b_gate, conv_w, conv_b, dt_bias, a_log, d_skip, ssm_norm_g, w_ssm_proj, q_norm_g, w_q_b, kv_norm_g, w_uk, w_uv, w_mla_proj, w_out, ln1_g, ln1_b, w_up, w_down, ln2_g, ln2_b):
    depth = w_in.shape[0]
    alpha = (2 * depth) ** 0.25
    layer_params = (w_in, b_gate, conv_w, conv_b, dt_bias, a_log, d_skip, ssm_norm_g, w_ssm_proj, q_norm_g, w_q_b,
                    kv_norm_g, w_uk, w_uv, w_mla_proj, w_out, ln1_g, ln1_b, w_up, w_down, ln2_g, ln2_b)
    yp, ys = x_prompt, x_sample
    st_p, st_s = [], []
    for i in range(depth):
        w = _layer_weights(*(p[i] for p in layer_params))
        yp, *sp = _prompt_layer(yp, w, alpha)
        ys, *ss = _sample_layer(ys, cache_ckv[i], cache_kpe[i], state_ssm[i], state_conv[i], w, alpha)
        st_p.append(sp)
        st_s.append(ss)
    stack = lambda sts, k: jnp.stack([s[k] for s in sts])
    return (yp, ys, stack(st_p, 0), stack(st_p, 1), stack(st_p, 2), stack(st_p, 3),
            stack(st_s, 0), stack(st_s, 1), stack(st_s, 2), stack(st_s, 3))
```

```python
import functools
import math

import jax
import jax.numpy as jnp
from jax import lax
from jax.experimental import pallas as pl
from jax.experimental.pallas import tpu as pltpu

D_MODEL = 1024
D_INNER = 2 * D_MODEL
SSM_HEAD_DIM = 64
SSM_HEADS = D_INNER // SSM_HEAD_DIM
SSM_GROUPS = 8
HEADS_PER_GROUP = SSM_HEADS // SSM_GROUPS
GROUP_DIM = HEADS_PER_GROUP * SSM_HEAD_DIM
SSM_STATE = 128
CONV_WIDTH = 4
CONV_DIM = D_INNER + 2 * SSM_GROUPS * SSM_STATE
MLA_HEADS = 16
QK_NOPE = 128
QK_ROPE = 64
V_HEAD = 128
Q_RANK = 512
KV_RANK = 512
QK_LAT = KV_RANK + QK_ROPE
ROPE_THETA = 10000.0
ATTN_SCALE = (QK_NOPE + QK_ROPE) ** -0.5
D_FF = 4 * D_MODEL
RMS_EPS = 1e-6
LN_EPS = 1e-5
CHUNK = 64
CHUNK_SHIFT = 6

LANES = 128
SUBLANES = 8
VMEM_LIMIT = 62 * 1024 * 1024

SSD_Q = 128
SSD_ROWS = 256
SSD_PROJ_TILE = 512
CONV_HALO = (CONV_WIDTH - 1) * SUBLANES
SSD_SAMPLE_SEQS = 4
MLA_BLK = 256
DEC_TK = 2048
ROW_TILE = 256
MLA_ROWS = 1024
FFN_ROWS = 512
FFN_SUB_ROWS = 512

BF16 = jnp.bfloat16
F32 = jnp.float32
NEG_INF = float("-inf")
LOG2_E = math.log2(math.e)


def _dot(a, b):
    return jnp.dot(a, b, preferred_element_type=F32)


def _dot_nt(a, b):
    return lax.dot_general(a, b, (((1,), (1,)), ((), ())), preferred_element_type=F32)


def _rms(t):
    return t * lax.rsqrt(jnp.mean(t * t, axis=-1, keepdims=True) + RMS_EPS)


def _layernorm(t, g, b):
    mu = jnp.mean(t, axis=-1, keepdims=True)
    d = t - mu
    var = jnp.mean(d * d, axis=-1, keepdims=True)
    return d * lax.rsqrt(var + LN_EPS) * g + b


def _silu(t):
    return t * jax.nn.sigmoid(t)


def _softplus(t):
    return jnp.maximum(t, 0.0) + jnp.log(1.0 + jnp.exp(-jnp.abs(t)))


def _const_spec(shape):
    zeros = (0,) * len(shape)
    return pl.BlockSpec(shape, lambda *_: zeros, pipeline_mode=pl.Buffered(1))


def _params(n_axes):
    return pltpu.CompilerParams(dimension_semantics=("arbitrary",) * n_axes, vmem_limit_bytes=VMEM_LIMIT)


def _expand_heads(v, e_ref):
    hi = v.astype(BF16).astype(F32)
    r1 = v - hi
    mid = r1.astype(BF16).astype(F32)
    lo = r1 - mid
    lane = lax.broadcasted_iota(jnp.int32, v.shape, 1)
    packed = jnp.where(lane < 32, hi,
                       jnp.where(lane < 64, pltpu.roll(mid, 32, 1),
                                 jnp.where(lane < 96, pltpu.roll(lo, 64, 1), 0.0)))
    return _dot(packed.astype(BF16), e_ref[...])


def _time_of_row(r, q):
    return (r & (SUBLANES - 1)) * (q // SUBLANES) + (r >> 3)


def _ssd_block(xc, z, dt, neg_a2, dskip, norm_g, e_ref, st_ref, bd_ref, side_jobs=None, interleaved=False):
    q = xc.shape[0]
    row = lax.broadcasted_iota(jnp.int32, (q, q), 0)
    col = lax.broadcasted_iota(jnp.int32, (q, q), 1)
    if interleaved:
        row, col = _time_of_row(row, q), _time_of_row(col, q)
    causal = row >= col
    da = dt * neg_a2
    acum = jnp.dot(causal.astype(F32), da, precision=lax.Precision.HIGHEST, preferred_element_type=F32)
    acum_t = acum.T
    dt_full = _expand_heads(dt, e_ref)
    acum_full = _expand_heads(acum, e_ref)
    decay_full = jnp.exp2(acum_full)
    alast_full = acum_full[q - 1:q, :]
    xs = xc[:, :D_INNER]
    xdt = xs * dt_full
    xdt_b = xdt.astype(BF16)
    xw_b = (xdt * jnp.exp2(alast_full - acum_full)).astype(BF16)
    state_decay = jnp.exp2(alast_full)
    bm_t = xc[:, D_INNER:D_INNER + SSM_GROUPS * SSM_STATE].T.astype(BF16)
    cm_b = xc[:, D_INNER + SSM_GROUPS * SSM_STATE:].astype(BF16)
    ys = []
    for g in range(SSM_GROUPS):
        if side_jobs:
            side_jobs.pop(0)()
        gsl = slice(g * GROUP_DIM, (g + 1) * GROUP_DIM)
        nsl = slice(g * SSM_STATE, (g + 1) * SSM_STATE)
        cg = cm_b[:, nsl]
        bg_t = bm_t[nsl, :]
        cb = _dot(cg, bg_t)
        ms = []
        for r in range(HEADS_PER_GROUP):
            h = g * HEADS_PER_GROUP + r
            hsl = slice(r * SSM_HEAD_DIM, (r + 1) * SSM_HEAD_DIM)
            bd_ref[g, r * q:(r + 1) * q, hsl] = xdt_b[:, g * GROUP_DIM + r * SSM_HEAD_DIM:g * GROUP_DIM + (r + 1) * SSM_HEAD_DIM]
            seg = acum[:, h:h + 1] - acum_t[h:h + 1, :]
            ms.append((cb * jnp.exp2(jnp.where(causal, seg, NEG_INF))).astype(BF16))
        yg = _dot(jnp.concatenate(ms, axis=1), bd_ref[g])
        st = st_ref[g]
        yg = yg + _dot(cg, st.astype(BF16)) * decay_full[:, gsl]
        st_ref[g] = st * state_decay[:, gsl] + _dot(bg_t, xw_b[:, gsl])
        ys.append(yg)
    y = jnp.concatenate(ys, axis=1) + xs * dskip
    yz = y * _silu(z)
    outs = []
    for g in range(SSM_GROUPS):
        outs.append(_rms(yz[:, g * GROUP_DIM:(g + 1) * GROUP_DIM]))
    return jnp.concatenate(outs, axis=1) * norm_g


def _conv_silu(xp_ref, rows, convw, convb):
    acc = convb
    for k in range(CONV_WIDTH):
        acc = acc + xp_ref[SUBLANES - (CONV_WIDTH - 1) + k:SUBLANES - (CONV_WIDTH - 1) + k + rows, :] * convw[k:k + 1, :]
    return _silu(acc)


def _interleave_rows(xb, q, inverse=False):
    row = lax.broadcasted_iota(jnp.int32, (q, q), 0)
    col = lax.broadcasted_iota(jnp.int32, (q, q), 1)
    hit = (row == _time_of_row(col, q)) if inverse else (col == _time_of_row(row, q))
    perm = jnp.where(hit, 1.0, 0.0).astype(BF16)
    return jnp.concatenate([_dot(perm, xb[b0:b0 + q]).astype(BF16) for b0 in range(0, xb.shape[0], q)], axis=0)


def _conv_silu_interleaved(xp_ref, blk, prev, convw, convb):
    q = xp_ref.shape[1] - CONV_HALO
    first = lax.broadcasted_iota(jnp.int32, (SUBLANES, CONV_DIM), 0) == 0
    for j in range(CONV_WIDTH - 1):
        js = slice(j * SUBLANES, (j + 1) * SUBLANES)
        cur_j = xp_ref[blk, q + j * SUBLANES:q + (j + 1) * SUBLANES, :]
        xp_ref[blk, js, :] = jnp.where(first, pltpu.roll(prev[js], 1, 0), pltpu.roll(cur_j, 1, 0))
    acc = convb
    for k in range(CONV_WIDTH):
        acc = acc + xp_ref[blk, k * SUBLANES:k * SUBLANES + q, :] * convw[k:k + 1, :]
    return _silu(acc)


def _ssd_project_jobs(get_xb, get_xb_tokens, wz_ref, wxbc_ref, wdt_ref, wg_ref, bg_ref, dtb_ref,
                      z_ref, xp_ref, dt_ref, gate_ref):
    tile = SSD_PROJ_TILE
    q = xp_ref.shape[1] - CONV_HALO

    def xbc_job(c0):
        def job():
            res = _dot(get_xb(), wxbc_ref[:, c0:c0 + tile])
            for b in range(xp_ref.shape[0]):
                xp_ref[b, CONV_HALO:, c0:c0 + tile] = res[b * q:(b + 1) * q]
        return job

    def z_job(c0):
        def job():
            z_ref[:, c0:c0 + tile] = _dot(get_xb(), wz_ref[:, c0:c0 + tile])
        return job

    def gate_job(c0):
        def job():
            gate_ref[:, c0:c0 + tile] = jax.nn.sigmoid(_dot(get_xb_tokens(), wg_ref[:, c0:c0 + tile])
                                                       + bg_ref[:, c0:c0 + tile])
        return job

    def dt_job():
        dt_ref[...] = _softplus(_dot(get_xb(), wdt_ref[...]) + dtb_ref[...])

    return ([xbc_job(c0) for c0 in range(0, CONV_DIM, tile)] + [z_job(c0) for c0 in range(0, D_INNER, tile)]
            + [gate_job(c0) for c0 in range(0, D_MODEL, tile)] + [dt_job])


def _ssd_prompt_kernel(x0_ref, xn_ref, wz_ref, wxbc_ref, wdt_ref, wg_ref, bg_ref, convw_ref, convb_ref, dtb_ref,
                       alog_ref, dskip_ref, ng_ref, e_ref, wproj_ref,
                       a_out_ref, ssm_out_ref, conv_out_ref,
                       st_ref, bd_ref, hist_ref, xbn_ref, z_a, xp_a, dt_a, gate_a, z_b, xp_b, dt_b, gate_b, *,
                       steps_per_row):
    g = pl.program_id(0)
    c = g % steps_per_row
    nblk = xp_a.shape[0]
    proj_w = (wz_ref, wxbc_ref, wdt_ref, wg_ref, bg_ref, dtb_ref)

    @pl.when(g == 0)
    def _():
        bd_ref[...] = jnp.zeros_like(bd_ref)
        xb0 = x0_ref[...].astype(BF16)
        xbn_ref[...] = _interleave_rows(xb0, SSD_Q)
        for job in _ssd_project_jobs(lambda: xbn_ref[...], lambda: xb0, *proj_w, z_a, xp_a, dt_a, gate_a):
            job()

    @pl.when(c == 0)
    def _():
        st_ref[...] = jnp.zeros_like(st_ref)
        hist_ref[...] = jnp.zeros_like(hist_ref)

    def step(cur, nxt):
        z_ref, xp_ref, dt_ref, gate_ref = cur
        xbn_ref[...] = _interleave_rows(xn_ref[...].astype(BF16), SSD_Q)
        jobs = _ssd_project_jobs(lambda: xbn_ref[...], lambda: xn_ref[...].astype(BF16), *proj_w, *nxt)
        neg_a2 = -jnp.exp(alog_ref[...]) * LOG2_E
        yn = []
        for s in range(nblk):
            sl = slice(s * SSD_Q, (s + 1) * SSD_Q)
            prev = hist_ref[...] if s == 0 else xp_ref[s - 1, SSD_Q:, :]
            xc = _conv_silu_interleaved(xp_ref, s, prev, convw_ref[...], convb_ref[...])
            yn.append(_ssd_block(xc, z_ref[sl, :], dt_ref[sl, :], neg_a2, dskip_ref[...], ng_ref[...], e_ref,
                                 st_ref, bd_ref, jobs, interleaved=True))
        hist_ref[...] = xp_ref[nblk - 1, SSD_Q:, :]
        for job in jobs:
            job()
        yn = _interleave_rows(jnp.concatenate(yn, axis=0).astype(BF16), SSD_Q, inverse=True)
        a_out_ref[...] = gate_ref[...] * _dot(yn, wproj_ref[...])

    slot_a, slot_b = (z_a, xp_a, dt_a, gate_a), (z_b, xp_b, dt_b, gate_b)

    @pl.when(g % 2 == 0)
    def _():
        step(slot_a, slot_b)

    @pl.when(g % 2 == 1)
    def _():
        step(slot_b, slot_a)

    @pl.when(c == steps_per_row - 1)
    def _():
        for j in range(CONV_WIDTH - 1):
            conv_out_ref[j:j + 1, :] = hist_ref[(j + 1) * SUBLANES - 1:(j + 1) * SUBLANES, :]
        for k in range(SSM_GROUPS):
            ssm_out_ref[k * GROUP_DIM:(k + 1) * GROUP_DIM, :] = st_ref[k].T


def _ssd_sample_kernel(x_ref, state_ref, cstate_ref, wz_ref, wxbc_ref, wdt_ref, wg_ref, bg_ref, convw_ref, convb_ref,
                       dtb_ref, alog_ref, dskip_ref, ng_ref, e_ref, wproj_ref,
                       a_out_ref, ssm_out_ref, conv_out_ref,
                       st_ref, bd_ref, xp_ref, z_scr, xbc_scr, dt_scr, xcp_scr, zp_scr, dtp_scr, yn_scr):
    nseq, seq = cstate_ref.shape[0], x_ref.shape[0] // cstate_ref.shape[0]
    xb = x_ref[...].astype(BF16)
    z_scr[...] = _dot(xb, wz_ref[...])
    xbc_scr[...] = _dot(xb, wxbc_ref[...])
    dt_scr[...] = _softplus(_dot(xb, wdt_ref[...]) + dtb_ref[...])
    gate = jax.nn.sigmoid(_dot(xb, wg_ref[...]) + bg_ref[...])
    xcp_scr[...] = jnp.zeros_like(xcp_scr)
    zp_scr[...] = jnp.zeros_like(zp_scr)
    dtp_scr[...] = jnp.zeros_like(dtp_scr)
    bd_ref[...] = jnp.zeros_like(bd_ref)
    neg_a2 = -jnp.exp(alog_ref[...]) * LOG2_E
    lo = SUBLANES - (CONV_WIDTH - 1)

    def body(s, carry):
        r0 = pl.multiple_of(s * seq, seq)
        xp_ref[lo:SUBLANES, :] = cstate_ref[s]
        xp_ref[SUBLANES:SUBLANES + seq, :] = xbc_scr[pl.ds(r0, seq), :]
        xcp_scr[0:seq, :] = _conv_silu(xp_ref, seq, convw_ref[...], convb_ref[...])
        conv_out_ref[s] = xp_ref[seq + lo:seq + SUBLANES, :]
        zp_scr[0:seq, :] = z_scr[pl.ds(r0, seq), :]
        dtp_scr[0:seq, :] = dt_scr[pl.ds(r0, seq), :]
        for g in range(SSM_GROUPS):
            st_ref[g] = state_ref[s, g * GROUP_DIM:(g + 1) * GROUP_DIM, :].T
        yn = _ssd_block(xcp_scr[...], zp_scr[...], dtp_scr[...], neg_a2, dskip_ref[...], ng_ref[...], e_ref, st_ref,
                        bd_ref)
        yn_scr[pl.ds(r0, seq), :] = yn[0:seq].astype(BF16)
        for g in range(SSM_GROUPS):
            ssm_out_ref[s, g * GROUP_DIM:(g + 1) * GROUP_DIM, :] = st_ref[g].T
        return carry

    lax.fori_loop(0, nseq, body, 0)
    a_out_ref[...] = gate * _dot(yn_scr[...], wproj_ref[...])


def _ssd_weight_specs(w):
    names = ("w_z", "w_xbc", "w_dt", "w_gs", "b_gs", "conv_w", "conv_b", "dt_bias", "a_log", "d_skip", "norm_g",
             "expand", "w_ssm_proj")
    arrs = [w[n] for n in names]
    return arrs, [_const_spec(a.shape) for a in arrs]


def _ssd_prompt(x, w):
    b, l, d = x.shape
    rows = min(SSD_ROWS, l)
    assert l % rows == 0 and rows % SSD_Q == 0
    arrs, specs = _ssd_weight_specs(w)
    nc = l // rows
    last = b * nc - 1
    slot = [pltpu.VMEM((rows, D_INNER), F32), pltpu.VMEM((rows // SSD_Q, CONV_HALO + SSD_Q, CONV_DIM), F32),
            pltpu.VMEM((rows, LANES), F32), pltpu.VMEM((rows, D_MODEL), F32)]

    def next_block(g):
        n = jnp.minimum(g + 1, last)
        return (n // nc, n % nc, 0)

    return pl.pallas_call(
        functools.partial(_ssd_prompt_kernel, steps_per_row=nc),
        grid=(b * nc,),
        in_specs=[pl.BlockSpec((None, rows, d), lambda g: (0, 0, 0)),
                  pl.BlockSpec((None, rows, d), next_block)] + specs,
        out_specs=[pl.BlockSpec((None, rows, D_MODEL), lambda g: (g // nc, g % nc, 0)),
                   pl.BlockSpec((None, D_INNER, SSM_STATE), lambda g: (g // nc, 0, 0)),
                   pl.BlockSpec((None, CONV_WIDTH - 1, CONV_DIM), lambda g: (g // nc, 0, 0))],
        out_shape=[jax.ShapeDtypeStruct((b, l, D_MODEL), F32),
                   jax.ShapeDtypeStruct((b, D_INNER, SSM_STATE), F32),
                   jax.ShapeDtypeStruct((b, CONV_WIDTH - 1, CONV_DIM), F32)],
        scratch_shapes=[pltpu.VMEM((SSM_GROUPS, SSM_STATE, GROUP_DIM), F32),
                        pltpu.VMEM((SSM_GROUPS, HEADS_PER_GROUP * SSD_Q, GROUP_DIM), BF16),
                        pltpu.VMEM((CONV_HALO, CONV_DIM), F32),
                        pltpu.VMEM((rows, d), BF16)] + slot + slot,
        compiler_params=_params(1),
        name="ssd_prompt",
    )(x, x, *arrs)


def _ssd_sample(x, state, cstate, w):
    b, l, d = x.shape
    nseq = min(SSD_SAMPLE_SEQS, b)
    assert b % nseq == 0 and l % SUBLANES == 0 and l <= SSD_Q
    rows = nseq * l
    arrs, specs = _ssd_weight_specs(w)
    a, ssm, conv = pl.pallas_call(
        _ssd_sample_kernel,
        grid=(b // nseq,),
        in_specs=[pl.BlockSpec((rows, d), lambda i: (i, 0)),
                  pl.BlockSpec((nseq, D_INNER, SSM_STATE), lambda i: (i, 0, 0)),
                  pl.BlockSpec((nseq, CONV_WIDTH - 1, CONV_DIM), lambda i: (i, 0, 0))] + specs,
        out_specs=[pl.BlockSpec((rows, D_MODEL), lambda i: (i, 0)),
                   pl.BlockSpec((nseq, D_INNER, SSM_STATE), lambda i: (i, 0, 0)),
                   pl.BlockSpec((nseq, CONV_WIDTH - 1, CONV_DIM), lambda i: (i, 0, 0))],
        out_shape=[jax.ShapeDtypeStruct((b * l, D_MODEL), F32),
                   jax.ShapeDtypeStruct((b, D_INNER, SSM_STATE), F32),
                   jax.ShapeDtypeStruct((b, CONV_WIDTH - 1, CONV_DIM), F32)],
        scratch_shapes=[pltpu.VMEM((SSM_GROUPS, SSM_STATE, GROUP_DIM), F32),
                        pltpu.VMEM((SSM_GROUPS, HEADS_PER_GROUP * SSD_Q, GROUP_DIM), BF16),
                        pltpu.VMEM((l + SUBLANES, CONV_DIM), F32),
                        pltpu.VMEM((rows, D_INNER), F32),
                        pltpu.VMEM((rows, CONV_DIM), F32),
                        pltpu.VMEM((rows, LANES), F32),
                        pltpu.VMEM((SSD_Q, CONV_DIM), F32),
                        pltpu.VMEM((SSD_Q, D_INNER), F32),
                        pltpu.VMEM((SSD_Q, LANES), F32),
                        pltpu.VMEM((rows, D_INNER), BF16)],
        compiler_params=_params(1),
        name="ssd_sample",
    )(x.reshape(b * l, d), state.reshape(b, D_INNER, SSM_STATE), cstate, *arrs)
    return a, ssm, conv


def _mla_keys(xb, cos, sin, wkva_ref, kvg_ref):
    kva = _dot(xb, wkva_ref[...])
    ckv = _rms(kva[:, :KV_RANK]) * kvg_ref[...]
    kpe = (kva[:, KV_RANK:KV_RANK + QK_ROPE] * cos[:, :QK_ROPE]
           + kva[:, KV_RANK + LANES:KV_RANK + LANES + QK_ROPE] * sin[:, :QK_ROPE])
    return ckv, kpe


def _mla_queries(xb, cos, sin, wqa_ref, qg_ref, wqb_ref, wukt_ref, store):
    qn = (_rms(_dot(xb, wqa_ref[...])) * qg_ref[...]).astype(BF16)
    q = _dot(qn, wqb_ref[...])
    nope_w = MLA_HEADS * QK_NOPE
    rope_w = MLA_HEADS * QK_ROPE
    for p in range(MLA_HEADS // 2):
        sl = slice(nope_w + p * LANES, nope_w + (p + 1) * LANES)
        sl_sw = slice(nope_w + rope_w + p * LANES, nope_w + rope_w + (p + 1) * LANES)
        pair = (q[:, sl] * cos + q[:, sl_sw] * sin) * ATTN_SCALE
        pair_hi = pltpu.roll(pair, QK_ROPE, 1)
        for h, pe in ((2 * p, pair), (2 * p + 1, pair_hi)):
            q_lat = _dot(q[:, h * QK_NOPE:(h + 1) * QK_NOPE].astype(BF16), wukt_ref[h]) * ATTN_SCALE
            store(h, q_lat.astype(BF16), pe[:, :QK_ROPE].astype(BF16))


def _lane_tile(t, width):
    return jnp.concatenate([t] * (width // LANES), axis=1) if width > LANES else t


def _softmax_update(s, v_b, m_scr, l_scr, acc_scr):
    m_prev = m_scr[...]
    m_new = jnp.maximum(m_prev, jnp.max(s, axis=-1, keepdims=True))
    alpha = jnp.exp(m_prev - m_new)
    p = jnp.exp(s - _lane_tile(m_new, s.shape[1]))
    p_lanes = p[:, 0:LANES]
    for c in range(1, s.shape[1] // LANES):
        p_lanes = p_lanes + p[:, c * LANES:(c + 1) * LANES]
    l_scr[...] = alpha * l_scr[...] + p_lanes
    acc_scr[...] = _lane_tile(alpha, acc_scr.shape[1]) * acc_scr[...] + _dot(p.astype(BF16), v_b)
    m_scr[...] = m_new


def _softmax_finish(l_scr, acc_scr):
    return acc_scr[...] / jnp.sum(l_scr[...], axis=-1, keepdims=True)


def _mla_out(o_of_head, xb, wuv_ref, wmla_ref, wg_ref, bg_ref, o_scr):
    for h in range(MLA_HEADS):
        o_scr[:, h * V_HEAD:(h + 1) * V_HEAD] = _dot(o_of_head(h), wuv_ref[h]).astype(BF16)
    gate = jax.nn.sigmoid(_dot(xb, wg_ref[...]) + bg_ref[...])
    return gate * _dot(o_scr[...], wmla_ref[...])


def _mla_pre_kernel(x_ref, cos_ref, sin_ref, wqa_ref, qg_ref, wkva_ref, kvg_ref, qn_out_ref, ckv_out_ref, kpe_out_ref):
    xb = x_ref[...].astype(BF16)
    ckv, kpe = _mla_keys(xb, cos_ref[...], sin_ref[...], wkva_ref, kvg_ref)
    ckv_out_ref[...] = ckv
    kpe_out_ref[...] = kpe
    qn_out_ref[...] = (_rms(_dot(xb, wqa_ref[...])) * qg_ref[...]).astype(BF16)


def _mla_prompt_attn_kernel(qn_ref, ckv_ref, kpe_ref, cos_ref, sin_ref, wqp_ref, wkv_ref, o_ref,
                            qa_scr, qb_scr, ka_scr, kb_scr, vbd_scr, m_scr, l_scr, acc_scr):
    l = qn_ref.shape[0]
    g = min(MLA_BLK, l)
    nb = l // g
    half = QK_ROPE
    zpad = jnp.zeros((l, LANES - half), BF16)
    kv = _dot(ckv_ref[...].astype(BF16), wkv_ref[...])
    kpe_b = kpe_ref[...].astype(BF16)
    for scr, c0 in ((ka_scr, 0), (kb_scr, QK_NOPE)):
        scr[:, 0:QK_NOPE] = kv[:, c0:c0 + QK_NOPE].astype(BF16)
        scr[:, QK_NOPE:QK_NOPE + half] = kpe_b
        scr[:, QK_NOPE + half:] = zpad
    zv = jnp.zeros((g, V_HEAD), BF16)
    for j in range(nb):
        rs = slice(j * g, (j + 1) * g)
        vbd_scr[j, 0:g, 0:V_HEAD] = kv[rs, 2 * QK_NOPE:2 * QK_NOPE + V_HEAD].astype(BF16)
        vbd_scr[j, 0:g, V_HEAD:] = zv
        vbd_scr[j, g:, 0:V_HEAD] = zv
        vbd_scr[j, g:, V_HEAD:] = kv[rs, 2 * QK_NOPE + V_HEAD:].astype(BF16)
    q = _dot(qn_ref[...], wqp_ref[...])
    c = ATTN_SCALE * LOG2_E
    pair = (q[:, 2 * QK_NOPE:3 * QK_NOPE] * cos_ref[...] + q[:, 3 * QK_NOPE:] * sin_ref[...]) * c
    pair_hi = pltpu.roll(pair, half, 1)
    for scr, c0, pe in ((qa_scr, 0, pair), (qb_scr, QK_NOPE, pair_hi)):
        scr[:, 0:QK_NOPE] = (q[:, c0:c0 + QK_NOPE] * c).astype(BF16)
        scr[:, QK_NOPE:QK_NOPE + half] = pe[:, :half].astype(BF16)
        scr[:, QK_NOPE + half:] = zpad
    row = lax.broadcasted_iota(jnp.int32, (g, g), 0)
    col = lax.broadcasted_iota(jnp.int32, (g, g), 1)
    diag_visible = (row >> CHUNK_SHIFT) >= (col >> CHUNK_SHIFT)
    for j in range(nb):
        r0 = j * g
        ps, alphas = [], []
        for hi, (q_scr, k_scr) in enumerate(((qa_scr, ka_scr), (qb_scr, kb_scr))):
            s = _dot_nt(q_scr[r0:, :], k_scr[r0:r0 + g, :])
            top = jnp.where(diag_visible, s[:g], NEG_INF)
            s = top if j == nb - 1 else jnp.concatenate([top, s[g:]], axis=0)
            m_cur = jnp.max(s, axis=-1, keepdims=True)
            if j == 0:
                m_new = jnp.broadcast_to(m_cur, (l, LANES))
            else:
                m_prev = m_scr[hi, r0:, :]
                m_new = jnp.maximum(m_prev, m_cur)
                alphas.append(jnp.exp2(m_prev - m_new))
            p = jnp.exp2(s - _lane_tile(m_new, g))
            p_lanes = p[:, 0:LANES]
            for cc in range(1, g // LANES):
                p_lanes = p_lanes + p[:, cc * LANES:(cc + 1) * LANES]
            l_scr[hi, r0:, :] = p_lanes if j == 0 else alphas[-1] * l_scr[hi, r0:, :] + p_lanes
            m_scr[hi, r0:, :] = m_new
            ps.append(p.astype(BF16))
        pv = _dot(jnp.concatenate(ps, axis=1), vbd_scr[j])
        if j == 0:
            acc_scr[...] = pv
        else:
            acc_scr[r0:, :] = jnp.concatenate(alphas, axis=1) * acc_scr[r0:, :] + pv
    inv = [1.0 / jnp.sum(l_scr[hi], axis=-1, keepdims=True) for hi in range(2)]
    scale = jnp.concatenate([jnp.broadcast_to(t, (l, V_HEAD)) for t in inv], axis=1)
    o_ref[...] = (acc_scr[...] * scale).astype(BF16)


def _mla_post_kernel(o_ref, x_ref, wmla_ref, wg_ref, bg_ref, a_out_ref):
    gate = jax.nn.sigmoid(_dot(x_ref[...].astype(BF16), wg_ref[...]) + bg_ref[...])
    a_out_ref[...] = gate * _dot(o_ref[...], wmla_ref[...])


def _mla_prompt(x, cos, sin, w):
    b, l, d = x.shape
    g = min(MLA_BLK, l)
    tm = min(MLA_ROWS, l)
    npair = MLA_HEADS // 2
    assert l % g == 0 and g % CHUNK == 0 and g % LANES == 0 and l % tm == 0
    nt = l // tm
    pre = [w[n] for n in ("w_qa", "q_norm_g", "w_kva", "kv_norm_g")]
    qn, ckv, kpe = pl.pallas_call(
        _mla_pre_kernel,
        grid=(b, nt),
        in_specs=[pl.BlockSpec((None, tm, d), lambda i, c: (i, c, 0)),
                  pl.BlockSpec((tm, LANES), lambda i, c: (c, 0)),
                  pl.BlockSpec((tm, LANES), lambda i, c: (c, 0))] + [_const_spec(a.shape) for a in pre],
        out_specs=[pl.BlockSpec((None, tm, Q_RANK), lambda i, c: (i, c, 0)),
                   pl.BlockSpec((None, tm, KV_RANK), lambda i, c: (i, c, 0)),
                   pl.BlockSpec((None, tm, QK_ROPE), lambda i, c: (i, c, 0))],
        out_shape=[jax.ShapeDtypeStruct((b, l, Q_RANK), BF16),
                   jax.ShapeDtypeStruct((b, l, KV_RANK), F32),
                   jax.ShapeDtypeStruct((b, l, QK_ROPE), F32)],
        compiler_params=_params(2),
        name="mla_pre",
    )(x, cos, sin, *pre)
    o = pl.pallas_call(
        _mla_prompt_attn_kernel,
        grid=(b, npair),
        in_specs=[pl.BlockSpec((None, l, Q_RANK), lambda i, p: (i, 0, 0)),
                  pl.BlockSpec((None, l, KV_RANK), lambda i, p: (i, 0, 0)),
                  pl.BlockSpec((None, l, QK_ROPE), lambda i, p: (i, 0, 0)),
                  _const_spec(cos.shape), _const_spec(sin.shape),
                  pl.BlockSpec((None, Q_RANK, 4 * QK_NOPE), lambda i, p: (p, 0, 0)),
                  pl.BlockSpec((None, KV_RANK, 2 * QK_NOPE + 2 * V_HEAD), lambda i, p: (p, 0, 0))],
        out_specs=pl.BlockSpec((None, l, 2 * V_HEAD), lambda i, p: (i, 0, p)),
        out_shape=jax.ShapeDtypeStruct((b, l, MLA_HEADS * V_HEAD), BF16),
        scratch_shapes=[pltpu.VMEM((l, 2 * LANES), BF16)] * 4
        + [pltpu.VMEM((l // g, 2 * g, 2 * V_HEAD), BF16),
           pltpu.VMEM((2, l, LANES), F32),
           pltpu.VMEM((2, l, LANES), F32),
           pltpu.VMEM((l, 2 * V_HEAD), F32)],
        compiler_params=_params(2),
        name="mla_prompt_attn",
    )(qn, ckv, kpe, cos, sin, w["w_qpair"], w["w_kvpair"])
    post = [w[n] for n in ("w_mla_proj", "w_gm", "b_gm")]
    a = pl.pallas_call(
        _mla_post_kernel,
        grid=(b, nt),
        in_specs=[pl.BlockSpec((None, tm, MLA_HEADS * V_HEAD), lambda i, c: (i, c, 0)),
                  pl.BlockSpec((None, tm, d), lambda i, c: (i, c, 0))] + [_const_spec(t.shape) for t in post],
        out_specs=pl.BlockSpec((None, tm, D_MODEL), lambda i, c: (i, c, 0)),
        out_shape=jax.ShapeDtypeStruct((b, l, D_MODEL), F32),
        compiler_params=_params(2),
        name="mla_post",
    )(o, x, *post)
    return a, ckv, kpe


def _mla_dec_pre_kernel(x_ref, cos_ref, sin_ref, wqa_ref, qg_ref, wqb_ref, wukt_ref, wkva_ref, kvg_ref,
                        q_out_ref, ckv_out_ref, kpe_out_ref):
    xb = x_ref[...].astype(BF16)
    cos, sin = cos_ref[...], sin_ref[...]
    ckv, kpe = _mla_keys(xb, cos, sin, wkva_ref, kvg_ref)
    ckv_out_ref[...] = ckv
    kpe_out_ref[...] = kpe

    def store(h, q_lat, q_pe):
        q_out_ref[h, :, 0:KV_RANK] = q_lat
        q_out_ref[h, :, KV_RANK:QK_LAT] = q_pe

    _mla_queries(xb, cos, sin, wqa_ref, qg_ref, wqb_ref, wukt_ref, store)


def _mla_dec_attn_kernel(q_ref, cckv_ref, ckpe_t_ref, nckv_ref, nkpe_ref, o_ref,
                         m_scr, l_scr, acc_scr, nk_scr, *, past_len):
    j = pl.program_id(1)
    seq = q_ref.shape[1]
    rows = MLA_HEADS * seq

    @pl.when(j == 0)
    def _():
        m_scr[...] = jnp.full_like(m_scr, NEG_INF)
        l_scr[...] = jnp.zeros_like(l_scr)
        acc_scr[...] = jnp.zeros_like(acc_scr)

    q = q_ref[...].reshape(rows, QK_LAT)
    ck = cckv_ref[...].astype(BF16)
    kp_t = ckpe_t_ref[...].astype(BF16)
    s = _dot_nt(q[:, :KV_RANK], ck) + _dot(q[:, KV_RANK:], kp_t)
    _softmax_update(s, ck, m_scr, l_scr, acc_scr)

    @pl.when(j == pl.num_programs(1) - 1)
    def _():
        nk_scr[...] = jnp.zeros_like(nk_scr)
        nk_scr[0:seq, 0:KV_RANK] = nckv_ref[...].astype(BF16)
        nk_scr[0:seq, KV_RANK:QK_LAT] = nkpe_ref[...].astype(BF16)
        nk = nk_scr[...]
        s2 = _dot_nt(q, nk)
        row = lax.broadcasted_iota(jnp.int32, (rows, 1), 0)
        q_chunk = (past_len + (row & (seq - 1))) >> CHUNK_SHIFT
        k_lane = lax.broadcasted_iota(jnp.int32, (1, LANES), 1)
        visible = (k_lane < seq) & (((past_len + k_lane) >> CHUNK_SHIFT) <= q_chunk)
        _softmax_update(jnp.where(visible, s2, NEG_INF), nk[:, :KV_RANK], m_scr, l_scr, acc_scr)
        o_ref[...] = _softmax_finish(l_scr, acc_scr).astype(BF16).reshape(MLA_HEADS, seq, KV_RANK)


def _mla_dec_post_kernel(o_ref, x_ref, wuv_ref, wmla_ref, wg_ref, bg_ref, a_out_ref, o_scr):
    xb = x_ref[...].astype(BF16)
    a_out_ref[...] = _mla_out(lambda h: o_ref[h], xb, wuv_ref, wmla_ref, wg_ref, bg_ref, o_scr)


def _mla_sample(x, cache_ckv, cache_kpe, cos, sin, w):
    b, l, d = x.shape
    t = b * l
    past_len = cache_ckv.shape[1]
    tm = min(ROW_TILE, t)
    tk = min(DEC_TK, past_len)
    assert t % tm == 0 and past_len % tk == 0 and (l & (l - 1)) == 0 and l <= LANES and l % 16 == 0
    x2 = x.reshape(t, d)
    pre_names = ("w_qa", "q_norm_g", "w_qb", "w_ukt", "w_kva", "kv_norm_g")
    pre = [w[n] for n in pre_names]
    q, ckv, kpe = pl.pallas_call(
        _mla_dec_pre_kernel,
        grid=(t // tm,),
        in_specs=[pl.BlockSpec((tm, d), lambda i: (i, 0)),
                  pl.BlockSpec((tm, LANES), lambda i: (i, 0)),
                  pl.BlockSpec((tm, LANES), lambda i: (i, 0))] + [_const_spec(a.shape) for a in pre],
        out_specs=[pl.BlockSpec((MLA_HEADS, tm, QK_LAT), lambda i: (0, i, 0)),
                   pl.BlockSpec((tm, KV_RANK), lambda i: (i, 0)),
                   pl.BlockSpec((tm, QK_ROPE), lambda i: (i, 0))],
        out_shape=[jax.ShapeDtypeStruct((MLA_HEADS, t, QK_LAT), BF16),
                   jax.ShapeDtypeStruct((t, KV_RANK), F32),
                   jax.ShapeDtypeStruct((t, QK_ROPE), F32)],
        compiler_params=_params(1),
        name="mla_dec_pre",
    )(x2, jnp.tile(cos, (b, 1)), jnp.tile(sin, (b, 1)), *pre)
    rows = MLA_HEADS * l
    o = pl.pallas_call(
        functools.partial(_mla_dec_attn_kernel, past_len=past_len),
        grid=(b, past_len // tk),
        in_specs=[pl.BlockSpec((MLA_HEADS, l, QK_LAT), lambda i, j: (0, i, 0)),
                  pl.BlockSpec((None, tk, KV_RANK), lambda i, j: (i, j, 0)),
                  pl.BlockSpec((None, QK_ROPE, tk), lambda i, j: (i, 0, j)),
                  pl.BlockSpec((l, KV_RANK), lambda i, j: (i, 0)),
                  pl.BlockSpec((l, QK_ROPE), lambda i, j: (i, 0))],
        out_specs=pl.BlockSpec((MLA_HEADS, l, KV_RANK), lambda i, j: (0, i, 0)),
        out_shape=jax.ShapeDtypeStruct((MLA_HEADS, t, KV_RANK), BF16),
        scratch_shapes=[pltpu.VMEM((rows, LANES), F32),
                        pltpu.VMEM((rows, LANES), F32),
                        pltpu.VMEM((rows, KV_RANK), F32),
                        pltpu.VMEM((LANES, QK_LAT), BF16)],
        compiler_params=_params(2),
        name="mla_dec_attn",
    )(q, cache_ckv, jnp.swapaxes(cache_kpe, 1, 2), ckv, kpe)
    post_names = ("w_uv", "w_mla_proj", "w_gm", "b_gm")
    post = [w[n] for n in post_names]
    a = pl.pallas_call(
        _mla_dec_post_kernel,
        grid=(t // tm,),
        in_specs=[pl.BlockSpec((MLA_HEADS, tm, KV_RANK), lambda i: (0, i, 0)),
                  pl.BlockSpec((tm, d), lambda i: (i, 0))] + [_const_spec(a.shape) for a in post],
        out_specs=pl.BlockSpec((tm, D_MODEL), lambda i: (i, 0)),
        out_shape=jax.ShapeDtypeStruct((t, D_MODEL), F32),
        scratch_shapes=[pltpu.VMEM((tm, MLA_HEADS * V_HEAD), BF16)],
        compiler_params=_params(1),
        name="mla_dec_post",
    )(o, x2, *post)
    return a, ckv.reshape(b, l, KV_RANK), kpe.reshape(b, l, QK_ROPE)


def _ffn_kernel(x_ref, as_ref, am_ref, wout_ref, g1_ref, b1_ref, wup_ref, wdown_ref, g2_ref, b2_ref, y_ref, *, alpha):
    sub = min(FFN_SUB_ROWS, x_ref.shape[0])
    for r0 in range(0, x_ref.shape[0], sub):
        rs = pl.ds(r0, sub)
        mix = _dot((as_ref[rs, :] + am_ref[rs, :]).astype(BF16), wout_ref[...])
        h = _layernorm(alpha * x_ref[rs, :] + mix, g1_ref[...], b1_ref[...])
        up = jnp.maximum(_dot(h.astype(BF16), wup_ref[...]), 0.0)
        ff = _dot((up * up).astype(BF16), wdown_ref[...])
        y_ref[rs, :] = _layernorm(alpha * h + ff, g2_ref[...], b2_ref[...])


def _ffn(x2, a_ssm, a_mla, w, alpha):
    t, d = x2.shape
    tm = min(FFN_ROWS, t)
    assert t % tm == 0 and tm % min(FFN_SUB_ROWS, tm) == 0
    names = ("w_out", "ln1_g", "ln1_b", "w_up", "w_down", "ln2_g", "ln2_b")
    arrs = [w[n] for n in names]
    row_spec = pl.BlockSpec((tm, d), lambda i: (i, 0))
    return pl.pallas_call(
        functools.partial(_ffn_kernel, alpha=alpha),
        grid=(t // tm,),
        in_specs=[row_spec, row_spec, row_spec] + [_const_spec(a.shape) for a in arrs],
        out_specs=row_spec,
        out_shape=jax.ShapeDtypeStruct((t, d), F32),
        compiler_params=_params(1),
        name="ffn",
    )(x2, a_ssm, a_mla, *arrs)


def _rope_tables(pos0, n):
    inv = ROPE_THETA ** (-jnp.arange(0, QK_ROPE, 2, dtype=F32) / QK_ROPE)
    ang = (pos0 + jnp.arange(n, dtype=jnp.int32)).astype(F32)[:, None] * inv[None, :]
    cos, sin = jnp.cos(ang), jnp.sin(ang)
    cos2 = jnp.concatenate([cos, cos], axis=-1)
    sin2 = jnp.concatenate([-sin, sin], axis=-1)
    return jnp.tile(cos2, (1, LANES // QK_ROPE)), jnp.tile(sin2, (1, LANES // QK_ROPE))


def _swap_halves(t):
    half = t.shape[-1] // 2
    return jnp.concatenate([t[..., half:], t[..., :half]], axis=-1)


def _layer_weights(w_in, b_gate, conv_w, conv_b, dt_bias, a_log, d_skip, ssm_norm_g, w_ssm_proj, q_norm_g, w_q_b,
                   kv_norm_g, w_uk, w_uv, w_mla_proj, w_out, ln1_g, ln1_b, w_up, w_down, ln2_g, ln2_b):
    d = w_in.shape[0]
    o_z = 2 * D_MODEL
    o_xbc = o_z + D_INNER
    o_dt = o_xbc + CONV_DIM
    o_qa = o_dt + SSM_HEADS
    o_kva = o_qa + Q_RANK
    bf = lambda t: t.astype(BF16)
    row = lambda t: t.reshape(1, -1)
    lane_pad = lambda t: jnp.pad(t, ((0, 0), (0, LANES - t.shape[-1])))
    w_kv = w_in[:, o_kva:]
    w_kpe = w_kv[:, KV_RANK:]
    zpad = jnp.zeros((d, LANES - QK_ROPE), w_in.dtype)
    qb = w_q_b.reshape(Q_RANK, MLA_HEADS, QK_NOPE + QK_ROPE)
    qb_rope = qb[:, :, QK_NOPE:]
    npair = MLA_HEADS // 2
    k = jnp.arange(LANES)[:, None]
    c = jnp.arange(D_INNER)[None, :]
    expand = ((k < 3 * SSM_HEADS) & ((k % SSM_HEADS) == (c // SSM_HEAD_DIM))).astype(BF16)
    return dict(
        w_gs=bf(w_in[:, :D_MODEL]), b_gs=row(b_gate[:D_MODEL]),
        w_gm=bf(w_in[:, D_MODEL:o_z]), b_gm=row(b_gate[D_MODEL:]),
        w_z=bf(w_in[:, o_z:o_xbc]), w_xbc=bf(w_in[:, o_xbc:o_dt]), w_dt=bf(lane_pad(w_in[:, o_dt:o_qa])),
        conv_w=conv_w, conv_b=row(conv_b), dt_bias=lane_pad(row(dt_bias)), a_log=lane_pad(row(a_log)),
        d_skip=row(jnp.repeat(d_skip, SSM_HEAD_DIM)), norm_g=row(ssm_norm_g), expand=expand,
        w_ssm_proj=bf(w_ssm_proj),
        w_qa=bf(w_in[:, o_qa:o_kva]), q_norm_g=row(q_norm_g),
        w_qb=bf(jnp.concatenate([qb[:, :, :QK_NOPE].reshape(Q_RANK, -1), qb_rope.reshape(Q_RANK, -1),
                                 _swap_halves(qb_rope).reshape(Q_RANK, -1)], axis=1)),
        w_ukt=bf(jnp.transpose(w_uk, (1, 2, 0))),
        w_qpair=bf(jnp.transpose(jnp.concatenate(
            [qb[:, :, :QK_NOPE].reshape(Q_RANK, npair, 2 * QK_NOPE), qb_rope.reshape(Q_RANK, npair, 2 * QK_ROPE),
             _swap_halves(qb_rope).reshape(Q_RANK, npair, 2 * QK_ROPE)], axis=2), (1, 0, 2))),
        w_kvpair=bf(jnp.transpose(jnp.concatenate(
            [w_uk.reshape(KV_RANK, npair, 2 * QK_NOPE), w_uv.reshape(KV_RANK, npair, 2 * V_HEAD)], axis=2), (1, 0, 2))),
        w_kva=bf(jnp.concatenate([w_kv[:, :KV_RANK], w_kpe, zpad, _swap_halves(w_kpe), zpad], axis=1)),
        kv_norm_g=row(kv_norm_g), w_uv=bf(jnp.transpose(w_uv, (1, 0, 2))), w_mla_proj=bf(w_mla_proj),
        w_out=bf(w_out), ln1_g=row(ln1_g), ln1_b=row(ln1_b), w_up=bf(w_up), w_down=bf(w_down),
        ln2_g=row(ln2_g), ln2_b=row(ln2_b))


def _prompt_layer(x, w, alpha):
    b, l, d = x.shape
    cos, sin = _rope_tables(0, l)
    a_ssm, ssm, conv = _ssd_prompt(x, w)
    a_mla, ckv, kpe = _mla_prompt(x, cos, sin, w)
    y = _ffn(x.reshape(b * l, d), a_ssm.reshape(b * l, d), a_mla.reshape(b * l, d), w, alpha)
    return y.reshape(b, l, d), ckv, kpe, ssm.reshape(b, SSM_HEADS, SSM_HEAD_DIM, SSM_STATE), conv


def _sample_layer(x, cache_ckv, cache_kpe, state_ssm, state_conv, w, alpha):
    b, l, d = x.shape
    cos, sin = _rope_tables(cache_ckv.shape[1], l)
    a_ssm, ssm, conv = _ssd_sample(x, state_ssm, state_conv, w)
    a_mla, ckv, kpe = _mla_sample(x, cache_ckv, cache_kpe, cos, sin, w)
    y = _ffn(x.reshape(b * l, d), a_ssm, a_mla, w, alpha)
    return y.reshape(b, l, d), ckv, kpe, ssm.reshape(b, SSM_HEADS, SSM_HEAD_DIM, SSM_STATE), conv


def kernel(x_prompt, x_sample, cache_ckv, cache_kpe, state_ssm, state_conv, w_in, b_gate, conv_w, conv_b, dt_bias, a_log, d_skip, ssm_norm_g, w_ssm_proj, q_norm_g, w_q_b, kv_norm_g, w_uk, w_uv, w_mla_proj, w_out, ln1_g, ln1_b, w_up, w_down, ln2_g, ln2_b):
    depth = w_in.shape[0]
    alpha = (2 * depth) ** 0.25
    layer_params = (w_in, b_gate, conv_w, conv_b, dt_bias, a_log, d_skip, ssm_norm_g, w_ssm_proj, q_norm_g, w_q_b,
                    kv_norm_g, w_uk, w_uv, w_mla_proj, w_out, ln1_g, ln1_b, w_up, w_down, ln2_g, ln2_b)
    yp, ys = x_prompt, x_sample
    st_p, st_s = [], []
    for i in range(depth):
        w = _layer_weights(*(p[i] for p in layer_params))
        yp, *sp = _prompt_layer(yp, w, alpha)
        ys, *ss = _sample_layer(ys, cache_ckv[i], cache_kpe[i], state_ssm[i], state_conv[i], w, alpha)
        st_p.append(sp)
        st_s.append(ss)
    stack = lambda sts, k: jnp.stack([s[k] for s in sts])
    return (yp, ys, stack(st_p, 0), stack(st_p, 1), stack(st_p, 2), stack(st_p, 3),
            stack(st_s, 0), stack(st_s, 1), stack(st_s, 2), stack(st_s, 3))
```

```python
import functools
import math

import jax
import jax.numpy as jnp
from jax import lax
from jax.experimental import pallas as pl
from jax.experimental.pallas import tpu as pltpu

D_MODEL = 1024
D_INNER = 2 * D_MODEL
SSM_HEAD_DIM = 64
SSM_HEADS = D_INNER // SSM_HEAD_DIM
SSM_GROUPS = 8
HEADS_PER_GROUP = SSM_HEADS // SSM_GROUPS
GROUP_DIM = HEADS_PER_GROUP * SSM_HEAD_DIM
SSM_STATE = 128
CONV_WIDTH = 4
CONV_DIM = D_INNER + 2 * SSM_GROUPS * SSM_STATE
MLA_HEADS = 16
QK_NOPE = 128
QK_ROPE = 64
V_HEAD = 128
Q_RANK = 512
KV_RANK = 512
QK_LAT = KV_RANK + QK_ROPE
ROPE_THETA = 10000.0
ATTN_SCALE = (QK_NOPE + QK_ROPE) ** -0.5
D_FF = 4 * D_MODEL
RMS_EPS = 1e-6
LN_EPS = 1e-5
CHUNK = 64
CHUNK_SHIFT = 6

LANES = 128
SUBLANES = 8
VMEM_LIMIT = 62 * 1024 * 1024

SSD_Q = 128
SSD_ROWS = 256
SSD_PROJ_TILE = 512
CONV_HALO = (CONV_WIDTH - 1) * SUBLANES
SSD_SAMPLE_SEQS = 4
MLA_BLK = 256
DEC_TK = 2048
ROW_TILE = 256
MLA_ROWS = 1024
FFN_ROWS = 512
FFN_SUB_ROWS = 512

BF16 = jnp.bfloat16
F32 = jnp.float32
NEG_INF = float("-inf")
LOG2_E = math.log2(math.e)


def _dot(a, b):
    return jnp.dot(a, b, preferred_element_type=F32)


def _dot_nt(a, b):
    return lax.dot_general(a, b, (((1,), (1,)), ((), ())), preferred_element_type=F32)


def _rms(t):
    return t * lax.rsqrt(jnp.mean(t * t, axis=-1, keepdims=True) + RMS_EPS)


def _layernorm(t, g, b):
    mu = jnp.mean(t, axis=-1, keepdims=True)
    d = t - mu
    var = jnp.mean(d * d, axis=-1, keepdims=True)
    return d * lax.rsqrt(var + LN_EPS) * g + b


def _silu(t):
    return t * jax.nn.sigmoid(t)


def _softplus(t):
    return jnp.maximum(t, 0.0) + jnp.log(1.0 + jnp.exp(-jnp.abs(t)))


def _const_spec(shape):
    zeros = (0,) * len(shape)
    return pl.BlockSpec(shape, lambda *_: zeros, pipeline_mode=pl.Buffered(1))


def _params(n_axes):
    return pltpu.CompilerParams(dimension_semantics=("arbitrary",) * n_axes, vmem_limit_bytes=VMEM_LIMIT)


def _expand_heads(v, e_ref):
    hi = v.astype(BF16).astype(F32)
    r1 = v - hi
    mid = r1.astype(BF16).astype(F32)
    lo = r1 - mid
    lane = lax.broadcasted_iota(jnp.int32, v.shape, 1)
    packed = jnp.where(lane < 32, hi,
                       jnp.where(lane < 64, pltpu.roll(mid, 32, 1),
                                 jnp.where(lane < 96, pltpu.roll(lo, 64, 1), 0.0)))
    return _dot(packed.astype(BF16), e_ref[...])


def _time_of_row(r, q):
    return (r & (SUBLANES - 1)) * (q // SUBLANES) + (r >> 3)


def _ssd_block(xc, z, dt, neg_a2, dskip, norm_g, e_ref, st_ref, bd_ref, side_jobs=None, interleaved=False):
    q = xc.shape[0]
    row = lax.broadcasted_iota(jnp.int32, (q, q), 0)
    col = lax.broadcasted_iota(jnp.int32, (q, q), 1)
    if interleaved:
        row, col = _time_of_row(row, q), _time_of_row(col, q)
    causal = row >= col
    da = dt * neg_a2
    acum = jnp.dot(causal.astype(F32), da, precision=lax.Precision.HIGHEST, preferred_element_type=F32)
    acum_t = acum.T
    dt_full = _expand_heads(dt, e_ref)
    acum_full = _expand_heads(acum, e_ref)
    decay_full = jnp.exp2(acum_full)
    alast_full = acum_full[q - 1:q, :]
    xs = xc[:, :D_INNER]
    xdt = xs * dt_full
    xdt_b = xdt.astype(BF16)
    xw_b = (xdt * jnp.exp2(alast_full - acum_full)).astype(BF16)
    state_decay = jnp.exp2(alast_full)
    bm_t = xc[:, D_INNER:D_INNER + SSM_GROUPS * SSM_STATE].T.astype(BF16)
    cm_b = xc[:, D_INNER + SSM_GROUPS * SSM_STATE:].astype(BF16)
    ys = []
    for g in range(SSM_GROUPS):
        if side_jobs:
            side_jobs.pop(0)()
        gsl = slice(g * GROUP_DIM, (g + 1) * GROUP_DIM)
        nsl = slice(g * SSM_STATE, (g + 1) * SSM_STATE)
        cg = cm_b[:, nsl]
        bg_t = bm_t[nsl, :]
        cb = _dot(cg, bg_t)
        ms = []
        for r in range(HEADS_PER_GROUP):
            h = g * HEADS_PER_GROUP + r
            hsl = slice(r * SSM_HEAD_DIM, (r + 1) * SSM_HEAD_DIM)
            bd_ref[g, r * q:(r + 1) * q, hsl] = xdt_b[:, g * GROUP_DIM + r * SSM_HEAD_DIM:g * GROUP_DIM + (r + 1) * SSM_HEAD_DIM]
            seg = acum[:, h:h + 1] - acum_t[h:h + 1, :]
            ms.append((cb * jnp.exp2(jnp.where(causal, seg, NEG_INF))).astype(BF16))
        yg = _dot(jnp.concatenate(ms, axis=1), bd_ref[g])
        st = st_ref[g]
        yg = yg + _dot(cg, st.astype(BF16)) * decay_full[:, gsl]
        st_ref[g] = st * state_decay[:, gsl] + _dot(bg_t, xw_b[:, gsl])
        ys.append(yg)
    y = jnp.concatenate(ys, axis=1) + xs * dskip
    yz = y * _silu(z)
    outs = []
    for g in range(SSM_GROUPS):
        outs.append(_rms(yz[:, g * GROUP_DIM:(g + 1) * GROUP_DIM]))
    return jnp.concatenate(outs, axis=1) * norm_g


def _conv_silu(xp_ref, rows, convw, convb):
    acc = convb
    for k in range(CONV_WIDTH):
        acc = acc + xp_ref[SUBLANES - (CONV_WIDTH - 1) + k:SUBLANES - (CONV_WIDTH - 1) + k + rows, :] * convw[k:k + 1, :]
    return _silu(acc)


def _interleave_rows(xb, q, inverse=False):
    row = lax.broadcasted_iota(jnp.int32, (q, q), 0)
    col = lax.broadcasted_iota(jnp.int32, (q, q), 1)
    hit = (row == _time_of_row(col, q)) if inverse else (col == _time_of_row(row, q))
    perm = jnp.where(hit, 1.0, 0.0).astype(BF16)
    return jnp.concatenate([_dot(perm, xb[b0:b0 + q]).astype(BF16) for b0 in range(0, xb.shape[0], q)], axis=0)


def _conv_silu_interleaved(xp_ref, blk, prev, convw, convb):
    q = xp_ref.shape[1] - CONV_HALO
    first = lax.broadcasted_iota(jnp.int32, (SUBLANES, CONV_DIM), 0) == 0
    for j in range(CONV_WIDTH - 1):
        js = slice(j * SUBLANES, (j + 1) * SUBLANES)
        cur_j = xp_ref[blk, q + j * SUBLANES:q + (j + 1) * SUBLANES, :]
        xp_ref[blk, js, :] = jnp.where(first, pltpu.roll(prev[js], 1, 0), pltpu.roll(cur_j, 1, 0))
    acc = convb
    for k in range(CONV_WIDTH):
        acc = acc + xp_ref[blk, k * SUBLANES:k * SUBLANES + q, :] * convw[k:k + 1, :]
    return _silu(acc)


def _ssd_project_jobs(get_xb, get_xb_tokens, wz_ref, wxbc_ref, wdt_ref, wg_ref, bg_ref, dtb_ref,
                      z_ref, xp_ref, dt_ref, gate_ref):
    tile = SSD_PROJ_TILE
    q = xp_ref.shape[1] - CONV_HALO

    def xbc_job(c0):
        def job():
            res = _dot(get_xb(), wxbc_ref[:, c0:c0 + tile])
            for b in range(xp_ref.shape[0]):
                xp_ref[b, CONV_HALO:, c0:c0 + tile] = res[b * q:(b + 1) * q]
        return job

    def z_job(c0):
        def job():
            z_ref[:, c0:c0 + tile] = _dot(get_xb(), wz_ref[:, c0:c0 + tile])
        return job

    def gate_job(c0):
        def job():
            gate_ref[:, c0:c0 + tile] = jax.nn.sigmoid(_dot(get_xb_tokens(), wg_ref[:, c0:c0 + tile])
                                                       + bg_ref[:, c0:c0 + tile])
        return job

    def dt_job():
        dt_ref[...] = _softplus(_dot(get_xb(), wdt_ref[...]) + dtb_ref[...])

    return ([xbc_job(c0) for c0 in range(0, CONV_DIM, tile)] + [z_job(c0) for c0 in range(0, D_INNER, tile)]
            + [gate_job(c0) for c0 in range(0, D_MODEL, tile)] + [dt_job])


def _ssd_prompt_kernel(x0_ref, xn_ref, wz_ref, wxbc_ref, wdt_ref, wg_ref, bg_ref, convw_ref, convb_ref, dtb_ref,
                       alog_ref, dskip_ref, ng_ref, e_ref, wproj_ref,
                       a_out_ref, ssm_out_ref, conv_out_ref,
                       st_ref, bd_ref, hist_ref, xbn_ref, z_a, xp_a, dt_a, gate_a, z_b, xp_b, dt_b, gate_b, *,
                       steps_per_row):
    g = pl.program_id(0)
    c = g % steps_per_row
    nblk = xp_a.shape[0]
    proj_w = (wz_ref, wxbc_ref, wdt_ref, wg_ref, bg_ref, dtb_ref)

    @pl.when(g == 0)
    def _():
        bd_ref[...] = jnp.zeros_like(bd_ref)
        xb0 = x0_ref[...].astype(BF16)
        xbn_ref[...] = _interleave_rows(xb0, SSD_Q)
        for job in _ssd_project_jobs(lambda: xbn_ref[...], lambda: xb0, *proj_w, z_a, xp_a, dt_a, gate_a):
            job()

    @pl.when(c == 0)
    def _():
        st_ref[...] = jnp.zeros_like(st_ref)
        hist_ref[...] = jnp.zeros_like(hist_ref)

    def step(cur, nxt):
        z_ref, xp_ref, dt_ref, gate_ref = cur
        xbn_ref[...] = _interleave_rows(xn_ref[...].astype(BF16), SSD_Q)
        jobs = _ssd_project_jobs(lambda: xbn_ref[...], lambda: xn_ref[...].astype(BF16), *proj_w, *nxt)
        neg_a2 = -jnp.exp(alog_ref[...]) * LOG2_E
        yn = []
        for s in range(nblk):
            sl = slice(s * SSD_Q, (s + 1) * SSD_Q)
            prev = hist_ref[...] if s == 0 else xp_ref[s - 1, SSD_Q:, :]
            xc = _conv_silu_interleaved(xp_ref, s, prev, convw_ref[...], convb_ref[...])
            yn.append(_ssd_block(xc, z_ref[sl, :], dt_ref[sl, :], neg_a2, dskip_ref[...], ng_ref[...], e_ref,
                                 st_ref, bd_ref, jobs, interleaved=True))
        hist_ref[...] = xp_ref[nblk - 1, SSD_Q:, :]
        for job in jobs:
            job()
        yn = _interleave_rows(jnp.concatenate(yn, axis=0).astype(BF16), SSD_Q, inverse=True)
        a_out_ref[...] = gate_ref[...] * _dot(yn, wproj_ref[...])

    slot_a, slot_b = (z_a, xp_a, dt_a, gate_a), (z_b, xp_b, dt_b, gate_b)

    @pl.when(g % 2 == 0)
    def _():
        step(slot_a, slot_b)

    @pl.when(g % 2 == 1)
    def _():
        step(slot_b, slot_a)

    @pl.when(c == steps_per_row - 1)
    def _():
        for j in range(CONV_WIDTH - 1):
            conv_out_ref[j:j + 1, :] = hist_ref[(j + 1) * SUBLANES - 1:(j + 1) * SUBLANES, :]
        for k in range(SSM_GROUPS):
            ssm_out_ref[k * GROUP_DIM:(k + 1) * GROUP_DIM, :] = st_ref[k].T


def _ssd_sample_kernel(x_ref, state_ref, cstate_ref, wz_ref, wxbc_ref, wdt_ref, wg_ref, bg_ref, convw_ref, convb_ref,
                       dtb_ref, alog_ref, dskip_ref, ng_ref, e_ref, wproj_ref,
                       a_out_ref, ssm_out_ref, conv_out_ref,
                       st_ref, bd_ref, xp_ref, z_scr, xbc_scr, dt_scr, xcp_scr, zp_scr, dtp_scr, yn_scr):
    nseq, seq = cstate_ref.shape[0], x_ref.shape[0] // cstate_ref.shape[0]
    xb = x_ref[...].astype(BF16)
    z_scr[...] = _dot(xb, wz_ref[...])
    xbc_scr[...] = _dot(xb, wxbc_ref[...])
    dt_scr[...] = _softplus(_dot(xb, wdt_ref[...]) + dtb_ref[...])
    gate = jax.nn.sigmoid(_dot(xb, wg_ref[...]) + bg_ref[...])
    xcp_scr[...] = jnp.zeros_like(xcp_scr)
    zp_scr[...] = jnp.zeros_like(zp_scr)
    dtp_scr[...] = jnp.zeros_like(dtp_scr)
    bd_ref[...] = jnp.zeros_like(bd_ref)
    neg_a2 = -jnp.exp(alog_ref[...]) * LOG2_E
    lo = SUBLANES - (CONV_WIDTH - 1)

    def body(s, carry):
        r0 = pl.multiple_of(s * seq, seq)
        xp_ref[lo:SUBLANES, :] = cstate_ref[s]
        xp_ref[SUBLANES:SUBLANES + seq, :] = xbc_scr[pl.ds(r0, seq), :]
        xcp_scr[0:seq, :] = _conv_silu(xp_ref, seq, convw_ref[...], convb_ref[...])
        conv_out_ref[s] = xp_ref[seq + lo:seq + SUBLANES, :]
        zp_scr[0:seq, :] = z_scr[pl.ds(r0, seq), :]
        dtp_scr[0:seq, :] = dt_scr[pl.ds(r0, seq), :]
        for g in range(SSM_GROUPS):
            st_ref[g] = state_ref[s, g * GROUP_DIM:(g + 1) * GROUP_DIM, :].T
        yn = _ssd_block(xcp_scr[...], zp_scr[...], dtp_scr[...], neg_a2, dskip_ref[...], ng_ref[...], e_ref, st_ref,
                        bd_ref)
        yn_scr[pl.ds(r0, seq), :] = yn[0:seq].astype(BF16)
        for g in range(SSM_GROUPS):
            ssm_out_ref[s, g * GROUP_DIM:(g + 1) * GROUP_DIM, :] = st_ref[g].T
        return carry

    lax.fori_loop(0, nseq, body, 0)
    a_out_ref[...] = gate * _dot(yn_scr[...], wproj_ref[...])


def _ssd_weight_specs(w):
    names = ("w_z", "w_xbc", "w_dt", "w_gs", "b_gs", "conv_w", "conv_b", "dt_bias", "a_log", "d_skip", "norm_g",
             "expand", "w_ssm_proj")
    arrs = [w[n] for n in names]
    return arrs, [_const_spec(a.shape) for a in arrs]


def _ssd_prompt(x, w):
    b, l, d = x.shape
    rows = min(SSD_ROWS, l)
    assert l % rows == 0 and rows % SSD_Q == 0
    arrs, specs = _ssd_weight_specs(w)
    nc = l // rows
    last = b * nc - 1
    slot = [pltpu.VMEM((rows, D_INNER), F32), pltpu.VMEM((rows // SSD_Q, CONV_HALO + SSD_Q, CONV_DIM), F32),
            pltpu.VMEM((rows, LANES), F32), pltpu.VMEM((rows, D_MODEL), F32)]

    def next_block(g):
        n = jnp.minimum(g + 1, last)
        return (n // nc, n % nc, 0)

    return pl.pallas_call(
        functools.partial(_ssd_prompt_kernel, steps_per_row=nc),
        grid=(b * nc,),
        in_specs=[pl.BlockSpec((None, rows, d), lambda g: (0, 0, 0)),
                  pl.BlockSpec((None, rows, d), next_block)] + specs,
        out_specs=[pl.BlockSpec((None, rows, D_MODEL), lambda g: (g // nc, g % nc, 0)),
                   pl.BlockSpec((None, D_INNER, SSM_STATE), lambda g: (g // nc, 0, 0)),
                   pl.BlockSpec((None, CONV_WIDTH - 1, CONV_DIM), lambda g: (g // nc, 0, 0))],
        out_shape=[jax.ShapeDtypeStruct((b, l, D_MODEL), F32),
                   jax.ShapeDtypeStruct((b, D_INNER, SSM_STATE), F32),
                   jax.ShapeDtypeStruct((b, CONV_WIDTH - 1, CONV_DIM), F32)],
        scratch_shapes=[pltpu.VMEM((SSM_GROUPS, SSM_STATE, GROUP_DIM), F32),
                        pltpu.VMEM((SSM_GROUPS, HEADS_PER_GROUP * SSD_Q, GROUP_DIM), BF16),
                        pltpu.VMEM((CONV_HALO, CONV_DIM), F32),
                        pltpu.VMEM((rows, d), BF16)] + slot + slot,
        compiler_params=_params(1),
        name="ssd_prompt",
    )(x, x, *arrs)


def _ssd_sample(x, state, cstate, w):
    b, l, d = x.shape
    nseq = min(SSD_SAMPLE_SEQS, b)
    assert b % nseq == 0 and l % SUBLANES == 0 and l <= SSD_Q
    rows = nseq * l
    arrs, specs = _ssd_weight_specs(w)
    a, ssm, conv = pl.pallas_call(
        _ssd_sample_kernel,
        grid=(b // nseq,),
        in_specs=[pl.BlockSpec((rows, d), lambda i: (i, 0)),
                  pl.BlockSpec((nseq, D_INNER, SSM_STATE), lambda i: (i, 0, 0)),
                  pl.BlockSpec((nseq, CONV_WIDTH - 1, CONV_DIM), lambda i: (i, 0, 0))] + specs,
        out_specs=[pl.BlockSpec((rows, D_MODEL), lambda i: (i, 0)),
                   pl.BlockSpec((nseq, D_INNER, SSM_STATE), lambda i: (i, 0, 0)),
                   pl.BlockSpec((nseq, CONV_WIDTH - 1, CONV_DIM), lambda i: (i, 0, 0))],
        out_shape=[jax.ShapeDtypeStruct((b * l, D_MODEL), F32),
                   jax.ShapeDtypeStruct((b, D_INNER, SSM_STATE), F32),
                   jax.ShapeDtypeStruct((b, CONV_WIDTH - 1, CONV_DIM), F32)],
        scratch_shapes=[pltpu.VMEM((SSM_GROUPS, SSM_STATE, GROUP_DIM), F32),
                        pltpu.VMEM((SSM_GROUPS, HEADS_PER_GROUP * SSD_Q, GROUP_DIM), BF16),
                        pltpu.VMEM((l + SUBLANES, CONV_DIM), F32),
                        pltpu.VMEM((rows, D_INNER), F32),
                        pltpu.VMEM((rows, CONV_DIM), F32),
                        pltpu.VMEM((rows, LANES), F32),
                        pltpu.VMEM((SSD_Q, CONV_DIM), F32),
                        pltpu.VMEM((SSD_Q, D_INNER), F32),
                        pltpu.VMEM((SSD_Q, LANES), F32),
                        pltpu.VMEM((rows, D_INNER), BF16)],
        compiler_params=_params(1),
        name="ssd_sample",
    )(x.reshape(b * l, d), state.reshape(b, D_INNER, SSM_STATE), cstate, *arrs)
    return a, ssm, conv


def _mla_keys(xb, cos, sin, wkva_ref, kvg_ref):
    kva = _dot(xb, wkva_ref[...])
    ckv = _rms(kva[:, :KV_RANK]) * kvg_ref[...]
    kpe = (kva[:, KV_RANK:KV_RANK + QK_ROPE] * cos[:, :QK_ROPE]
           + kva[:, KV_RANK + LANES:KV_RANK + LANES + QK_ROPE] * sin[:, :QK_ROPE])
    return ckv, kpe


def _mla_queries(xb, cos, sin, wqa_ref, qg_ref, wqb_ref, wukt_ref, store):
    qn = (_rms(_dot(xb, wqa_ref[...])) * qg_ref[...]).astype(BF16)
    q = _dot(qn, wqb_ref[...])
    nope_w = MLA_HEADS * QK_NOPE
    rope_w = MLA_HEADS * QK_ROPE
    for p in range(MLA_HEADS // 2):
        sl = slice(nope_w + p * LANES, nope_w + (p + 1) * LANES)
        sl_sw = slice(nope_w + rope_w + p * LANES, nope_w + rope_w + (p + 1) * LANES)
        pair = (q[:, sl] * cos + q[:, sl_sw] * sin) * ATTN_SCALE
        pair_hi = pltpu.roll(pair, QK_ROPE, 1)
        for h, pe in ((2 * p, pair), (2 * p + 1, pair_hi)):
            q_lat = _dot(q[:, h * QK_NOPE:(h + 1) * QK_NOPE].astype(BF16), wukt_ref[h]) * ATTN_SCALE
            store(h, q_lat.astype(BF16), pe[:, :QK_ROPE].astype(BF16))


def _lane_tile(t, width):
    return jnp.concatenate([t] * (width // LANES), axis=1) if width > LANES else t


def _softmax_update(s, v_b, m_scr, l_scr, acc_scr):
    m_prev = m_scr[...]
    m_new = jnp.maximum(m_prev, jnp.max(s, axis=-1, keepdims=True))
    alpha = jnp.exp(m_prev - m_new)
    p = jnp.exp(s - _lane_tile(m_new, s.shape[1]))
    p_lanes = p[:, 0:LANES]
    for c in range(1, s.shape[1] // LANES):
        p_lanes = p_lanes + p[:, c * LANES:(c + 1) * LANES]
    l_scr[...] = alpha * l_scr[...] + p_lanes
    acc_scr[...] = _lane_tile(alpha, acc_scr.shape[1]) * acc_scr[...] + _dot(p.astype(BF16), v_b)
    m_scr[...] = m_new


def _softmax_finish(l_scr, acc_scr):
    return acc_scr[...] / jnp.sum(l_scr[...], axis=-1, keepdims=True)


def _mla_out(o_of_head, xb, wuv_ref, wmla_ref, wg_ref, bg_ref, o_scr):
    for h in range(MLA_HEADS):
        o_scr[:, h * V_HEAD:(h + 1) * V_HEAD] = _dot(o_of_head(h), wuv_ref[h]).astype(BF16)
    gate = jax.nn.sigmoid(_dot(xb, wg_ref[...]) + bg_ref[...])
    return gate * _dot(o_scr[...], wmla_ref[...])


def _mla_pre_kernel(x_ref, cos_ref, sin_ref, wqa_ref, qg_ref, wkva_ref, kvg_ref, qn_out_ref, ckv_out_ref, kpe_out_ref):
    xb = x_ref[...].astype(BF16)
    ckv, kpe = _mla_keys(xb, cos_ref[...], sin_ref[...], wkva_ref, kvg_ref)
    ckv_out_ref[...] = ckv
    kpe_out_ref[...] = kpe
    qn_out_ref[...] = (_rms(_dot(xb, wqa_ref[...])) * qg_ref[...]).astype(BF16)


def _mla_prompt_attn_kernel(qn_ref, ckv_ref, kpe_ref, cos_ref, sin_ref, wqp_ref, wkv_ref, o_ref,
                            qt_scr, k_scr, vt_scr, acc_scr):
    l = qn_ref.shape[0]
    g = min(MLA_BLK, l)
    nb = l // g
    half = QK_ROPE
    kv = _dot(ckv_ref[...].astype(BF16), wkv_ref[...])
    kpe_b = kpe_ref[...].astype(BF16)
    q = _dot(qn_ref[...], wqp_ref[...])
    c = ATTN_SCALE * LOG2_E
    pair_t = ((q[:, 2 * QK_NOPE:3 * QK_NOPE] * cos_ref[...] + q[:, 3 * QK_NOPE:] * sin_ref[...]) * c).T
    for hi in range(2):
        k_scr[hi, :, 0:QK_NOPE] = kv[:, hi * QK_NOPE:(hi + 1) * QK_NOPE].astype(BF16)
        k_scr[hi, :, QK_NOPE:QK_NOPE + half] = kpe_b
        k_scr[hi, :, QK_NOPE + half:] = jnp.zeros((l, LANES - half), BF16)
        vt_scr[hi] = kv[:, 2 * QK_NOPE + hi * V_HEAD:2 * QK_NOPE + (hi + 1) * V_HEAD].T.astype(BF16)
        qt_scr[hi, 0:QK_NOPE, :] = (q[:, hi * QK_NOPE:(hi + 1) * QK_NOPE] * c).T.astype(BF16)
        qt_scr[hi, QK_NOPE:QK_NOPE + half, :] = pair_t[hi * half:(hi + 1) * half].astype(BF16)
        qt_scr[hi, QK_NOPE + half:, :] = jnp.zeros((LANES - half, l), BF16)
    key = lax.broadcasted_iota(jnp.int32, (g, g), 0)
    qry = lax.broadcasted_iota(jnp.int32, (g, g), 1)
    diag_visible = (qry >> CHUNK_SHIFT) >= (key >> CHUNK_SHIFT)
    m, lsum = [None, None], [None, None]
    for j in range(nb):
        r0 = j * g
        for hi in range(2):
            s_t = _dot(k_scr[hi, r0:r0 + g, :], qt_scr[hi, :, r0:])
            left = jnp.where(diag_visible, s_t[:, :g], NEG_INF)
            s_t = left if j == nb - 1 else jnp.concatenate([left, s_t[:, g:]], axis=1)
            m_cur = jnp.max(s_t, axis=0, keepdims=True)
            if j == 0:
                m_new = m_cur
            else:
                m_prev = m[hi][:, r0:]
                m_new = jnp.maximum(m_prev, m_cur)
                alpha = jnp.exp2(m_prev - m_new)
            p_t = jnp.exp2(s_t - m_new)
            psum = jnp.sum(p_t, axis=0, keepdims=True)
            pv_t = _dot(vt_scr[hi, :, r0:r0 + g], p_t.astype(BF16))
            if j == 0:
                m[hi], lsum[hi] = m_new, psum
                acc_scr[hi] = pv_t
            else:
                m[hi] = jnp.concatenate([m[hi][:, :r0], m_new], axis=1)
                lsum[hi] = jnp.concatenate([lsum[hi][:, :r0], alpha * lsum[hi][:, r0:] + psum], axis=1)
                acc_scr[hi, :, r0:] = alpha * acc_scr[hi, :, r0:] + pv_t
    for hi in range(2):
        o_ref[:, hi * V_HEAD:(hi + 1) * V_HEAD] = (acc_scr[hi] * (1.0 / lsum[hi])).T.astype(BF16)


def _mla_post_kernel(o_ref, x_ref, wmla_ref, wg_ref, bg_ref, a_out_ref):
    gate = jax.nn.sigmoid(_dot(x_ref[...].astype(BF16), wg_ref[...]) + bg_ref[...])
    a_out_ref[...] = gate * _dot(o_ref[...], wmla_ref[...])


def _mla_prompt(x, cos, sin, w):
    b, l, d = x.shape
    g = min(MLA_BLK, l)
    tm = min(MLA_ROWS, l)
    npair = MLA_HEADS // 2
    assert l % g == 0 and g % CHUNK == 0 and g % LANES == 0 and l % tm == 0
    nt = l // tm
    pre = [w[n] for n in ("w_qa", "q_norm_g", "w_kva", "kv_norm_g")]
    qn, ckv, kpe = pl.pallas_call(
        _mla_pre_kernel,
        grid=(b, nt),
        in_specs=[pl.BlockSpec((None, tm, d), lambda i, c: (i, c, 0)),
                  pl.BlockSpec((tm, LANES), lambda i, c: (c, 0)),
                  pl.BlockSpec((tm, LANES), lambda i, c: (c, 0))] + [_const_spec(a.shape) for a in pre],
        out_specs=[pl.BlockSpec((None, tm, Q_RANK), lambda i, c: (i, c, 0)),
                   pl.BlockSpec((None, tm, KV_RANK), lambda i, c: (i, c, 0)),
                   pl.BlockSpec((None, tm, QK_ROPE), lambda i, c: (i, c, 0))],
        out_shape=[jax.ShapeDtypeStruct((b, l, Q_RANK), BF16),
                   jax.ShapeDtypeStruct((b, l, KV_RANK), F32),
                   jax.ShapeDtypeStruct((b, l, QK_ROPE), F32)],
        compiler_params=_params(2),
        name="mla_pre",
    )(x, cos, sin, *pre)
    o = pl.pallas_call(
        _mla_prompt_attn_kernel,
        grid=(b, npair),
        in_specs=[pl.BlockSpec((None, l, Q_RANK), lambda i, p: (i, 0, 0)),
                  pl.BlockSpec((None, l, KV_RANK), lambda i, p: (i, 0, 0)),
                  pl.BlockSpec((None, l, QK_ROPE), lambda i, p: (i, 0, 0)),
                  _const_spec(cos.shape), _const_spec(sin.shape),
                  pl.BlockSpec((None, Q_RANK, 4 * QK_NOPE), lambda i, p: (p, 0, 0)),
                  pl.BlockSpec((None, KV_RANK, 2 * QK_NOPE + 2 * V_HEAD), lambda i, p: (p, 0, 0))],
        out_specs=pl.BlockSpec((None, l, 2 * V_HEAD), lambda i, p: (i, 0, p)),
        out_shape=jax.ShapeDtypeStruct((b, l, MLA_HEADS * V_HEAD), BF16),
        scratch_shapes=[pltpu.VMEM((2, 2 * LANES, l), BF16),
                        pltpu.VMEM((2, l, 2 * LANES), BF16),
                        pltpu.VMEM((2, V_HEAD, l), BF16),
                        pltpu.VMEM((2, V_HEAD, l), F32)],
        compiler_params=_params(2),
        name="mla_prompt_attn",
    )(qn, ckv, kpe, cos, sin, w["w_qpair"], w["w_kvpair"])
    post = [w[n] for n in ("w_mla_proj", "w_gm", "b_gm")]
    a = pl.pallas_call(
        _mla_post_kernel,
        grid=(b, nt),
        in_specs=[pl.BlockSpec((None, tm, MLA_HEADS * V_HEAD), lambda i, c: (i, c, 0)),
                  pl.BlockSpec((None, tm, d), lambda i, c: (i, c, 0))] + [_const_spec(t.shape) for t in post],
        out_specs=pl.BlockSpec((None, tm, D_MODEL), lambda i, c: (i, c, 0)),
        out_shape=jax.ShapeDtypeStruct((b, l, D_MODEL), F32),
        compiler_params=_params(2),
        name="mla_post",
    )(o, x, *post)
    return a, ckv, kpe


def _mla_dec_pre_kernel(x_ref, cos_ref, sin_ref, wqa_ref, qg_ref, wqb_ref, wukt_ref, wkva_ref, kvg_ref,
                        q_out_ref, ckv_out_ref, kpe_out_ref):
    xb = x_ref[...].astype(BF16)
    cos, sin = cos_ref[...], sin_ref[...]
    ckv, kpe = _mla_keys(xb, cos, sin, wkva_ref, kvg_ref)
    ckv_out_ref[...] = ckv
    kpe_out_ref[...] = kpe

    def store(h, q_lat, q_pe):
        q_out_ref[h, :, 0:KV_RANK] = q_lat
        q_out_ref[h, :, KV_RANK:QK_LAT] = q_pe

    _mla_queries(xb, cos, sin, wqa_ref, qg_ref, wqb_ref, wukt_ref, store)


def _mla_dec_attn_kernel(q_ref, cckv_ref, ckpe_t_ref, nckv_ref, nkpe_ref, o_ref,
                         m_scr, l_scr, acc_scr, nk_scr, *, past_len):
    j = pl.program_id(1)
    seq = q_ref.shape[1]
    rows = MLA_HEADS * seq

    @pl.when(j == 0)
    def _():
        m_scr[...] = jnp.full_like(m_scr, NEG_INF)
        l_scr[...] = jnp.zeros_like(l_scr)
        acc_scr[...] = jnp.zeros_like(acc_scr)

    q = q_ref[...].reshape(rows, QK_LAT)
    ck = cckv_ref[...].astype(BF16)
    kp_t = ckpe_t_ref[...].astype(BF16)
    s = _dot_nt(q[:, :KV_RANK], ck) + _dot(q[:, KV_RANK:], kp_t)
    _softmax_update(s, ck, m_scr, l_scr, acc_scr)

    @pl.when(j == pl.num_programs(1) - 1)
    def _():
        nk_scr[...] = jnp.zeros_like(nk_scr)
        nk_scr[0:seq, 0:KV_RANK] = nckv_ref[...].astype(BF16)
        nk_scr[0:seq, KV_RANK:QK_LAT] = nkpe_ref[...].astype(BF16)
        nk = nk_scr[...]
        s2 = _dot_nt(q, nk)
        row = lax.broadcasted_iota(jnp.int32, (rows, 1), 0)
        q_chunk = (past_len + (row & (seq - 1))) >> CHUNK_SHIFT
        k_lane = lax.broadcasted_iota(jnp.int32, (1, LANES), 1)
        visible = (k_lane < seq) & (((past_len + k_lane) >> CHUNK_SHIFT) <= q_chunk)
        _softmax_update(jnp.where(visible, s2, NEG_INF), nk[:, :KV_RANK], m_scr, l_scr, acc_scr)
        o_ref[...] = _softmax_finish(l_scr, acc_scr).astype(BF16).reshape(MLA_HEADS, seq, KV_RANK)


def _mla_dec_post_kernel(o_ref, x_ref, wuv_ref, wmla_ref, wg_ref, bg_ref, a_out_ref, o_scr):
    xb = x_ref[...].astype(BF16)
    a_out_ref[...] = _mla_out(lambda h: o_ref[h], xb, wuv_ref, wmla_ref, wg_ref, bg_ref, o_scr)


def _mla_sample(x, cache_ckv, cache_kpe, cos, sin, w):
    b, l, d = x.shape
    t = b * l
    past_len = cache_ckv.shape[1]
    tm = min(ROW_TILE, t)
    tk = min(DEC_TK, past_len)
    assert t % tm == 0 and past_len % tk == 0 and (l & (l - 1)) == 0 and l <= LANES and l % 16 == 0
    x2 = x.reshape(t, d)
    pre_names = ("w_qa", "q_norm_g", "w_qb", "w_ukt", "w_kva", "kv_norm_g")
    pre = [w[n] for n in pre_names]
    q, ckv, kpe = pl.pallas_call(
        _mla_dec_pre_kernel,
        grid=(t // tm,),
        in_specs=[pl.BlockSpec((tm, d), lambda i: (i, 0)),
                  pl.BlockSpec((tm, LANES), lambda i: (i, 0)),
                  pl.BlockSpec((tm, LANES), lambda i: (i, 0))] + [_const_spec(a.shape) for a in pre],
        out_specs=[pl.BlockSpec((MLA_HEADS, tm, QK_LAT), lambda i: (0, i, 0)),
                   pl.BlockSpec((tm, KV_RANK), lambda i: (i, 0)),
                   pl.BlockSpec((tm, QK_ROPE), lambda i: (i, 0))],
        out_shape=[jax.ShapeDtypeStruct((MLA_HEADS, t, QK_LAT), BF16),
                   jax.ShapeDtypeStruct((t, KV_RANK), F32),
                   jax.ShapeDtypeStruct((t, QK_ROPE), F32)],
        compiler_params=_params(1),
        name="mla_dec_pre",
    )(x2, jnp.tile(cos, (b, 1)), jnp.tile(sin, (b, 1)), *pre)
    rows = MLA_HEADS * l
    o = pl.pallas_call(
        functools.partial(_mla_dec_attn_kernel, past_len=past_len),
        grid=(b, past_len // tk),
        in_specs=[pl.BlockSpec((MLA_HEADS, l, QK_LAT), lambda i, j: (0, i, 0)),
                  pl.BlockSpec((None, tk, KV_RANK), lambda i, j: (i, j, 0)),
                  pl.BlockSpec((None, QK_ROPE, tk), lambda i, j: (i, 0, j)),
                  pl.BlockSpec((l, KV_RANK), lambda i, j: (i, 0)),
                  pl.BlockSpec((l, QK_ROPE), lambda i, j: (i, 0))],
        out_specs=pl.BlockSpec((MLA_HEADS, l, KV_RANK), lambda i, j: (0, i, 0)),
        out_shape=jax.ShapeDtypeStruct((MLA_HEADS, t, KV_RANK), BF16),
        scratch_shapes=[pltpu.VMEM((rows, LANES), F32),
                        pltpu.VMEM((rows, LANES), F32),
                        pltpu.VMEM((rows, KV_RANK), F32),
                        pltpu.VMEM((LANES, QK_LAT), BF16)],
        compiler_params=_params(2),
        name="mla_dec_attn",
    )(q, cache_ckv, jnp.swapaxes(cache_kpe, 1, 2), ckv, kpe)
    post_names = ("w_uv", "w_mla_proj", "w_gm", "b_gm")
    post = [w[n] for n in post_names]
    a = pl.pallas_call(
        _mla_dec_post_kernel,
        grid=(t // tm,),
        in_specs=[pl.BlockSpec((MLA_HEADS, tm, KV_RANK), lambda i: (0, i, 0)),
                  pl.BlockSpec((tm, d), lambda i: (i, 0))] + [_const_spec(a.shape) for a in post],
        out_specs=pl.BlockSpec((tm, D_MODEL), lambda i: (i, 0)),
        out_shape=jax.ShapeDtypeStruct((t, D_MODEL), F32),
        scratch_shapes=[pltpu.VMEM((tm, MLA_HEADS * V_HEAD), BF16)],
        compiler_params=_params(1),
        name="mla_dec_post",
    )(o, x2, *post)
    return a, ckv.reshape(b, l, KV_RANK), kpe.reshape(b, l, QK_ROPE)


def _ffn_kernel(x_ref, as_ref, am_ref, wout_ref, g1_ref, b1_ref, wup_ref, wdown_ref, g2_ref, b2_ref, y_ref, *, alpha):
    sub = min(FFN_SUB_ROWS, x_ref.shape[0])
    for r0 in range(0, x_ref.shape[0], sub):
        rs = pl.ds(r0, sub)
        mix = _dot((as_ref[rs, :] + am_ref[rs, :]).astype(BF16), wout_ref[...])
        h = _layernorm(alpha * x_ref[rs, :] + mix, g1_ref[...], b1_ref[...])
        up = jnp.maximum(_dot(h.astype(BF16), wup_ref[...]), 0.0)
        ff = _dot((up * up).astype(BF16), wdown_ref[...])
        y_ref[rs, :] = _layernorm(alpha * h + ff, g2_ref[...], b2_ref[...])


def _ffn(x2, a_ssm, a_mla, w, alpha):
    t, d = x2.shape
    tm = min(FFN_ROWS, t)
    assert t % tm == 0 and tm % min(FFN_SUB_ROWS, tm) == 0
    names = ("w_out", "ln1_g", "ln1_b", "w_up", "w_down", "ln2_g", "ln2_b")
    arrs = [w[n] for n in names]
    row_spec = pl.BlockSpec((tm, d), lambda i: (i, 0))
    return pl.pallas_call(
        functools.partial(_ffn_kernel, alpha=alpha),
        grid=(t // tm,),
        in_specs=[row_spec, row_spec, row_spec] + [_const_spec(a.shape) for a in arrs],
        out_specs=row_spec,
        out_shape=jax.ShapeDtypeStruct((t, d), F32),
        compiler_params=_params(1),
        name="ffn",
    )(x2, a_ssm, a_mla, *arrs)


def _rope_tables(pos0, n):
    inv = ROPE_THETA ** (-jnp.arange(0, QK_ROPE, 2, dtype=F32) / QK_ROPE)
    ang = (pos0 + jnp.arange(n, dtype=jnp.int32)).astype(F32)[:, None] * inv[None, :]
    cos, sin = jnp.cos(ang), jnp.sin(ang)
    cos2 = jnp.concatenate([cos, cos], axis=-1)
    sin2 = jnp.concatenate([-sin, sin], axis=-1)
    return jnp.tile(cos2, (1, LANES // QK_ROPE)), jnp.tile(sin2, (1, LANES // QK_ROPE))


def _swap_halves(t):
    half = t.shape[-1] // 2
    return jnp.concatenate([t[..., half:], t[..., :half]], axis=-1)


def _layer_weights(w_in, b_gate, conv_w, conv_b, dt_bias, a_log, d_skip, ssm_norm_g, w_ssm_proj, q_norm_g, w_q_b,
                   kv_norm_g, w_uk, w_uv, w_mla_proj, w_out, ln1_g, ln1_b, w_up, w_down, ln2_g, ln2_b):
    d = w_in.shape[0]
    o_z = 2 * D_MODEL
    o_xbc = o_z + D_INNER
    o_dt = o_xbc + CONV_DIM
    o_qa = o_dt + SSM_HEADS
    o_kva = o_qa + Q_RANK
    bf = lambda t: t.astype(BF16)
    row = lambda t: t.reshape(1, -1)
    lane_pad = lambda t: jnp.pad(t, ((0, 0), (0, LANES - t.shape[-1])))
    w_kv = w_in[:, o_kva:]
    w_kpe = w_kv[:, KV_RANK:]
    zpad = jnp.zeros((d, LANES - QK_ROPE), w_in.dtype)
    qb = w_q_b.reshape(Q_RANK, MLA_HEADS, QK_NOPE + QK_ROPE)
    qb_rope = qb[:, :, QK_NOPE:]
    npair = MLA_HEADS // 2
    k = jnp.arange(LANES)[:, None]
    c = jnp.arange(D_INNER)[None, :]
    expand = ((k < 3 * SSM_HEADS) & ((k % SSM_HEADS) == (c // SSM_HEAD_DIM))).astype(BF16)
    return dict(
        w_gs=bf(w_in[:, :D_MODEL]), b_gs=row(b_gate[:D_MODEL]),
        w_gm=bf(w_in[:, D_MODEL:o_z]), b_gm=row(b_gate[D_MODEL:]),
        w_z=bf(w_in[:, o_z:o_xbc]), w_xbc=bf(w_in[:, o_xbc:o_dt]), w_dt=bf(lane_pad(w_in[:, o_dt:o_qa])),
        conv_w=conv_w, conv_b=row(conv_b), dt_bias=lane_pad(row(dt_bias)), a_log=lane_pad(row(a_log)),
        d_skip=row(jnp.repeat(d_skip, SSM_HEAD_DIM)), norm_g=row(ssm_norm_g), expand=expand,
        w_ssm_proj=bf(w_ssm_proj),
        w_qa=bf(w_in[:, o_qa:o_kva]), q_norm_g=row(q_norm_g),
        w_qb=bf(jnp.concatenate([qb[:, :, :QK_NOPE].reshape(Q_RANK, -1), qb_rope.reshape(Q_RANK, -1),
                                 _swap_halves(qb_rope).reshape(Q_RANK, -1)], axis=1)),
        w_ukt=bf(jnp.transpose(w_uk, (1, 2, 0))),
        w_qpair=bf(jnp.transpose(jnp.concatenate(
            [qb[:, :, :QK_NOPE].reshape(Q_RANK, npair, 2 * QK_NOPE), qb_rope.reshape(Q_RANK, npair, 2 * QK_ROPE),
             _swap_halves(qb_rope).reshape(Q_RANK, npair, 2 * QK_ROPE)], axis=2), (1, 0, 2))),
        w_kvpair=bf(jnp.transpose(jnp.concatenate(
            [w_uk.reshape(KV_RANK, npair, 2 * QK_NOPE), w_uv.reshape(KV_RANK, npair, 2 * V_HEAD)], axis=2), (1, 0, 2))),
        w_kva=bf(jnp.concatenate([w_kv[:, :KV_RANK], w_kpe, zpad, _swap_halves(w_kpe), zpad], axis=1)),
        kv_norm_g=row(kv_norm_g), w_uv=bf(jnp.transpose(w_uv, (1, 0, 2))), w_mla_proj=bf(w_mla_proj),
        w_out=bf(w_out), ln1_g=row(ln1_g), ln1_b=row(ln1_b), w_up=bf(w_up), w_down=bf(w_down),
        ln2_g=row(ln2_g), ln2_b=row(ln2_b))


def _prompt_layer(x, w, alpha):
    b, l, d = x.shape
    cos, sin = _rope_tables(0, l)
    a_ssm, ssm, conv = _ssd_prompt(x, w)
    a_mla, ckv, kpe = _mla_prompt(x, cos, sin, w)
    y = _ffn(x.reshape(b * l, d), a_ssm.reshape(b * l, d), a_mla.reshape(b * l, d), w, alpha)
    return y.reshape(b, l, d), ckv, kpe, ssm.reshape(b, SSM_HEADS, SSM_HEAD_DIM, SSM_STATE), conv


def _sample_layer(x, cache_ckv, cache_kpe, state_ssm, state_conv, w, alpha):
    b, l, d = x.shape
    cos, sin = _rope_tables(cache_ckv.shape[1], l)
    a_ssm, ssm, conv = _ssd_sample(x, state_ssm, state_conv, w)
    a_mla, ckv, kpe = _mla_sample(x, cache_ckv, cache_kpe, cos, sin, w)
    y = _ffn(x.reshape(b * l, d), a_ssm, a_mla, w, alpha)
    return y.reshape(b, l, d), ckv, kpe, ssm.reshape(b, SSM_HEADS, SSM_HEAD_DIM, SSM_STATE), conv


def kernel(x_prompt, x_sample, cache_ckv, cache_kpe, state_ssm, state_conv, w_in, b_gate, conv_w, conv_b, dt_bias, a_log, d_skip, ssm_norm_g, w_ssm_proj, q_norm_g, w_q_b, kv_norm_g, w_uk, w_uv, w_mla_proj, w_out, ln1_g, ln1_b, w_up, w_down, ln2_g, ln2_b):
    depth = w_in.shape[0]
    alpha = (2 * depth) ** 0.25
    layer_params = (w_in, b_gate, conv_w, conv_b, dt_bias, a_log, d_skip, ssm_norm_g, w_ssm_proj, q_norm_g, w_q_b,
                    kv_norm_g, w_uk, w_uv, w_mla_proj, w_out, ln1_g, ln1_b, w_up, w_down, ln2_g, ln2_b)
    yp, ys = x_prompt, x_sample
    st_p, st_s = [], []
    for i in range(depth):
        w = _layer_weights(*(p[i] for p in layer_params))
        yp, *sp = _prompt_layer(yp, w, alpha)
        ys, *ss = _sample_layer(ys, cache_ckv[i], cache_kpe[i], state_ssm[i], state_conv[i], w, alpha)
        st_p.append(sp)
        st_s.append(ss)
    stack = lambda sts, k: jnp.stack([s[k] for s in sts])
    return (yp, ys, stack(st_p, 0), stack(st_p, 1), stack(st_p, 2), stack(st_p, 3),
            stack(st_s, 0), stack(st_s, 1), stack(st_s, 2), stack(st_s, 3))
```

```python
import functools
import math

import jax
import jax.numpy as jnp
import numpy as np
from jax import lax
from jax.experimental import pallas as pl
from jax.experimental.pallas import tpu as pltpu

D_MODEL = 1024
D_INNER = 2 * D_MODEL
SSM_HEAD_DIM = 64
SSM_HEADS = D_INNER // SSM_HEAD_DIM
SSM_GROUPS = 8
HEADS_PER_GROUP = SSM_HEADS // SSM_GROUPS
GROUP_DIM = HEADS_PER_GROUP * SSM_HEAD_DIM
SSM_STATE = 128
CONV_WIDTH = 4
CONV_DIM = D_INNER + 2 * SSM_GROUPS * SSM_STATE
MLA_HEADS = 16
QK_NOPE = 128
QK_ROPE = 64
V_HEAD = 128
Q_RANK = 512
KV_RANK = 512
QK_LAT = KV_RANK + QK_ROPE
ROPE_THETA = 10000.0
ATTN_SCALE = (QK_NOPE + QK_ROPE) ** -0.5
D_FF = 4 * D_MODEL
RMS_EPS = 1e-6
LN_EPS = 1e-5
CHUNK = 64
CHUNK_SHIFT = 6

LANES = 128
SUBLANES = 8
VMEM_LIMIT = 62 * 1024 * 1024

SSD_Q = 128
SSD_ROWS = 256
SSD_PROJ_TILE = 512
CONV_HALO = (CONV_WIDTH - 1) * SUBLANES
SSD_SAMPLE_SEQS = 4
MLA_BLK = 256
DEC_TK = 2048
ROW_TILE = 512
MLA_ROWS = 1024
FFN_ROWS = 512
FFN_SUB_ROWS = 512

BF16 = jnp.bfloat16
F32 = jnp.float32
NEG_INF = float("-inf")
LOG2_E = math.log2(math.e)


def _dot(a, b):
    return jnp.dot(a, b, preferred_element_type=F32)


def _dot_nt(a, b):
    return lax.dot_general(a, b, (((1,), (1,)), ((), ())), preferred_element_type=F32)


def _rms(t):
    return t * lax.rsqrt(jnp.mean(t * t, axis=-1, keepdims=True) + RMS_EPS)


def _layernorm(t, g, b):
    mu = jnp.mean(t, axis=-1, keepdims=True)
    d = t - mu
    var = jnp.mean(d * d, axis=-1, keepdims=True)
    return d * lax.rsqrt(var + LN_EPS) * g + b


def _silu(t):
    return t * jax.nn.sigmoid(t)


def _softplus(t):
    return jnp.maximum(t, 0.0) + jnp.log(1.0 + jnp.exp(-jnp.abs(t)))


def _const_spec(shape):
    zeros = (0,) * len(shape)
    return pl.BlockSpec(shape, lambda *_: zeros, pipeline_mode=pl.Buffered(1))


def _column_spec(rows, width, start):
    assert start % width == 0
    return pl.BlockSpec((rows, width), lambda *_: (0, start // width), pipeline_mode=pl.Buffered(1))


def _operands(w, names):
    pairs = [w[n] if isinstance(w[n], tuple) else (w[n], _const_spec(w[n].shape)) for n in names]
    return [a for a, _ in pairs], [s for _, s in pairs]


def _params(n_axes):
    return pltpu.CompilerParams(dimension_semantics=("arbitrary",) * n_axes, vmem_limit_bytes=VMEM_LIMIT)


def _expand_heads(v, e_ref):
    hi = v.astype(BF16).astype(F32)
    r1 = v - hi
    mid = r1.astype(BF16).astype(F32)
    lo = r1 - mid
    lane = lax.broadcasted_iota(jnp.int32, v.shape, 1)
    packed = jnp.where(lane < 32, hi,
                       jnp.where(lane < 64, pltpu.roll(mid, 32, 1),
                                 jnp.where(lane < 96, pltpu.roll(lo, 64, 1), 0.0)))
    return _dot(packed.astype(BF16), e_ref[...])


def _time_of_row(r, q):
    return (r & (SUBLANES - 1)) * (q // SUBLANES) + (r >> 3)


def _ssd_block(xc, z, dt, neg_a2, dskip, norm_g, e_ref, st_ref, bd_ref, side_jobs=None, interleaved=False):
    q = xc.shape[0]
    row = lax.broadcasted_iota(jnp.int32, (q, q), 0)
    col = lax.broadcasted_iota(jnp.int32, (q, q), 1)
    if interleaved:
        row, col = _time_of_row(row, q), _time_of_row(col, q)
    causal = row >= col
    da = dt * neg_a2
    acum = jnp.dot(causal.astype(F32), da, precision=lax.Precision.HIGHEST, preferred_element_type=F32)
    acum_t = acum.T
    dt_full = _expand_heads(dt, e_ref)
    acum_full = _expand_heads(acum, e_ref)
    decay_full = jnp.exp2(acum_full)
    alast_full = acum_full[q - 1:q, :]
    xs = xc[:, :D_INNER]
    xdt = xs * dt_full
    xdt_b = xdt.astype(BF16)
    xw_b = (xdt * jnp.exp2(alast_full - acum_full)).astype(BF16)
    state_decay = jnp.exp2(alast_full)
    bm_t = xc[:, D_INNER:D_INNER + SSM_GROUPS * SSM_STATE].T.astype(BF16)
    cm_b = xc[:, D_INNER + SSM_GROUPS * SSM_STATE:].astype(BF16)
    ys = []
    for g in range(SSM_GROUPS):
        if side_jobs:
            side_jobs.pop(0)()
        gsl = slice(g * GROUP_DIM, (g + 1) * GROUP_DIM)
        nsl = slice(g * SSM_STATE, (g + 1) * SSM_STATE)
        cg = cm_b[:, nsl]
        bg_t = bm_t[nsl, :]
        cb = _dot(cg, bg_t)
        ms = []
        for r in range(HEADS_PER_GROUP):
            h = g * HEADS_PER_GROUP + r
            hsl = slice(r * SSM_HEAD_DIM, (r + 1) * SSM_HEAD_DIM)
            bd_ref[g, r * q:(r + 1) * q, hsl] = xdt_b[:, g * GROUP_DIM + r * SSM_HEAD_DIM:g * GROUP_DIM + (r + 1) * SSM_HEAD_DIM]
            seg = acum[:, h:h + 1] - acum_t[h:h + 1, :]
            ms.append((cb * jnp.exp2(jnp.where(causal, seg, NEG_INF))).astype(BF16))
        yg = _dot(jnp.concatenate(ms, axis=1), bd_ref[g])
        st = st_ref[g]
        yg = yg + _dot(cg, st.astype(BF16)) * decay_full[:, gsl]
        st_ref[g] = st * state_decay[:, gsl] + _dot(bg_t, xw_b[:, gsl])
        ys.append(yg)
    y = jnp.concatenate(ys, axis=1) + xs * dskip
    yz = y * _silu(z)
    outs = []
    for g in range(SSM_GROUPS):
        outs.append(_rms(yz[:, g * GROUP_DIM:(g + 1) * GROUP_DIM]))
    return jnp.concatenate(outs, axis=1) * norm_g


def _conv_silu(xp_ref, rows, convw, convb):
    acc = convb
    for k in range(CONV_WIDTH):
        acc = acc + xp_ref[SUBLANES - (CONV_WIDTH - 1) + k:SUBLANES - (CONV_WIDTH - 1) + k + rows, :] * convw[k:k + 1, :]
    return _silu(acc)


def _interleave_rows(xb, q, inverse=False):
    row = lax.broadcasted_iota(jnp.int32, (q, q), 0)
    col = lax.broadcasted_iota(jnp.int32, (q, q), 1)
    hit = (row == _time_of_row(col, q)) if inverse else (col == _time_of_row(row, q))
    perm = jnp.where(hit, 1.0, 0.0).astype(BF16)
    return jnp.concatenate([_dot(perm, xb[b0:b0 + q]).astype(BF16) for b0 in range(0, xb.shape[0], q)], axis=0)


def _conv_silu_interleaved(xp_ref, blk, prev, convw, convb):
    q = xp_ref.shape[1] - CONV_HALO
    first = lax.broadcasted_iota(jnp.int32, (SUBLANES, CONV_DIM), 0) == 0
    for j in range(CONV_WIDTH - 1):
        js = slice(j * SUBLANES, (j + 1) * SUBLANES)
        cur_j = xp_ref[blk, q + j * SUBLANES:q + (j + 1) * SUBLANES, :]
        xp_ref[blk, js, :] = jnp.where(first, pltpu.roll(prev[js], 1, 0), pltpu.roll(cur_j, 1, 0))
    acc = convb
    for k in range(CONV_WIDTH):
        acc = acc + xp_ref[blk, k * SUBLANES:k * SUBLANES + q, :] * convw[k:k + 1, :]
    return _silu(acc)


def _ssd_project_jobs(get_xb, get_xb_tokens, wz_ref, wxbc_ref, wdt_ref, wg_ref, bg_ref, dtb_ref,
                      z_ref, xp_ref, dt_ref, gate_ref):
    tile = SSD_PROJ_TILE
    q = xp_ref.shape[1] - CONV_HALO

    def xbc_job(c0):
        def job():
            res = _dot(get_xb(), wxbc_ref[:, c0:c0 + tile])
            for b in range(xp_ref.shape[0]):
                xp_ref[b, CONV_HALO:, c0:c0 + tile] = res[b * q:(b + 1) * q]
        return job

    def z_job(c0):
        def job():
            z_ref[:, c0:c0 + tile] = _dot(get_xb(), wz_ref[:, c0:c0 + tile])
        return job

    def gate_job(c0):
        def job():
            gate_ref[:, c0:c0 + tile] = jax.nn.sigmoid(_dot(get_xb_tokens(), wg_ref[:, c0:c0 + tile])
                                                       + bg_ref[:, c0:c0 + tile])
        return job

    def dt_job():
        dt_ref[...] = _softplus(_dot(get_xb(), wdt_ref[...]) + dtb_ref[...])

    return ([xbc_job(c0) for c0 in range(0, CONV_DIM, tile)] + [z_job(c0) for c0 in range(0, D_INNER, tile)]
            + [gate_job(c0) for c0 in range(0, D_MODEL, tile)] + [dt_job])


def _ssd_prompt_kernel(x0_ref, xn_ref, wz_ref, wxbc_ref, wdt_ref, wg_ref, bg_ref, convw_ref, convb_ref, dtb_ref,
                       alog_ref, dskip_ref, ng_ref, e_ref, wproj_ref,
                       a_out_ref, ssm_out_ref, conv_out_ref,
                       st_ref, bd_ref, hist_ref, xbn_ref, z_a, xp_a, dt_a, gate_a, z_b, xp_b, dt_b, gate_b, *,
                       steps_per_row):
    g = pl.program_id(0)
    c = g % steps_per_row
    nblk = xp_a.shape[0]
    proj_w = (wz_ref, wxbc_ref, wdt_ref, wg_ref, bg_ref, dtb_ref)

    @pl.when(g == 0)
    def _():
        bd_ref[...] = jnp.zeros_like(bd_ref)
        xb0 = x0_ref[...].astype(BF16)
        xbn_ref[...] = _interleave_rows(xb0, SSD_Q)
        for job in _ssd_project_jobs(lambda: xbn_ref[...], lambda: xb0, *proj_w, z_a, xp_a, dt_a, gate_a):
            job()

    @pl.when(c == 0)
    def _():
        st_ref[...] = jnp.zeros_like(st_ref)
        hist_ref[...] = jnp.zeros_like(hist_ref)

    def step(cur, nxt):
        z_ref, xp_ref, dt_ref, gate_ref = cur
        xbn_ref[...] = _interleave_rows(xn_ref[...].astype(BF16), SSD_Q)
        jobs = _ssd_project_jobs(lambda: xbn_ref[...], lambda: xn_ref[...].astype(BF16), *proj_w, *nxt)
        neg_a2 = -jnp.exp(alog_ref[...]) * LOG2_E
        yn = []
        for s in range(nblk):
            sl = slice(s * SSD_Q, (s + 1) * SSD_Q)
            prev = hist_ref[...] if s == 0 else xp_ref[s - 1, SSD_Q:, :]
            xc = _conv_silu_interleaved(xp_ref, s, prev, convw_ref[...], convb_ref[...])
            yn.append(_ssd_block(xc, z_ref[sl, :], dt_ref[sl, :], neg_a2, dskip_ref[...], ng_ref[...], e_ref,
                                 st_ref, bd_ref, jobs, interleaved=True))
        hist_ref[...] = xp_ref[nblk - 1, SSD_Q:, :]
        for job in jobs:
            job()
        yn = _interleave_rows(jnp.concatenate(yn, axis=0).astype(BF16), SSD_Q, inverse=True)
        a_out_ref[...] = gate_ref[...] * _dot(yn, wproj_ref[...])

    slot_a, slot_b = (z_a, xp_a, dt_a, gate_a), (z_b, xp_b, dt_b, gate_b)

    @pl.when(g % 2 == 0)
    def _():
        step(slot_a, slot_b)

    @pl.when(g % 2 == 1)
    def _():
        step(slot_b, slot_a)

    @pl.when(c == steps_per_row - 1)
    def _():
        for j in range(CONV_WIDTH - 1):
            conv_out_ref[j:j + 1, :] = hist_ref[(j + 1) * SUBLANES - 1:(j + 1) * SUBLANES, :]
        for k in range(SSM_GROUPS):
            ssm_out_ref[k * GROUP_DIM:(k + 1) * GROUP_DIM, :] = st_ref[k].T


def _ssd_sample_kernel(x_ref, state_ref, cstate_ref, wz_ref, wxbc_ref, wdt_ref, wg_ref, bg_ref, convw_ref, convb_ref,
                       dtb_ref, alog_ref, dskip_ref, ng_ref, e_ref, wproj_ref,
                       a_out_ref, ssm_out_ref, conv_out_ref,
                       st_ref, bd_ref, xp_ref, z_scr, xbc_scr, dt_scr, xcp_scr, zp_scr, dtp_scr, yn_scr):
    nseq, seq = cstate_ref.shape[0], x_ref.shape[0] // cstate_ref.shape[0]
    xb = x_ref[...].astype(BF16)
    z_scr[...] = _dot(xb, wz_ref[...])
    xbc_scr[...] = _dot(xb, wxbc_ref[...])
    dt_scr[...] = _softplus(_dot(xb, wdt_ref[...]) + dtb_ref[...])
    gate = jax.nn.sigmoid(_dot(xb, wg_ref[...]) + bg_ref[...])
    xcp_scr[...] = jnp.zeros_like(xcp_scr)
    zp_scr[...] = jnp.zeros_like(zp_scr)
    dtp_scr[...] = jnp.zeros_like(dtp_scr)
    bd_ref[...] = jnp.zeros_like(bd_ref)
    neg_a2 = -jnp.exp(alog_ref[...]) * LOG2_E
    lo = SUBLANES - (CONV_WIDTH - 1)

    def body(s, carry):
        r0 = pl.multiple_of(s * seq, seq)
        xp_ref[lo:SUBLANES, :] = cstate_ref[s]
        xp_ref[SUBLANES:SUBLANES + seq, :] = xbc_scr[pl.ds(r0, seq), :]
        xcp_scr[0:seq, :] = _conv_silu(xp_ref, seq, convw_ref[...], convb_ref[...])
        conv_out_ref[s] = xp_ref[seq + lo:seq + SUBLANES, :]
        zp_scr[0:seq, :] = z_scr[pl.ds(r0, seq), :]
        dtp_scr[0:seq, :] = dt_scr[pl.ds(r0, seq), :]
        for g in range(SSM_GROUPS):
            st_ref[g] = state_ref[s, g * GROUP_DIM:(g + 1) * GROUP_DIM, :].T
        yn = _ssd_block(xcp_scr[...], zp_scr[...], dtp_scr[...], neg_a2, dskip_ref[...], ng_ref[...], e_ref, st_ref,
                        bd_ref)
        yn_scr[pl.ds(r0, seq), :] = yn[0:seq].astype(BF16)
        for g in range(SSM_GROUPS):
            ssm_out_ref[s, g * GROUP_DIM:(g + 1) * GROUP_DIM, :] = st_ref[g].T
        return carry

    lax.fori_loop(0, nseq, body, 0)
    a_out_ref[...] = gate * _dot(yn_scr[...], wproj_ref[...])


def _ssd_weight_specs(w):
    names = ("w_z", "w_xbc", "w_dt", "w_gs", "b_gs", "conv_w", "conv_b", "dt_bias", "a_log", "d_skip", "norm_g",
             "expand", "w_ssm_proj")
    return _operands(w, names)


def _ssd_prompt(x, w):
    b, l, d = x.shape
    rows = min(SSD_ROWS, l)
    assert l % rows == 0 and rows % SSD_Q == 0
    arrs, specs = _ssd_weight_specs(w)
    nc = l // rows
    last = b * nc - 1
    slot = [pltpu.VMEM((rows, D_INNER), F32), pltpu.VMEM((rows // SSD_Q, CONV_HALO + SSD_Q, CONV_DIM), F32),
            pltpu.VMEM((rows, LANES), F32), pltpu.VMEM((rows, D_MODEL), F32)]

    def next_block(g):
        n = jnp.minimum(g + 1, last)
        return (n // nc, n % nc, 0)

    return pl.pallas_call(
        functools.partial(_ssd_prompt_kernel, steps_per_row=nc),
        grid=(b * nc,),
        in_specs=[pl.BlockSpec((None, rows, d), lambda g: (0, 0, 0)),
                  pl.BlockSpec((None, rows, d), next_block)] + specs,
        out_specs=[pl.BlockSpec((None, rows, D_MODEL), lambda g: (g // nc, g % nc, 0)),
                   pl.BlockSpec((None, D_INNER, SSM_STATE), lambda g: (g // nc, 0, 0)),
                   pl.BlockSpec((None, CONV_WIDTH - 1, CONV_DIM), lambda g: (g // nc, 0, 0))],
        out_shape=[jax.ShapeDtypeStruct((b, l, D_MODEL), F32),
                   jax.ShapeDtypeStruct((b, D_INNER, SSM_STATE), F32),
                   jax.ShapeDtypeStruct((b, CONV_WIDTH - 1, CONV_DIM), F32)],
        scratch_shapes=[pltpu.VMEM((SSM_GROUPS, SSM_STATE, GROUP_DIM), F32),
                        pltpu.VMEM((SSM_GROUPS, HEADS_PER_GROUP * SSD_Q, GROUP_DIM), BF16),
                        pltpu.VMEM((CONV_HALO, CONV_DIM), F32),
                        pltpu.VMEM((rows, d), BF16)] + slot + slot,
        compiler_params=_params(1),
        name="ssd_prompt",
    )(x, x, *arrs)


def _ssd_sample(x, state, cstate, w):
    b, l, d = x.shape
    nseq = min(SSD_SAMPLE_SEQS, b)
    assert b % nseq == 0 and l % SUBLANES == 0 and l <= SSD_Q
    rows = nseq * l
    arrs, specs = _ssd_weight_specs(w)
    a, ssm, conv = pl.pallas_call(
        _ssd_sample_kernel,
        grid=(b // nseq,),
        in_specs=[pl.BlockSpec((rows, d), lambda i: (i, 0)),
                  pl.BlockSpec((nseq, D_INNER, SSM_STATE), lambda i: (i, 0, 0)),
                  pl.BlockSpec((nseq, CONV_WIDTH - 1, CONV_DIM), lambda i: (i, 0, 0))] + specs,
        out_specs=[pl.BlockSpec((rows, D_MODEL), lambda i: (i, 0)),
                   pl.BlockSpec((nseq, D_INNER, SSM_STATE), lambda i: (i, 0, 0)),
                   pl.BlockSpec((nseq, CONV_WIDTH - 1, CONV_DIM), lambda i: (i, 0, 0))],
        out_shape=[jax.ShapeDtypeStruct((b * l, D_MODEL), F32),
                   jax.ShapeDtypeStruct((b, D_INNER, SSM_STATE), F32),
                   jax.ShapeDtypeStruct((b, CONV_WIDTH - 1, CONV_DIM), F32)],
        scratch_shapes=[pltpu.VMEM((SSM_GROUPS, SSM_STATE, GROUP_DIM), F32),
                        pltpu.VMEM((SSM_GROUPS, HEADS_PER_GROUP * SSD_Q, GROUP_DIM), BF16),
                        pltpu.VMEM((l + SUBLANES, CONV_DIM), F32),
                        pltpu.VMEM((rows, D_INNER), F32),
                        pltpu.VMEM((rows, CONV_DIM), F32),
                        pltpu.VMEM((rows, LANES), F32),
                        pltpu.VMEM((SSD_Q, CONV_DIM), F32),
                        pltpu.VMEM((SSD_Q, D_INNER), F32),
                        pltpu.VMEM((SSD_Q, LANES), F32),
                        pltpu.VMEM((rows, D_INNER), BF16)],
        compiler_params=_params(1),
        name="ssd_sample",
    )(x.reshape(b * l, d), state.reshape(b, D_INNER, SSM_STATE), cstate, *arrs)
    return a, ssm, conv


def _mla_keys(xb, cos, sin, wkva_ref, kvg_ref):
    kva = _dot(xb, wkva_ref[...])
    ckv = _rms(kva[:, :KV_RANK]) * kvg_ref[...]
    kpe = (kva[:, KV_RANK:KV_RANK + QK_ROPE] * cos[:, :QK_ROPE]
           + kva[:, KV_RANK + LANES:KV_RANK + LANES + QK_ROPE] * sin[:, :QK_ROPE])
    return ckv, kpe


def _mla_queries(xb, cos, sin, wqa_ref, qg_ref, wqb_ref, wukt_ref, store):
    qn = (_rms(_dot(xb, wqa_ref[...])) * qg_ref[...]).astype(BF16)
    q = _dot(qn, wqb_ref[...])
    nope_w = MLA_HEADS * QK_NOPE
    rope_w = MLA_HEADS * QK_ROPE
    for p in range(MLA_HEADS // 2):
        sl = slice(nope_w + p * LANES, nope_w + (p + 1) * LANES)
        sl_sw = slice(nope_w + rope_w + p * LANES, nope_w + rope_w + (p + 1) * LANES)
        pair = (q[:, sl] * cos + q[:, sl_sw] * sin) * ATTN_SCALE
        pair_hi = pltpu.roll(pair, QK_ROPE, 1)
        for h, pe in ((2 * p, pair), (2 * p + 1, pair_hi)):
            q_lat = _dot(q[:, h * QK_NOPE:(h + 1) * QK_NOPE].astype(BF16), wukt_ref[h]) * ATTN_SCALE
            store(h, q_lat.astype(BF16), pe[:, :QK_ROPE].astype(BF16))


def _lane_tile(t, width):
    return jnp.concatenate([t] * (width // LANES), axis=1) if width > LANES else t


def _softmax_update(s, v_b, m_scr, l_scr, acc_scr):
    m_prev = m_scr[...]
    m_new = jnp.maximum(m_prev, jnp.max(s, axis=-1, keepdims=True))
    alpha = jnp.exp(m_prev - m_new)
    p = jnp.exp(s - _lane_tile(m_new, s.shape[1]))
    p_lanes = p[:, 0:LANES]
    for c in range(1, s.shape[1] // LANES):
        p_lanes = p_lanes + p[:, c * LANES:(c + 1) * LANES]
    l_scr[...] = alpha * l_scr[...] + p_lanes
    acc_scr[...] = _lane_tile(alpha, acc_scr.shape[1]) * acc_scr[...] + _dot(p.astype(BF16), v_b)
    m_scr[...] = m_new


def _softmax_finish(l_scr, acc_scr):
    return acc_scr[...] / jnp.sum(l_scr[...], axis=-1, keepdims=True)


def _mla_out(o_of_head, xb, wuv_ref, wmla_ref, wg_ref, bg_ref, o_scr):
    for h in range(MLA_HEADS):
        o_scr[:, h * V_HEAD:(h + 1) * V_HEAD] = _dot(o_of_head(h), wuv_ref[h]).astype(BF16)
    gate = jax.nn.sigmoid(_dot(xb, wg_ref[...]) + bg_ref[...])
    return gate * _dot(o_scr[...], wmla_ref[...])


def _mla_pre_kernel(x_ref, cos_ref, sin_ref, wqa_ref, qg_ref, wkva_ref, kvg_ref, qn_out_ref, ckv_out_ref, kpe_out_ref):
    xb = x_ref[...].astype(BF16)
    ckv, kpe = _mla_keys(xb, cos_ref[...], sin_ref[...], wkva_ref, kvg_ref)
    ckv_out_ref[...] = ckv
    kpe_out_ref[...] = kpe
    qn_out_ref[...] = (_rms(_dot(xb, wqa_ref[...])) * qg_ref[...]).astype(BF16)


def _mla_prompt_attn_kernel(qn_ref, ckv_ref, kpe_ref, cos_ref, sin_ref, wqp_ref, wkv_ref, o_ref,
                            qa_scr, qb_scr, ka_scr, kb_scr, vbd_scr, m_scr, l_scr, acc_scr):
    l = qn_ref.shape[0]
    g = min(MLA_BLK, l)
    nb = l // g
    half = QK_ROPE
    zpad = jnp.zeros((l, LANES - half), BF16)
    kv = _dot(ckv_ref[...].astype(BF16), wkv_ref[...])
    kpe_b = kpe_ref[...].astype(BF16)
    for scr, c0 in ((ka_scr, 0), (kb_scr, QK_NOPE)):
        scr[:, 0:QK_NOPE] = kv[:, c0:c0 + QK_NOPE].astype(BF16)
        scr[:, QK_NOPE:QK_NOPE + half] = kpe_b
        scr[:, QK_NOPE + half:] = zpad
    zv = jnp.zeros((g, V_HEAD), BF16)
    for j in range(nb):
        rs = slice(j * g, (j + 1) * g)
        vbd_scr[j, 0:g, 0:V_HEAD] = kv[rs, 2 * QK_NOPE:2 * QK_NOPE + V_HEAD].astype(BF16)
        vbd_scr[j, 0:g, V_HEAD:] = zv
        vbd_scr[j, g:, 0:V_HEAD] = zv
        vbd_scr[j, g:, V_HEAD:] = kv[rs, 2 * QK_NOPE + V_HEAD:].astype(BF16)
    q = _dot(qn_ref[...], wqp_ref[...])
    c = ATTN_SCALE * LOG2_E
    pair = (q[:, 2 * QK_NOPE:3 * QK_NOPE] * cos_ref[...] + q[:, 3 * QK_NOPE:] * sin_ref[...]) * c
    pair_hi = pltpu.roll(pair, half, 1)
    for scr, c0, pe in ((qa_scr, 0, pair), (qb_scr, QK_NOPE, pair_hi)):
        scr[:, 0:QK_NOPE] = (q[:, c0:c0 + QK_NOPE] * c).astype(BF16)
        scr[:, QK_NOPE:QK_NOPE + half] = pe[:, :half].astype(BF16)
        scr[:, QK_NOPE + half:] = zpad
    row = lax.broadcasted_iota(jnp.int32, (g, g), 0)
    col = lax.broadcasted_iota(jnp.int32, (g, g), 1)
    diag_visible = (row >> CHUNK_SHIFT) >= (col >> CHUNK_SHIFT)
    for j in range(nb):
        r0 = j * g
        ps, alphas = [], []
        for hi, (q_scr, k_scr) in enumerate(((qa_scr, ka_scr), (qb_scr, kb_scr))):
            s = _dot_nt(q_scr[r0:, :], k_scr[r0:r0 + g, :])
            top = jnp.where(diag_visible, s[:g], NEG_INF)
            s = top if j == nb - 1 else jnp.concatenate([top, s[g:]], axis=0)
            m_cur = jnp.max(s, axis=-1, keepdims=True)
            if j == 0:
                m_new = jnp.broadcast_to(m_cur, (l, LANES))
            else:
                m_prev = m_scr[hi, r0:, :]
                m_new = jnp.maximum(m_prev, m_cur)
                alphas.append(jnp.exp2(m_prev - m_new))
            p = jnp.exp2(s - _lane_tile(m_new, g))
            p_lanes = p[:, 0:LANES]
            for cc in range(1, g // LANES):
                p_lanes = p_lanes + p[:, cc * LANES:(cc + 1) * LANES]
            l_scr[hi, r0:, :] = p_lanes if j == 0 else alphas[-1] * l_scr[hi, r0:, :] + p_lanes
            m_scr[hi, r0:, :] = m_new
            ps.append(p.astype(BF16))
        pv = _dot(jnp.concatenate(ps, axis=1), vbd_scr[j])
        if j == 0:
            acc_scr[...] = pv
        else:
            acc_scr[r0:, :] = jnp.concatenate(alphas, axis=1) * acc_scr[r0:, :] + pv
    inv = [1.0 / jnp.sum(l_scr[hi], axis=-1, keepdims=True) for hi in range(2)]
    scale = jnp.concatenate([jnp.broadcast_to(t, (l, V_HEAD)) for t in inv], axis=1)
    o_ref[...] = (acc_scr[...] * scale).astype(BF16)


def _mla_post_kernel(o_ref, x_ref, wmla_ref, wg_ref, bg_ref, a_out_ref):
    gate = jax.nn.sigmoid(_dot(x_ref[...].astype(BF16), wg_ref[...]) + bg_ref[...])
    a_out_ref[...] = gate * _dot(o_ref[...], wmla_ref[...])


def _mla_prompt(x, cos, sin, w):
    b, l, d = x.shape
    g = min(MLA_BLK, l)
    tm = min(MLA_ROWS, l)
    npair = MLA_HEADS // 2
    assert l % g == 0 and g % CHUNK == 0 and g % LANES == 0 and l % tm == 0
    nt = l // tm
    pre = [w[n] for n in ("w_qa", "q_norm_g", "w_kva", "kv_norm_g")]
    qn, ckv, kpe = pl.pallas_call(
        _mla_pre_kernel,
        grid=(b, nt),
        in_specs=[pl.BlockSpec((None, tm, d), lambda i, c: (i, c, 0)),
                  pl.BlockSpec((tm, LANES), lambda i, c: (c, 0)),
                  pl.BlockSpec((tm, LANES), lambda i, c: (c, 0))] + [_const_spec(a.shape) for a in pre],
        out_specs=[pl.BlockSpec((None, tm, Q_RANK), lambda i, c: (i, c, 0)),
                   pl.BlockSpec((None, tm, KV_RANK), lambda i, c: (i, c, 0)),
                   pl.BlockSpec((None, tm, QK_ROPE), lambda i, c: (i, c, 0))],
        out_shape=[jax.ShapeDtypeStruct((b, l, Q_RANK), BF16),
                   jax.ShapeDtypeStruct((b, l, KV_RANK), F32),
                   jax.ShapeDtypeStruct((b, l, QK_ROPE), F32)],
        compiler_params=_params(2),
        name="mla_pre",
    )(x, cos, sin, *pre)
    o = pl.pallas_call(
        _mla_prompt_attn_kernel,
        grid=(b, npair),
        in_specs=[pl.BlockSpec((None, l, Q_RANK), lambda i, p: (i, 0, 0)),
                  pl.BlockSpec((None, l, KV_RANK), lambda i, p: (i, 0, 0)),
                  pl.BlockSpec((None, l, QK_ROPE), lambda i, p: (i, 0, 0)),
                  _const_spec(cos.shape), _const_spec(sin.shape),
                  pl.BlockSpec((None, Q_RANK, 4 * QK_NOPE), lambda i, p: (p, 0, 0)),
                  pl.BlockSpec((None, KV_RANK, 2 * QK_NOPE + 2 * V_HEAD), lambda i, p: (p, 0, 0))],
        out_specs=pl.BlockSpec((None, l, 2 * V_HEAD), lambda i, p: (i, 0, p)),
        out_shape=jax.ShapeDtypeStruct((b, l, MLA_HEADS * V_HEAD), BF16),
        scratch_shapes=[pltpu.VMEM((l, 2 * LANES), BF16)] * 4
        + [pltpu.VMEM((l // g, 2 * g, 2 * V_HEAD), BF16),
           pltpu.VMEM((2, l, LANES), F32),
           pltpu.VMEM((2, l, LANES), F32),
           pltpu.VMEM((l, 2 * V_HEAD), F32)],
        compiler_params=_params(2),
        name="mla_prompt_attn",
    )(qn, ckv, kpe, cos, sin, w["w_qpair"], w["w_kvpair"])
    post, post_specs = _operands(w, ("w_mla_proj", "w_gm", "b_gm"))
    a = pl.pallas_call(
        _mla_post_kernel,
        grid=(b, nt),
        in_specs=[pl.BlockSpec((None, tm, MLA_HEADS * V_HEAD), lambda i, c: (i, c, 0)),
                  pl.BlockSpec((None, tm, d), lambda i, c: (i, c, 0))] + post_specs,
        out_specs=pl.BlockSpec((None, tm, D_MODEL), lambda i, c: (i, c, 0)),
        out_shape=jax.ShapeDtypeStruct((b, l, D_MODEL), F32),
        compiler_params=_params(2),
        name="mla_post",
    )(o, x, *post)
    return a, ckv, kpe


def _mla_dec_pre_kernel(x_ref, cos_ref, sin_ref, wqa_ref, qg_ref, wqb_ref, wukt_ref, wkva_ref, kvg_ref,
                        q_out_ref, ckv_out_ref, kpe_out_ref):
    xb = x_ref[...].astype(BF16)
    cos, sin = cos_ref[...], sin_ref[...]
    ckv, kpe = _mla_keys(xb, cos, sin, wkva_ref, kvg_ref)
    ckv_out_ref[...] = ckv
    kpe_out_ref[...] = kpe

    def store(h, q_lat, q_pe):
        q_out_ref[h, :, 0:KV_RANK] = q_lat
        q_out_ref[h, :, KV_RANK:QK_LAT] = q_pe

    _mla_queries(xb, cos, sin, wqa_ref, qg_ref, wqb_ref, wukt_ref, store)


def _mla_dec_attn_kernel(q_ref, cckv_ref, ckpe_t_ref, nckv_ref, nkpe_ref, o_ref,
                         m_scr, l_scr, acc_scr, nk_scr, *, past_len):
    j = pl.program_id(1)
    seq = q_ref.shape[1]
    rows = MLA_HEADS * seq

    @pl.when(j == 0)
    def _():
        m_scr[...] = jnp.full_like(m_scr, NEG_INF)
        l_scr[...] = jnp.zeros_like(l_scr)
        acc_scr[...] = jnp.zeros_like(acc_scr)

    q = q_ref[...].reshape(rows, QK_LAT)
    ck = cckv_ref[...].astype(BF16)
    kp_t = ckpe_t_ref[...].astype(BF16)
    s = _dot_nt(q[:, :KV_RANK], ck) + _dot(q[:, KV_RANK:], kp_t)
    _softmax_update(s, ck, m_scr, l_scr, acc_scr)

    @pl.when(j == pl.num_programs(1) - 1)
    def _():
        nk_scr[...] = jnp.zeros_like(nk_scr)
        nk_scr[0:seq, 0:KV_RANK] = nckv_ref[...].astype(BF16)
        nk_scr[0:seq, KV_RANK:QK_LAT] = nkpe_ref[...].astype(BF16)
        nk = nk_scr[...]
        s2 = _dot_nt(q, nk)
        row = lax.broadcasted_iota(jnp.int32, (rows, 1), 0)
        q_chunk = (past_len + (row & (seq - 1))) >> CHUNK_SHIFT
        k_lane = lax.broadcasted_iota(jnp.int32, (1, LANES), 1)
        visible = (k_lane < seq) & (((past_len + k_lane) >> CHUNK_SHIFT) <= q_chunk)
        _softmax_update(jnp.where(visible, s2, NEG_INF), nk[:, :KV_RANK], m_scr, l_scr, acc_scr)
        o_ref[...] = _softmax_finish(l_scr, acc_scr).astype(BF16).reshape(MLA_HEADS, seq, KV_RANK)


def _mla_dec_post_kernel(o_ref, x_ref, wuv_ref, wmla_ref, wg_ref, bg_ref, a_out_ref, o_scr):
    xb = x_ref[...].astype(BF16)
    a_out_ref[...] = _mla_out(lambda h: o_ref[h], xb, wuv_ref, wmla_ref, wg_ref, bg_ref, o_scr)


def _mla_sample(x, cache_ckv, cache_kpe, cos, sin, w):
    b, l, d = x.shape
    t = b * l
    past_len = cache_ckv.shape[1]
    tm = min(ROW_TILE, t)
    tk = min(DEC_TK, past_len)
    assert t % tm == 0 and past_len % tk == 0 and (l & (l - 1)) == 0 and l <= LANES and l % 16 == 0
    x2 = x.reshape(t, d)
    pre_names = ("w_qa", "q_norm_g", "w_qb", "w_ukt", "w_kva", "kv_norm_g")
    pre = [w[n] for n in pre_names]
    q, ckv, kpe = pl.pallas_call(
        _mla_dec_pre_kernel,
        grid=(t // tm,),
        in_specs=[pl.BlockSpec((tm, d), lambda i: (i, 0)),
                  pl.BlockSpec((tm, LANES), lambda i: (i, 0)),
                  pl.BlockSpec((tm, LANES), lambda i: (i, 0))] + [_const_spec(a.shape) for a in pre],
        out_specs=[pl.BlockSpec((MLA_HEADS, tm, QK_LAT), lambda i: (0, i, 0)),
                   pl.BlockSpec((tm, KV_RANK), lambda i: (i, 0)),
                   pl.BlockSpec((tm, QK_ROPE), lambda i: (i, 0))],
        out_shape=[jax.ShapeDtypeStruct((MLA_HEADS, t, QK_LAT), BF16),
                   jax.ShapeDtypeStruct((t, KV_RANK), F32),
                   jax.ShapeDtypeStruct((t, QK_ROPE), F32)],
        compiler_params=_params(1),
        name="mla_dec_pre",
    )(x2, jnp.tile(cos, (b, 1)), jnp.tile(sin, (b, 1)), *pre)
    rows = MLA_HEADS * l
    o = pl.pallas_call(
        functools.partial(_mla_dec_attn_kernel, past_len=past_len),
        grid=(b, past_len // tk),
        in_specs=[pl.BlockSpec((MLA_HEADS, l, QK_LAT), lambda i, j: (0, i, 0)),
                  pl.BlockSpec((None, tk, KV_RANK), lambda i, j: (i, j, 0)),
                  pl.BlockSpec((None, QK_ROPE, tk), lambda i, j: (i, 0, j)),
                  pl.BlockSpec((l, KV_RANK), lambda i, j: (i, 0)),
                  pl.BlockSpec((l, QK_ROPE), lambda i, j: (i, 0))],
        out_specs=pl.BlockSpec((MLA_HEADS, l, KV_RANK), lambda i, j: (0, i, 0)),
        out_shape=jax.ShapeDtypeStruct((MLA_HEADS, t, KV_RANK), BF16),
        scratch_shapes=[pltpu.VMEM((rows, LANES), F32),
                        pltpu.VMEM((rows, LANES), F32),
                        pltpu.VMEM((rows, KV_RANK), F32),
                        pltpu.VMEM((LANES, QK_LAT), BF16)],
        compiler_params=_params(2),
        name="mla_dec_attn",
    )(q, cache_ckv, jnp.swapaxes(cache_kpe, 1, 2), ckv, kpe)
    post_names = ("w_uv", "w_mla_proj", "w_gm", "b_gm")
    post, post_specs = _operands(w, post_names)
    a = pl.pallas_call(
        _mla_dec_post_kernel,
        grid=(t // tm,),
        in_specs=[pl.BlockSpec((MLA_HEADS, tm, KV_RANK), lambda i: (0, i, 0)),
                  pl.BlockSpec((tm, d), lambda i: (i, 0))] + post_specs,
        out_specs=pl.BlockSpec((tm, D_MODEL), lambda i: (i, 0)),
        out_shape=jax.ShapeDtypeStruct((t, D_MODEL), F32),
        scratch_shapes=[pltpu.VMEM((tm, MLA_HEADS * V_HEAD), BF16)],
        compiler_params=_params(1),
        name="mla_dec_post",
    )(o, x2, *post)
    return a, ckv.reshape(b, l, KV_RANK), kpe.reshape(b, l, QK_ROPE)


def _ffn_kernel(x_ref, as_ref, am_ref, wout_ref, g1_ref, b1_ref, wup_ref, wdown_ref, g2_ref, b2_ref, y_ref, *, alpha):
    sub = min(FFN_SUB_ROWS, x_ref.shape[0])
    for r0 in range(0, x_ref.shape[0], sub):
        rs = pl.ds(r0, sub)
        mix = _dot((as_ref[rs, :] + am_ref[rs, :]).astype(BF16), wout_ref[...])
        h = _layernorm(alpha * x_ref[rs, :] + mix, g1_ref[...], b1_ref[...])
        up = jnp.maximum(_dot(h.astype(BF16), wup_ref[...]), 0.0)
        ff = _dot((up * up).astype(BF16), wdown_ref[...])
        y_ref[rs, :] = _layernorm(alpha * h + ff, g2_ref[...], b2_ref[...])


def _ffn(x2, a_ssm, a_mla, w, alpha):
    t, d = x2.shape
    tm = min(FFN_ROWS, t)
    assert t % tm == 0 and tm % min(FFN_SUB_ROWS, tm) == 0
    names = ("w_out", "ln1_g", "ln1_b", "w_up", "w_down", "ln2_g", "ln2_b")
    arrs = [w[n] for n in names]
    row_spec = pl.BlockSpec((tm, d), lambda i: (i, 0))
    return pl.pallas_call(
        functools.partial(_ffn_kernel, alpha=alpha),
        grid=(t // tm,),
        in_specs=[row_spec, row_spec, row_spec] + [_const_spec(a.shape) for a in arrs],
        out_specs=row_spec,
        out_shape=jax.ShapeDtypeStruct((t, d), F32),
        compiler_params=_params(1),
        name="ffn",
    )(x2, a_ssm, a_mla, *arrs)


def _rope_tables(pos0, n):
    inv = ROPE_THETA ** (-np.arange(0, QK_ROPE, 2, dtype=np.float64) / QK_ROPE)
    ang = (pos0 + np.arange(n, dtype=np.float64))[:, None] * inv[None, :]
    cos, sin = np.cos(ang), np.sin(ang)
    cos2 = np.concatenate([cos, cos], axis=-1)
    sin2 = np.concatenate([-sin, sin], axis=-1)
    reps = (1, LANES // QK_ROPE)
    return jnp.asarray(np.tile(cos2, reps), F32), jnp.asarray(np.tile(sin2, reps), F32)


def _swap_halves(t):
    half = t.shape[-1] // 2
    return jnp.concatenate([t[..., half:], t[..., :half]], axis=-1)


def _layer_weights(w_in, b_gate, conv_w, conv_b, dt_bias, a_log, d_skip, ssm_norm_g, w_ssm_proj, q_norm_g, w_q_b,
                   kv_norm_g, w_uk, w_uv, w_mla_proj, w_out, ln1_g, ln1_b, w_up, w_down, ln2_g, ln2_b):
    d = w_in.shape[0]
    o_z = 2 * D_MODEL
    o_xbc = o_z + D_INNER
    o_dt = o_xbc + CONV_DIM
    o_qa = o_dt + SSM_HEADS
    o_kva = o_qa + Q_RANK
    bf = lambda t: t.astype(BF16)
    row = lambda t: t.reshape(1, -1)
    lane_pad = lambda t: jnp.pad(t, ((0, 0), (0, LANES - t.shape[-1])))
    w_kv = w_in[:, o_kva:]
    w_kpe = w_kv[:, KV_RANK:]
    zpad = jnp.zeros((d, LANES - QK_ROPE), w_in.dtype)
    qb = w_q_b.reshape(Q_RANK, MLA_HEADS, QK_NOPE + QK_ROPE)
    qb_rope = qb[:, :, QK_NOPE:]
    npair = MLA_HEADS // 2
    k = np.arange(LANES)[:, None]
    c = np.arange(D_INNER)[None, :]
    expand = jnp.asarray((k < 3 * SSM_HEADS) & ((k % SSM_HEADS) == (c // SSM_HEAD_DIM)), BF16)
    w_in_b = bf(w_in)
    col_view = lambda width, start: (w_in_b, _column_spec(d, width, start))
    return dict(
        w_gs=col_view(D_MODEL, 0), b_gs=row(b_gate[:D_MODEL]),
        w_gm=col_view(D_MODEL, D_MODEL), b_gm=row(b_gate[D_MODEL:]),
        w_z=col_view(D_INNER, o_z), w_xbc=col_view(CONV_DIM, o_xbc), w_dt=col_view(LANES, o_dt),
        conv_w=conv_w, conv_b=row(conv_b), dt_bias=lane_pad(row(dt_bias)), a_log=lane_pad(row(a_log)),
        d_skip=row(jnp.repeat(d_skip, SSM_HEAD_DIM)), norm_g=row(ssm_norm_g), expand=expand,
        w_ssm_proj=bf(w_ssm_proj),
        w_qa=bf(w_in[:, o_qa:o_kva]), q_norm_g=row(q_norm_g),
        w_qb=bf(jnp.concatenate([qb[:, :, :QK_NOPE].reshape(Q_RANK, -1), qb_rope.reshape(Q_RANK, -1),
                                 _swap_halves(qb_rope).reshape(Q_RANK, -1)], axis=1)),
        w_ukt=bf(jnp.transpose(w_uk, (1, 2, 0))),
        w_qpair=bf(jnp.transpose(jnp.concatenate(
            [qb[:, :, :QK_NOPE].reshape(Q_RANK, npair, 2 * QK_NOPE), qb_rope.reshape(Q_RANK, npair, 2 * QK_ROPE),
             _swap_halves(qb_rope).reshape(Q_RANK, npair, 2 * QK_ROPE)], axis=2), (1, 0, 2))),
        w_kvpair=bf(jnp.transpose(jnp.concatenate(
            [w_uk.reshape(KV_RANK, npair, 2 * QK_NOPE), w_uv.reshape(KV_RANK, npair, 2 * V_HEAD)], axis=2), (1, 0, 2))),
        w_kva=bf(jnp.concatenate([w_kv[:, :KV_RANK], w_kpe, zpad, _swap_halves(w_kpe), zpad], axis=1)),
        kv_norm_g=row(kv_norm_g), w_uv=bf(jnp.transpose(w_uv, (1, 0, 2))), w_mla_proj=bf(w_mla_proj),
        w_out=bf(w_out), ln1_g=row(ln1_g), ln1_b=row(ln1_b), w_up=bf(w_up), w_down=bf(w_down),
        ln2_g=row(ln2_g), ln2_b=row(ln2_b))


def _prompt_layer(x, w, alpha):
    b, l, d = x.shape
    cos, sin = _rope_tables(0, l)
    a_ssm, ssm, conv = _ssd_prompt(x, w)
    a_mla, ckv, kpe = _mla_prompt(x, cos, sin, w)
    y = _ffn(x.reshape(b * l, d), a_ssm.reshape(b * l, d), a_mla.reshape(b * l, d), w, alpha)
    return y.reshape(b, l, d), ckv, kpe, ssm.reshape(b, SSM_HEADS, SSM_HEAD_DIM, SSM_STATE), conv


def _sample_layer(x, cache_ckv, cache_kpe, state_ssm, state_conv, w, alpha):
    b, l, d = x.shape
    cos, sin = _rope_tables(cache_ckv.shape[1], l)
    a_ssm, ssm, conv = _ssd_sample(x, state_ssm, state_conv, w)
    a_mla, ckv, kpe = _mla_sample(x, cache_ckv, cache_kpe, cos, sin, w)
    y = _ffn(x.reshape(b * l, d), a_ssm, a_mla, w, alpha)
    return y.reshape(b, l, d), ckv, kpe, ssm.reshape(b, SSM_HEADS, SSM_HEAD_DIM, SSM_STATE), conv


def kernel(x_prompt, x_sample, cache_ckv, cache_kpe, state_ssm, state_conv, w_in, b_gate, conv_w, conv_b, dt_bias, a_log, d_skip, ssm_norm_g, w_ssm_proj, q_norm_g, w_q_b, kv_norm_g, w_uk, w_uv, w_mla_proj, w_out, ln1_g, ln1_b, w_up, w_down, ln2_g, ln2_b):
    depth = w_in.shape[0]
    alpha = (2 * depth) ** 0.25
    layer_params = (w_in, b_gate, conv_w, conv_b, dt_bias, a_log, d_skip, ssm_norm_g, w_ssm_proj, q_norm_g, w_q_b,
                    kv_norm_g, w_uk, w_uv, w_mla_proj, w_out, ln1_g, ln1_b, w_up, w_down, ln2_g, ln2_b)
    yp, ys = x_prompt, x_sample
    st_p, st_s = [], []
    for i in range(depth):
        w = _layer_weights(*(p[i] for p in layer_params))
        yp, *sp = _prompt_layer(yp, w, alpha)
        ys, *ss = _sample_layer(ys, cache_ckv[i], cache_kpe[i], state_ssm[i], state_conv[i], w, alpha)
        st_p.append(sp)
        st_s.append(ss)
    stack = lambda sts, k: jnp.stack([s[k] for s in sts])
    return (yp, ys, stack(st_p, 0), stack(st_p, 1), stack(st_p, 2), stack(st_p, 3),
            stack(st_s, 0), stack(st_s, 1), stack(st_s, 2), stack(st_s, 3))
```

```python
import functools
import math

import jax
import jax.numpy as jnp
import numpy as np
from jax import lax
from jax.experimental import pallas as pl
from jax.experimental.pallas import tpu as pltpu

D_MODEL = 1024
D_INNER = 2 * D_MODEL
SSM_HEAD_DIM = 64
SSM_HEADS = D_INNER // SSM_HEAD_DIM
SSM_GROUPS = 8
HEADS_PER_GROUP = SSM_HEADS // SSM_GROUPS
GROUP_DIM = HEADS_PER_GROUP * SSM_HEAD_DIM
SSM_STATE = 128
CONV_WIDTH = 4
CONV_DIM = D_INNER + 2 * SSM_GROUPS * SSM_STATE
MLA_HEADS = 16
QK_NOPE = 128
QK_ROPE = 64
V_HEAD = 128
Q_RANK = 512
KV_RANK = 512
QK_LAT = KV_RANK + QK_ROPE
ROPE_THETA = 10000.0
ATTN_SCALE = (QK_NOPE + QK_ROPE) ** -0.5
RMS_EPS = 1e-6
LN_EPS = 1e-5
CHUNK = 64
CHUNK_SHIFT = 6

LANES = 128
SUBLANES = 8
VMEM_LIMIT = 62 * 1024 * 1024

SSD_Q = 128
SSD_ROWS = 256
SSD_PROJ_TILE = 512
CONV_HALO = (CONV_WIDTH - 1) * SUBLANES
MLA_BLK = 256
DEC_TK = 2048
ROW_TILE = 256
MLA_ROWS = 1024
FFN_ROWS = 512

BF16 = jnp.bfloat16
F32 = jnp.float32
NEG_INF = float("-inf")
LOG2_E = math.log2(math.e)


def _dot(a, b):
    return jnp.dot(a, b, preferred_element_type=F32)


def _dot_nt(a, b):
    return lax.dot_general(a, b, (((1,), (1,)), ((), ())), preferred_element_type=F32)


def _rms(t):
    return t * lax.rsqrt(jnp.mean(t * t, axis=-1, keepdims=True) + RMS_EPS)


def _layernorm(t, g, b):
    mu = jnp.mean(t, axis=-1, keepdims=True)
    d = t - mu
    var = jnp.mean(d * d, axis=-1, keepdims=True)
    return d * lax.rsqrt(var + LN_EPS) * g + b


def _silu(t):
    return t * jax.nn.sigmoid(t)


def _softplus(t):
    return jnp.maximum(t, 0.0) + jnp.log(1.0 + jnp.exp(-jnp.abs(t)))


def _const_spec(shape):
    zeros = (0,) * len(shape)
    return pl.BlockSpec(shape, lambda *_: zeros, pipeline_mode=pl.Buffered(1))


def _column_spec(rows, width, start):
    assert start % width == 0
    return pl.BlockSpec((rows, width), lambda *_: (0, start // width), pipeline_mode=pl.Buffered(1))


def _operands(w, names):
    pairs = [w[n] if isinstance(w[n], tuple) else (w[n], _const_spec(w[n].shape)) for n in names]
    return [a for a, _ in pairs], [s for _, s in pairs]


def _params(n_axes):
    return pltpu.CompilerParams(dimension_semantics=("arbitrary",) * n_axes, vmem_limit_bytes=VMEM_LIMIT)


def _expand_heads(v, e_ref):
    hi = v.astype(BF16).astype(F32)
    r1 = v - hi
    mid = r1.astype(BF16).astype(F32)
    lo = r1 - mid
    lane = lax.broadcasted_iota(jnp.int32, v.shape, 1)
    packed = jnp.where(lane < 32, hi,
                       jnp.where(lane < 64, pltpu.roll(mid, 32, 1),
                                 jnp.where(lane < 96, pltpu.roll(lo, 64, 1), 0.0)))
    return _dot(packed.astype(BF16), e_ref[...])


def _time_of_row(r, q):
    return (r & (SUBLANES - 1)) * (q // SUBLANES) + (r >> 3)


def _ssd_block(xc, z, dt, neg_a2, dskip, norm_g, e_ref, st_ref, bd_ref, side_jobs=None, interleaved=False,
               seq_len=None):
    q = xc.shape[0]
    nseq = q // seq_len if seq_len else 1
    seq = seq_len or q
    row = lax.broadcasted_iota(jnp.int32, (q, q), 0)
    col = lax.broadcasted_iota(jnp.int32, (q, q), 1)
    same_seq = None
    if seq_len:
        shift = seq_len.bit_length() - 1
        same_seq = (row >> shift) == (col >> shift)
    if interleaved:
        row, col = _time_of_row(row, q), _time_of_row(col, q)
    causal = row >= col
    if seq_len:
        causal = causal & same_seq
    da = dt * neg_a2
    acum = jnp.dot(causal.astype(F32), da, precision=lax.Precision.HIGHEST, preferred_element_type=F32)
    acum_t = acum.T
    dt_full = _expand_heads(dt, e_ref)
    acum_full = _expand_heads(acum, e_ref)
    decay_full = jnp.exp2(acum_full)
    alasts = [acum_full[(i + 1) * seq - 1:(i + 1) * seq, :] for i in range(nseq)]
    alast_rows = alasts[0] if nseq == 1 else jnp.concatenate(
        [jnp.broadcast_to(t, (seq, D_INNER)) for t in alasts], axis=0)
    xs = xc[:, :D_INNER]
    xdt = xs * dt_full
    xdt_b = xdt.astype(BF16)
    xw_b = (xdt * jnp.exp2(alast_rows - acum_full)).astype(BF16)
    state_decays = [jnp.exp2(t) for t in alasts]
    token = lax.broadcasted_iota(jnp.int32, (1, q), 1)
    bm_t = xc[:, D_INNER:D_INNER + SSM_GROUPS * SSM_STATE].T.astype(BF16)
    cm_b = xc[:, D_INNER + SSM_GROUPS * SSM_STATE:].astype(BF16)
    ys = []
    for g in range(SSM_GROUPS):
        if side_jobs:
            side_jobs.pop(0)()
        gsl = slice(g * GROUP_DIM, (g + 1) * GROUP_DIM)
        nsl = slice(g * SSM_STATE, (g + 1) * SSM_STATE)
        cg = cm_b[:, nsl]
        bg_t = bm_t[nsl, :]
        cb = _dot(cg, bg_t)
        ms = []
        for r in range(HEADS_PER_GROUP):
            h = g * HEADS_PER_GROUP + r
            hsl = slice(r * SSM_HEAD_DIM, (r + 1) * SSM_HEAD_DIM)
            bd_ref[g, r * q:(r + 1) * q, hsl] = xdt_b[:, g * GROUP_DIM + r * SSM_HEAD_DIM:g * GROUP_DIM + (r + 1) * SSM_HEAD_DIM]
            seg = acum[:, h:h + 1] - acum_t[h:h + 1, :]
            ms.append((cb * jnp.exp2(jnp.where(causal, seg, NEG_INF))).astype(BF16))
        yg = _dot(jnp.concatenate(ms, axis=1), bd_ref[g])
        if seq_len is None:
            st = st_ref[g]
            yg = yg + _dot(cg, st.astype(BF16)) * decay_full[:, gsl]
            st_ref[g] = st * state_decays[0][:, gsl] + _dot(bg_t, xw_b[:, gsl])
        else:
            sts = [st_ref[i, g] for i in range(nseq)]
            wide = _dot(cg, jnp.concatenate([t.astype(BF16) for t in sts], axis=1))
            inter = jnp.concatenate(
                [wide[i * seq:(i + 1) * seq, i * GROUP_DIM:(i + 1) * GROUP_DIM] for i in range(nseq)], axis=0)
            yg = yg + inter * decay_full[:, gsl]
            for i in range(nseq):
                own = (token >= i * seq) & (token < (i + 1) * seq)
                st_ref[i, g] = (sts[i] * state_decays[i][:, gsl]
                                + _dot(jnp.where(own, bg_t, jnp.zeros_like(bg_t)), xw_b[:, gsl]))
        ys.append(yg)
    y = jnp.concatenate(ys, axis=1) + xs * dskip
    yz = y * _silu(z)
    outs = []
    for g in range(SSM_GROUPS):
        outs.append(_rms(yz[:, g * GROUP_DIM:(g + 1) * GROUP_DIM]))
    return jnp.concatenate(outs, axis=1) * norm_g


def _conv_silu(xp_ref, rows, convw, convb):
    acc = convb
    for k in range(CONV_WIDTH):
        acc = acc + xp_ref[SUBLANES - (CONV_WIDTH - 1) + k:SUBLANES - (CONV_WIDTH - 1) + k + rows, :] * convw[k:k + 1, :]
    return _silu(acc)


def _interleave_rows(xb, q, inverse=False):
    row = lax.broadcasted_iota(jnp.int32, (q, q), 0)
    col = lax.broadcasted_iota(jnp.int32, (q, q), 1)
    hit = (row == _time_of_row(col, q)) if inverse else (col == _time_of_row(row, q))
    perm = jnp.where(hit, 1.0, 0.0).astype(BF16)
    return jnp.concatenate([_dot(perm, xb[b0:b0 + q]).astype(BF16) for b0 in range(0, xb.shape[0], q)], axis=0)


def _conv_silu_interleaved(xp_ref, blk, prev, convw, convb):
    q = xp_ref.shape[1] - CONV_HALO
    first = lax.broadcasted_iota(jnp.int32, (SUBLANES, CONV_DIM), 0) == 0
    for j in range(CONV_WIDTH - 1):
        js = slice(j * SUBLANES, (j + 1) * SUBLANES)
        cur_j = xp_ref[blk, q + j * SUBLANES:q + (j + 1) * SUBLANES, :]
        xp_ref[blk, js, :] = jnp.where(first, pltpu.roll(prev[js], 1, 0), pltpu.roll(cur_j, 1, 0))
    acc = convb
    for k in range(CONV_WIDTH):
        acc = acc + xp_ref[blk, k * SUBLANES:k * SUBLANES + q, :] * convw[k:k + 1, :]
    return _silu(acc)


def _ssd_project_jobs(get_xb, get_xb_tokens, wz_ref, wxbc_ref, wdt_ref, wg_ref, bg_ref, dtb_ref,
                      z_ref, xp_ref, dt_ref, gate_ref):
    tile = SSD_PROJ_TILE
    q = xp_ref.shape[1] - CONV_HALO

    def xbc_job(c0):
        def job():
            res = _dot(get_xb(), wxbc_ref[:, c0:c0 + tile])
            for b in range(xp_ref.shape[0]):
                xp_ref[b, CONV_HALO:, c0:c0 + tile] = res[b * q:(b + 1) * q]
        return job

    def z_job(c0):
        def job():
            z_ref[:, c0:c0 + tile] = _dot(get_xb(), wz_ref[:, c0:c0 + tile])
        return job

    def gate_job(c0):
        def job():
            gate_ref[:, c0:c0 + tile] = jax.nn.sigmoid(_dot(get_xb_tokens(), wg_ref[:, c0:c0 + tile])
                                                       + bg_ref[:, c0:c0 + tile])
        return job

    def dt_job():
        dt_ref[...] = _softplus(_dot(get_xb(), wdt_ref[...]) + dtb_ref[...])

    return ([xbc_job(c0) for c0 in range(0, CONV_DIM, tile)] + [z_job(c0) for c0 in range(0, D_INNER, tile)]
            + [gate_job(c0) for c0 in range(0, D_MODEL, tile)] + [dt_job])


def _ssd_prompt_kernel(x0_ref, xn_ref, wz_ref, wxbc_ref, wdt_ref, wg_ref, bg_ref, convw_ref, convb_ref, dtb_ref,
                       alog_ref, dskip_ref, ng_ref, e_ref, wproj_ref,
                       a_out_ref, ssm_out_ref, conv_out_ref,
                       st_ref, bd_ref, hist_ref, xbn_ref, z_a, xp_a, dt_a, gate_a, z_b, xp_b, dt_b, gate_b, *,
                       steps_per_row):
    g = pl.program_id(0)
    c = g % steps_per_row
    nblk = xp_a.shape[0]
    proj_w = (wz_ref, wxbc_ref, wdt_ref, wg_ref, bg_ref, dtb_ref)

    @pl.when(g == 0)
    def _():
        bd_ref[...] = jnp.zeros_like(bd_ref)
        xb0 = x0_ref[...].astype(BF16)
        xbn_ref[...] = _interleave_rows(xb0, SSD_Q)
        for job in _ssd_project_jobs(lambda: xbn_ref[...], lambda: xb0, *proj_w, z_a, xp_a, dt_a, gate_a):
            job()

    @pl.when(c == 0)
    def _():
        st_ref[...] = jnp.zeros_like(st_ref)
        hist_ref[...] = jnp.zeros_like(hist_ref)

    def step(cur, nxt):
        z_ref, xp_ref, dt_ref, gate_ref = cur
        xbn_ref[...] = _interleave_rows(xn_ref[...].astype(BF16), SSD_Q)
        jobs = _ssd_project_jobs(lambda: xbn_ref[...], lambda: xn_ref[...].astype(BF16), *proj_w, *nxt)
        neg_a2 = -jnp.exp(alog_ref[...]) * LOG2_E
        yn = []
        for s in range(nblk):
            sl = slice(s * SSD_Q, (s + 1) * SSD_Q)
            prev = hist_ref[...] if s == 0 else xp_ref[s - 1, SSD_Q:, :]
            xc = _conv_silu_interleaved(xp_ref, s, prev, convw_ref[...], convb_ref[...])
            yn.append(_ssd_block(xc, z_ref[sl, :], dt_ref[sl, :], neg_a2, dskip_ref[...], ng_ref[...], e_ref,
                                 st_ref, bd_ref, jobs, interleaved=True))
        hist_ref[...] = xp_ref[nblk - 1, SSD_Q:, :]
        for job in jobs:
            job()
        yn = _interleave_rows(jnp.concatenate(yn, axis=0).astype(BF16), SSD_Q, inverse=True)
        a_out_ref[...] = gate_ref[...] * _dot(yn, wproj_ref[...])

    slot_a, slot_b = (z_a, xp_a, dt_a, gate_a), (z_b, xp_b, dt_b, gate_b)

    @pl.when(g % 2 == 0)
    def _():
        step(slot_a, slot_b)

    @pl.when(g % 2 == 1)
    def _():
        step(slot_b, slot_a)

    @pl.when(c == steps_per_row - 1)
    def _():
        for j in range(CONV_WIDTH - 1):
            conv_out_ref[j:j + 1, :] = hist_ref[(j + 1) * SUBLANES - 1:(j + 1) * SUBLANES, :]
        for k in range(SSM_GROUPS):
            ssm_out_ref[k * GROUP_DIM:(k + 1) * GROUP_DIM, :] = st_ref[k].T


def _ssd_sample_kernel(x_ref, state_ref, cstate_ref, wz_ref, wxbc_ref, wdt_ref, wg_ref, bg_ref, convw_ref, convb_ref,
                       dtb_ref, alog_ref, dskip_ref, ng_ref, e_ref, wproj_ref,
                       a_out_ref, ssm_out_ref, conv_out_ref,
                       st_ref, bd_ref, xp_ref, xc_scr):
    nseq, seq = cstate_ref.shape[0], x_ref.shape[0] // cstate_ref.shape[0]
    xb = x_ref[...].astype(BF16)
    z = _dot(xb, wz_ref[...])
    xbc = _dot(xb, wxbc_ref[...])
    dt = _softplus(_dot(xb, wdt_ref[...]) + dtb_ref[...])
    gate = jax.nn.sigmoid(_dot(xb, wg_ref[...]) + bg_ref[...])
    bd_ref[...] = jnp.zeros_like(bd_ref)
    lo = SUBLANES - (CONV_WIDTH - 1)
    for s in range(nseq):
        xp_ref[s, lo:SUBLANES, :] = cstate_ref[s]
        xp_ref[s, SUBLANES:, :] = xbc[s * seq:(s + 1) * seq]
        xc_scr[s * seq:(s + 1) * seq, :] = _conv_silu(xp_ref.at[s], seq, convw_ref[...], convb_ref[...])
        conv_out_ref[s] = xp_ref[s, seq + lo:, :]
        for g in range(SSM_GROUPS):
            st_ref[s, g] = state_ref[s, g * GROUP_DIM:(g + 1) * GROUP_DIM, :].T
    yn = _ssd_block(xc_scr[...], z, dt, -jnp.exp(alog_ref[...]) * LOG2_E, dskip_ref[...], ng_ref[...], e_ref, st_ref,
                    bd_ref, seq_len=seq)
    for s in range(nseq):
        for g in range(SSM_GROUPS):
            ssm_out_ref[s, g * GROUP_DIM:(g + 1) * GROUP_DIM, :] = st_ref[s, g].T
    a_out_ref[...] = gate * _dot(yn.astype(BF16), wproj_ref[...])


def _ssd_weight_specs(w):
    names = ("w_z", "w_xbc", "w_dt", "w_gs", "b_gs", "conv_w", "conv_b", "dt_bias", "a_log", "d_skip", "norm_g",
             "expand", "w_ssm_proj")
    return _operands(w, names)


def _ssd_prompt(x, w):
    b, l, d = x.shape
    rows = min(SSD_ROWS, l)
    assert l % rows == 0 and rows % SSD_Q == 0
    arrs, specs = _ssd_weight_specs(w)
    nc = l // rows
    last = b * nc - 1
    slot = [pltpu.VMEM((rows, D_INNER), F32), pltpu.VMEM((rows // SSD_Q, CONV_HALO + SSD_Q, CONV_DIM), F32),
            pltpu.VMEM((rows, LANES), F32), pltpu.VMEM((rows, D_MODEL), F32)]

    def next_block(g):
        n = jnp.minimum(g + 1, last)
        return (n // nc, n % nc, 0)

    return pl.pallas_call(
        functools.partial(_ssd_prompt_kernel, steps_per_row=nc),
        grid=(b * nc,),
        in_specs=[pl.BlockSpec((None, rows, d), lambda g: (0, 0, 0)),
                  pl.BlockSpec((None, rows, d), next_block)] + specs,
        out_specs=[pl.BlockSpec((None, rows, D_MODEL), lambda g: (g // nc, g % nc, 0)),
                   pl.BlockSpec((None, D_INNER, SSM_STATE), lambda g: (g // nc, 0, 0)),
                   pl.BlockSpec((None, CONV_WIDTH - 1, CONV_DIM), lambda g: (g // nc, 0, 0))],
        out_shape=[jax.ShapeDtypeStruct((b, l, D_MODEL), F32),
                   jax.ShapeDtypeStruct((b, D_INNER, SSM_STATE), F32),
                   jax.ShapeDtypeStruct((b, CONV_WIDTH - 1, CONV_DIM), F32)],
        scratch_shapes=[pltpu.VMEM((SSM_GROUPS, SSM_STATE, GROUP_DIM), F32),
                        pltpu.VMEM((SSM_GROUPS, HEADS_PER_GROUP * SSD_Q, GROUP_DIM), BF16),
                        pltpu.VMEM((CONV_HALO, CONV_DIM), F32),
                        pltpu.VMEM((rows, d), BF16)] + slot + slot,
        compiler_params=_params(1),
        name="ssd_prompt",
    )(x, x, *arrs)


def _ssd_sample(x, state, cstate, w):
    b, l, d = x.shape
    nseq = SSD_Q // l
    assert SSD_Q % l == 0 and (l & (l - 1)) == 0 and l % SUBLANES == 0 and b % nseq == 0
    rows = nseq * l
    arrs, specs = _ssd_weight_specs(w)
    a, ssm, conv = pl.pallas_call(
        _ssd_sample_kernel,
        grid=(b // nseq,),
        in_specs=[pl.BlockSpec((rows, d), lambda i: (i, 0)),
                  pl.BlockSpec((nseq, D_INNER, SSM_STATE), lambda i: (i, 0, 0)),
                  pl.BlockSpec((nseq, CONV_WIDTH - 1, CONV_DIM), lambda i: (i, 0, 0))] + specs,
        out_specs=[pl.BlockSpec((rows, D_MODEL), lambda i: (i, 0)),
                   pl.BlockSpec((nseq, D_INNER, SSM_STATE), lambda i: (i, 0, 0)),
                   pl.BlockSpec((nseq, CONV_WIDTH - 1, CONV_DIM), lambda i: (i, 0, 0))],
        out_shape=[jax.ShapeDtypeStruct((b * l, D_MODEL), F32),
                   jax.ShapeDtypeStruct((b, D_INNER, SSM_STATE), F32),
                   jax.ShapeDtypeStruct((b, CONV_WIDTH - 1, CONV_DIM), F32)],
        scratch_shapes=[pltpu.VMEM((nseq, SSM_GROUPS, SSM_STATE, GROUP_DIM), F32),
                        pltpu.VMEM((SSM_GROUPS, HEADS_PER_GROUP * SSD_Q, GROUP_DIM), BF16),
                        pltpu.VMEM((nseq, l + SUBLANES, CONV_DIM), F32),
                        pltpu.VMEM((rows, CONV_DIM), F32)],
        compiler_params=_params(1),
        name="ssd_sample",
    )(x.reshape(b * l, d), state.reshape(b, D_INNER, SSM_STATE), cstate, *arrs)
    return a, ssm, conv


def _mla_keys(xb, cos, sin, wkva_ref, kvg_ref):
    kva = _dot(xb, wkva_ref[...])
    ckv = _rms(kva[:, :KV_RANK]) * kvg_ref[...]
    kpe = (kva[:, KV_RANK:KV_RANK + QK_ROPE] * cos[:, :QK_ROPE]
           + kva[:, KV_RANK + LANES:KV_RANK + LANES + QK_ROPE] * sin[:, :QK_ROPE])
    return ckv, kpe


def _mla_queries(xb, cos, sin, wqa_ref, qg_ref, wqb_ref, wukt_ref, store):
    qn = (_rms(_dot(xb, wqa_ref[...])) * qg_ref[...]).astype(BF16)
    q = _dot(qn, wqb_ref[...])
    nope_w = MLA_HEADS * QK_NOPE
    rope_w = MLA_HEADS * QK_ROPE
    for p in range(MLA_HEADS // 2):
        sl = slice(nope_w + p * LANES, nope_w + (p + 1) * LANES)
        sl_sw = slice(nope_w + rope_w + p * LANES, nope_w + rope_w + (p + 1) * LANES)
        pair = (q[:, sl] * cos + q[:, sl_sw] * sin) * ATTN_SCALE
        pair_hi = pltpu.roll(pair, QK_ROPE, 1)
        for h, pe in ((2 * p, pair), (2 * p + 1, pair_hi)):
            q_lat = _dot(q[:, h * QK_NOPE:(h + 1) * QK_NOPE].astype(BF16), wukt_ref[h]) * ATTN_SCALE
            store(h, q_lat.astype(BF16), pe[:, :QK_ROPE].astype(BF16))


def _lane_tile(t, width):
    return jnp.concatenate([t] * (width // LANES), axis=1) if width > LANES else t


def _softmax_update(s, v_b, m_scr, l_scr, acc_scr):
    m_prev = m_scr[...]
    m_new = jnp.maximum(m_prev, jnp.max(s, axis=-1, keepdims=True))
    alpha = jnp.exp(m_prev - m_new)
    p = jnp.exp(s - _lane_tile(m_new, s.shape[1]))
    p_lanes = p[:, 0:LANES]
    for c in range(1, s.shape[1] // LANES):
        p_lanes = p_lanes + p[:, c * LANES:(c + 1) * LANES]
    l_scr[...] = alpha * l_scr[...] + p_lanes
    acc_scr[...] = _lane_tile(alpha, acc_scr.shape[1]) * acc_scr[...] + _dot(p.astype(BF16), v_b)
    m_scr[...] = m_new


def _softmax_finish(l_scr, acc_scr):
    return acc_scr[...] / jnp.sum(l_scr[...], axis=-1, keepdims=True)


def _mla_out(o_of_head, xb, wuv_ref, wmla_ref, wg_ref, bg_ref, o_scr):
    for h in range(MLA_HEADS):
        o_scr[:, h * V_HEAD:(h + 1) * V_HEAD] = _dot(o_of_head(h), wuv_ref[h]).astype(BF16)
    gate = jax.nn.sigmoid(_dot(xb, wg_ref[...]) + bg_ref[...])
    return gate * _dot(o_scr[...], wmla_ref[...])


def _mla_pre_kernel(x_ref, cos_ref, sin_ref, wqa_ref, qg_ref, wkva_ref, kvg_ref, qn_out_ref, ckv_out_ref, kpe_out_ref):
    xb = x_ref[...].astype(BF16)
    ckv, kpe = _mla_keys(xb, cos_ref[...], sin_ref[...], wkva_ref, kvg_ref)
    ckv_out_ref[...] = ckv
    kpe_out_ref[...] = kpe
    qn_out_ref[...] = (_rms(_dot(xb, wqa_ref[...])) * qg_ref[...]).astype(BF16)


def _mla_prompt_attn_kernel(qn_ref, ckv_ref, kpe_ref, cos_ref, sin_ref, wqp_ref, wkv_ref, o_ref,
                            qa_scr, qb_scr, ka_scr, kb_scr, vbd_scr, m_scr, l_scr, acc_scr):
    l = qn_ref.shape[0]
    g = min(MLA_BLK, l)
    nb = l // g
    half = QK_ROPE
    zpad = jnp.zeros((l, LANES - half), BF16)
    kv = _dot(ckv_ref[...].astype(BF16), wkv_ref[...])
    kpe_b = kpe_ref[...].astype(BF16)
    for scr, c0 in ((ka_scr, 0), (kb_scr, QK_NOPE)):
        scr[:, 0:QK_NOPE] = kv[:, c0:c0 + QK_NOPE].astype(BF16)
        scr[:, QK_NOPE:QK_NOPE + half] = kpe_b
        scr[:, QK_NOPE + half:] = zpad
    zv = jnp.zeros((g, V_HEAD), BF16)
    for j in range(nb):
        rs = slice(j * g, (j + 1) * g)
        vbd_scr[j, 0:g, 0:V_HEAD] = kv[rs, 2 * QK_NOPE:2 * QK_NOPE + V_HEAD].astype(BF16)
        vbd_scr[j, 0:g, V_HEAD:] = zv
        vbd_scr[j, g:, 0:V_HEAD] = zv
        vbd_scr[j, g:, V_HEAD:] = kv[rs, 2 * QK_NOPE + V_HEAD:].astype(BF16)
    q = _dot(qn_ref[...], wqp_ref[...])
    c = ATTN_SCALE * LOG2_E
    rope = q[:, 2 * QK_NOPE:]
    upper = (lax.broadcasted_iota(jnp.int32, rope.shape, 1) & (half // 2)) != 0
    swapped = jnp.where(upper, pltpu.roll(rope, half // 2, 1), pltpu.roll(rope, LANES - half // 2, 1))
    pair = (rope * cos_ref[...] + swapped * sin_ref[...]) * c
    pair_hi = pltpu.roll(pair, half, 1)
    for scr, c0, pe in ((qa_scr, 0, pair), (qb_scr, QK_NOPE, pair_hi)):
        scr[:, 0:QK_NOPE] = (q[:, c0:c0 + QK_NOPE] * c).astype(BF16)
        scr[:, QK_NOPE:QK_NOPE + half] = pe[:, :half].astype(BF16)
        scr[:, QK_NOPE + half:] = zpad
    row = lax.broadcasted_iota(jnp.int32, (g, g), 0)
    col = lax.broadcasted_iota(jnp.int32, (g, g), 1)
    diag_visible = (row >> CHUNK_SHIFT) >= (col >> CHUNK_SHIFT)
    for j in range(nb):
        r0 = j * g
        ps, alphas = [], []
        for hi, (q_scr, k_scr) in enumerate(((qa_scr, ka_scr), (qb_scr, kb_scr))):
            s = _dot_nt(q_scr[r0:, :], k_scr[r0:r0 + g, :])
            top = jnp.where(diag_visible, s[:g], NEG_INF)
            s = top if j == nb - 1 else jnp.concatenate([top, s[g:]], axis=0)
            m_cur = jnp.max(s, axis=-1, keepdims=True)
            if j == 0:
                m_new = jnp.broadcast_to(m_cur, (l, LANES))
            else:
                m_prev = m_scr[hi, r0:, :]
                m_new = jnp.maximum(m_prev, m_cur)
                alphas.append(jnp.exp2(m_prev - m_new))
            p = jnp.exp2(s - _lane_tile(m_new, g))
            p_lanes = p[:, 0:LANES]
            for cc in range(1, g // LANES):
                p_lanes = p_lanes + p[:, cc * LANES:(cc + 1) * LANES]
            l_scr[hi, r0:, :] = p_lanes if j == 0 else alphas[-1] * l_scr[hi, r0:, :] + p_lanes
            m_scr[hi, r0:, :] = m_new
            ps.append(p.astype(BF16))
        pv = _dot(jnp.concatenate(ps, axis=1), vbd_scr[j])
        if j == 0:
            acc_scr[...] = pv
        else:
            acc_scr[r0:, :] = jnp.concatenate(alphas, axis=1) * acc_scr[r0:, :] + pv
    inv = [1.0 / jnp.sum(l_scr[hi], axis=-1, keepdims=True) for hi in range(2)]
    scale = jnp.concatenate([jnp.broadcast_to(t, (l, V_HEAD)) for t in inv], axis=1)
    o_ref[...] = (acc_scr[...] * scale).astype(BF16)


def _mla_post_kernel(o_ref, x_ref, wmla_ref, wg_ref, bg_ref, a_out_ref):
    gate = jax.nn.sigmoid(_dot(x_ref[...].astype(BF16), wg_ref[...]) + bg_ref[...])
    a_out_ref[...] = gate * _dot(o_ref[...], wmla_ref[...])


def _mla_prompt(x, cos, sin, w):
    b, l, d = x.shape
    g = min(MLA_BLK, l)
    tm = min(MLA_ROWS, l)
    npair = MLA_HEADS // 2
    assert l % g == 0 and g % CHUNK == 0 and g % LANES == 0 and l % tm == 0
    nt = l // tm
    pre = [w[n] for n in ("w_qa", "q_norm_g", "w_kva", "kv_norm_g")]
    qn, ckv, kpe = pl.pallas_call(
        _mla_pre_kernel,
        grid=(b, nt),
        in_specs=[pl.BlockSpec((None, tm, d), lambda i, c: (i, c, 0)),
                  pl.BlockSpec((tm, LANES), lambda i, c: (c, 0)),
                  pl.BlockSpec((tm, LANES), lambda i, c: (c, 0))] + [_const_spec(a.shape) for a in pre],
        out_specs=[pl.BlockSpec((None, tm, Q_RANK), lambda i, c: (i, c, 0)),
                   pl.BlockSpec((None, tm, KV_RANK), lambda i, c: (i, c, 0)),
                   pl.BlockSpec((None, tm, QK_ROPE), lambda i, c: (i, c, 0))],
        out_shape=[jax.ShapeDtypeStruct((b, l, Q_RANK), BF16),
                   jax.ShapeDtypeStruct((b, l, KV_RANK), F32),
                   jax.ShapeDtypeStruct((b, l, QK_ROPE), F32)],
        compiler_params=_params(2),
        name="mla_pre",
    )(x, cos, sin, *pre)
    o = pl.pallas_call(
        _mla_prompt_attn_kernel,
        grid=(b, npair),
        in_specs=[pl.BlockSpec((None, l, Q_RANK), lambda i, p: (i, 0, 0)),
                  pl.BlockSpec((None, l, KV_RANK), lambda i, p: (i, 0, 0)),
                  pl.BlockSpec((None, l, QK_ROPE), lambda i, p: (i, 0, 0)),
                  _const_spec(cos.shape), _const_spec(sin.shape),
                  pl.BlockSpec((None, Q_RANK, 2 * QK_NOPE + 2 * QK_ROPE), lambda i, p: (p, 0, 0)),
                  pl.BlockSpec((None, KV_RANK, 2 * QK_NOPE + 2 * V_HEAD), lambda i, p: (p, 0, 0))],
        out_specs=pl.BlockSpec((None, l, 2 * V_HEAD), lambda i, p: (i, 0, p)),
        out_shape=jax.ShapeDtypeStruct((b, l, MLA_HEADS * V_HEAD), BF16),
        scratch_shapes=[pltpu.VMEM((l, 2 * LANES), BF16)] * 4
        + [pltpu.VMEM((l // g, 2 * g, 2 * V_HEAD), BF16),
           pltpu.VMEM((2, l, LANES), F32),
           pltpu.VMEM((2, l, LANES), F32),
           pltpu.VMEM((l, 2 * V_HEAD), F32)],
        compiler_params=_params(2),
        name="mla_prompt_attn",
    )(qn, ckv, kpe, cos, sin, w["w_qpair"], w["w_kvpair"])
    post, post_specs = _operands(w, ("w_mla_proj", "w_gm", "b_gm"))
    a = pl.pallas_call(
        _mla_post_kernel,
        grid=(b, nt),
        in_specs=[pl.BlockSpec((None, tm, MLA_HEADS * V_HEAD), lambda i, c: (i, c, 0)),
                  pl.BlockSpec((None, tm, d), lambda i, c: (i, c, 0))] + post_specs,
        out_specs=pl.BlockSpec((None, tm, D_MODEL), lambda i, c: (i, c, 0)),
        out_shape=jax.ShapeDtypeStruct((b, l, D_MODEL), F32),
        compiler_params=_params(2),
        name="mla_post",
    )(o, x, *post)
    return a, ckv, kpe


def _mla_dec_pre_kernel(x_ref, cos_ref, sin_ref, wqa_ref, qg_ref, wqb_ref, wukt_ref, wkva_ref, kvg_ref,
                        q_out_ref, ckv_out_ref, kpe_out_ref):
    xb = x_ref[...].astype(BF16)
    cos, sin = cos_ref[...], sin_ref[...]
    ckv, kpe = _mla_keys(xb, cos, sin, wkva_ref, kvg_ref)
    ckv_out_ref[...] = ckv
    kpe_out_ref[...] = kpe

    def store(h, q_lat, q_pe):
        q_out_ref[h, :, 0:KV_RANK] = q_lat
        q_out_ref[h, :, KV_RANK:QK_LAT] = q_pe

    _mla_queries(xb, cos, sin, wqa_ref, qg_ref, wqb_ref, wukt_ref, store)


def _mla_dec_attn_kernel(q_ref, cckv_ref, ckpe_t_ref, nckv_ref, nkpe_ref, o_ref,
                         m_scr, l_scr, acc_scr, nk_scr, *, past_len):
    j = pl.program_id(1)
    seq = q_ref.shape[1]
    rows = MLA_HEADS * seq

    @pl.when(j == 0)
    def _():
        m_scr[...] = jnp.full_like(m_scr, NEG_INF)
        l_scr[...] = jnp.zeros_like(l_scr)
        acc_scr[...] = jnp.zeros_like(acc_scr)

    q = q_ref[...].reshape(rows, QK_LAT)
    ck = cckv_ref[...].astype(BF16)
    kp_t = ckpe_t_ref[...].astype(BF16)
    s = _dot_nt(q[:, :KV_RANK], ck) + _dot(q[:, KV_RANK:], kp_t)
    _softmax_update(s, ck, m_scr, l_scr, acc_scr)

    @pl.when(j == pl.num_programs(1) - 1)
    def _():
        nk_scr[...] = jnp.zeros_like(nk_scr)
        nk_scr[0:seq, 0:KV_RANK] = nckv_ref[...].astype(BF16)
        nk_scr[0:seq, KV_RANK:QK_LAT] = nkpe_ref[...].astype(BF16)
        nk = nk_scr[...]
        s2 = _dot_nt(q, nk)
        row = lax.broadcasted_iota(jnp.int32, (rows, 1), 0)
        q_chunk = (past_len + (row & (seq - 1))) >> CHUNK_SHIFT
        k_lane = lax.broadcasted_iota(jnp.int32, (1, LANES), 1)
        visible = (k_lane < seq) & (((past_len + k_lane) >> CHUNK_SHIFT) <= q_chunk)
        _softmax_update(jnp.where(visible, s2, NEG_INF), nk[:, :KV_RANK], m_scr, l_scr, acc_scr)
        o_ref[...] = _softmax_finish(l_scr, acc_scr).astype(BF16).reshape(MLA_HEADS, seq, KV_RANK)


def _mla_dec_post_kernel(o_ref, x_ref, wuv_ref, wmla_ref, wg_ref, bg_ref, a_out_ref, o_scr):
    xb = x_ref[...].astype(BF16)
    a_out_ref[...] = _mla_out(lambda h: o_ref[h], xb, wuv_ref, wmla_ref, wg_ref, bg_ref, o_scr)


def _mla_sample(x, cache_ckv, cache_kpe, cos, sin, w):
    b, l, d = x.shape
    t = b * l
    past_len = cache_ckv.shape[1]
    tm = min(ROW_TILE, t)
    tk = min(DEC_TK, past_len)
    assert t % tm == 0 and past_len % tk == 0 and (l & (l - 1)) == 0 and l <= LANES and l % 16 == 0
    x2 = x.reshape(t, d)
    pre_names = ("w_qa", "q_norm_g", "w_qb", "w_ukt", "w_kva", "kv_norm_g")
    pre = [w[n] for n in pre_names]
    q, ckv, kpe = pl.pallas_call(
        _mla_dec_pre_kernel,
        grid=(t // tm,),
        in_specs=[pl.BlockSpec((tm, d), lambda i: (i, 0)),
                  pl.BlockSpec((tm, LANES), lambda i: (i, 0)),
                  pl.BlockSpec((tm, LANES), lambda i: (i, 0))] + [_const_spec(a.shape) for a in pre],
        out_specs=[pl.BlockSpec((MLA_HEADS, tm, QK_LAT), lambda i: (0, i, 0)),
                   pl.BlockSpec((tm, KV_RANK), lambda i: (i, 0)),
                   pl.BlockSpec((tm, QK_ROPE), lambda i: (i, 0))],
        out_shape=[jax.ShapeDtypeStruct((MLA_HEADS, t, QK_LAT), BF16),
                   jax.ShapeDtypeStruct((t, KV_RANK), F32),
                   jax.ShapeDtypeStruct((t, QK_ROPE), F32)],
        compiler_params=_params(1),
        name="mla_dec_pre",
    )(x2, jnp.tile(cos, (b, 1)), jnp.tile(sin, (b, 1)), *pre)
    rows = MLA_HEADS * l
    o = pl.pallas_call(
        functools.partial(_mla_dec_attn_kernel, past_len=past_len),
        grid=(b, past_len // tk),
        in_specs=[pl.BlockSpec((MLA_HEADS, l, QK_LAT), lambda i, j: (0, i, 0)),
                  pl.BlockSpec((None, tk, KV_RANK), lambda i, j: (i, j, 0)),
                  pl.BlockSpec((None, QK_ROPE, tk), lambda i, j: (i, 0, j)),
                  pl.BlockSpec((l, KV_RANK), lambda i, j: (i, 0)),
                  pl.BlockSpec((l, QK_ROPE), lambda i, j: (i, 0))],
        out_specs=pl.BlockSpec((MLA_HEADS, l, KV_RANK), lambda i, j: (0, i, 0)),
        out_shape=jax.ShapeDtypeStruct((MLA_HEADS, t, KV_RANK), BF16),
        scratch_shapes=[pltpu.VMEM((rows, LANES), F32),
                        pltpu.VMEM((rows, LANES), F32),
                        pltpu.VMEM((rows, KV_RANK), F32),
                        pltpu.VMEM((LANES, QK_LAT), BF16)],
        compiler_params=_params(2),
        name="mla_dec_attn",
    )(q, cache_ckv, jnp.swapaxes(cache_kpe, 1, 2), ckv, kpe)
    post_names = ("w_uv", "w_mla_proj", "w_gm", "b_gm")
    post, post_specs = _operands(w, post_names)
    a = pl.pallas_call(
        _mla_dec_post_kernel,
        grid=(t // tm,),
        in_specs=[pl.BlockSpec((MLA_HEADS, tm, KV_RANK), lambda i: (0, i, 0)),
                  pl.BlockSpec((tm, d), lambda i: (i, 0))] + post_specs,
        out_specs=pl.BlockSpec((tm, D_MODEL), lambda i: (i, 0)),
        out_shape=jax.ShapeDtypeStruct((t, D_MODEL), F32),
        scratch_shapes=[pltpu.VMEM((tm, MLA_HEADS * V_HEAD), BF16)],
        compiler_params=_params(1),
        name="mla_dec_post",
    )(o, x2, *post)
    return a, ckv.reshape(b, l, KV_RANK), kpe.reshape(b, l, QK_ROPE)


def _ffn_kernel(x_ref, as_ref, am_ref, wout_ref, g1_ref, b1_ref, wup_ref, wdown_ref, g2_ref, b2_ref, y_ref, *, alpha):
    mix = _dot((as_ref[...] + am_ref[...]).astype(BF16), wout_ref[...])
    h = _layernorm(alpha * x_ref[...] + mix, g1_ref[...], b1_ref[...])
    up = jnp.maximum(_dot(h.astype(BF16), wup_ref[...]), 0.0)
    ff = _dot((up * up).astype(BF16), wdown_ref[...])
    y_ref[...] = _layernorm(alpha * h + ff, g2_ref[...], b2_ref[...])


def _ffn(x2, a_ssm, a_mla, w, alpha):
    t, d = x2.shape
    tm = min(FFN_ROWS, t)
    assert t % tm == 0
    names = ("w_out", "ln1_g", "ln1_b", "w_up", "w_down", "ln2_g", "ln2_b")
    arrs = [w[n] for n in names]
    row_spec = pl.BlockSpec((tm, d), lambda i: (i, 0))
    return pl.pallas_call(
        functools.partial(_ffn_kernel, alpha=alpha),
        grid=(t // tm,),
        in_specs=[row_spec, row_spec, row_spec] + [_const_spec(a.shape) for a in arrs],
        out_specs=row_spec,
        out_shape=jax.ShapeDtypeStruct((t, d), F32),
        compiler_params=_params(1),
        name="ffn",
    )(x2, a_ssm, a_mla, *arrs)


def _rope_tables(pos0, n):
    inv = ROPE_THETA ** (-np.arange(0, QK_ROPE, 2, dtype=np.float64) / QK_ROPE)
    ang = (pos0 + np.arange(n, dtype=np.float64))[:, None] * inv[None, :]
    cos, sin = np.cos(ang), np.sin(ang)
    cos2 = np.concatenate([cos, cos], axis=-1)
    sin2 = np.concatenate([-sin, sin], axis=-1)
    reps = (1, LANES // QK_ROPE)
    return jnp.asarray(np.tile(cos2, reps), F32), jnp.asarray(np.tile(sin2, reps), F32)


def _swap_halves(t):
    half = t.shape[-1] // 2
    return jnp.concatenate([t[..., half:], t[..., :half]], axis=-1)


def _layer_weights(w_in, b_gate, conv_w, conv_b, dt_bias, a_log, d_skip, ssm_norm_g, w_ssm_proj, q_norm_g, w_q_b,
                   kv_norm_g, w_uk, w_uv, w_mla_proj, w_out, ln1_g, ln1_b, w_up, w_down, ln2_g, ln2_b):
    d = w_in.shape[0]
    o_z = 2 * D_MODEL
    o_xbc = o_z + D_INNER
    o_dt = o_xbc + CONV_DIM
    o_qa = o_dt + SSM_HEADS
    o_kva = o_qa + Q_RANK
    bf = lambda t: t.astype(BF16)
    row = lambda t: t.reshape(1, -1)
    lane_pad = lambda t: jnp.pad(t, ((0, 0), (0, LANES - t.shape[-1])))
    w_kv = w_in[:, o_kva:]
    w_kpe = w_kv[:, KV_RANK:]
    zpad = jnp.zeros((d, LANES - QK_ROPE), w_in.dtype)
    qb = w_q_b.reshape(Q_RANK, MLA_HEADS, QK_NOPE + QK_ROPE)
    qb_rope = qb[:, :, QK_NOPE:]
    npair = MLA_HEADS // 2
    k = np.arange(LANES)[:, None]
    c = np.arange(D_INNER)[None, :]
    expand = jnp.asarray((k < 3 * SSM_HEADS) & ((k % SSM_HEADS) == (c // SSM_HEAD_DIM)), BF16)
    w_in_b = bf(w_in)
    col_view = lambda width, start: (w_in_b, _column_spec(d, width, start))
    return dict(
        w_gs=col_view(D_MODEL, 0), b_gs=row(b_gate[:D_MODEL]),
        w_gm=col_view(D_MODEL, D_MODEL), b_gm=row(b_gate[D_MODEL:]),
        w_z=col_view(D_INNER, o_z), w_xbc=col_view(CONV_DIM, o_xbc), w_dt=col_view(LANES, o_dt),
        conv_w=conv_w, conv_b=row(conv_b), dt_bias=lane_pad(row(dt_bias)), a_log=lane_pad(row(a_log)),
        d_skip=row(jnp.repeat(d_skip, SSM_HEAD_DIM)), norm_g=row(ssm_norm_g), expand=expand,
        w_ssm_proj=bf(w_ssm_proj),
        w_qa=bf(w_in[:, o_qa:o_kva]), q_norm_g=row(q_norm_g),
        w_qb=bf(jnp.concatenate([qb[:, :, :QK_NOPE].reshape(Q_RANK, -1), qb_rope.reshape(Q_RANK, -1),
                                 _swap_halves(qb_rope).reshape(Q_RANK, -1)], axis=1)),
        w_ukt=bf(jnp.transpose(w_uk, (1, 2, 0))),
        w_qpair=bf(jnp.transpose(jnp.concatenate(
            [qb[:, :, :QK_NOPE].reshape(Q_RANK, npair, 2 * QK_NOPE), qb_rope.reshape(Q_RANK, npair, 2 * QK_ROPE)],
            axis=2), (1, 0, 2))),
        w_kvpair=bf(jnp.transpose(jnp.concatenate(
            [w_uk.reshape(KV_RANK, npair, 2 * QK_NOPE), w_uv.reshape(KV_RANK, npair, 2 * V_HEAD)], axis=2), (1, 0, 2))),
        w_kva=bf(jnp.concatenate([w_kv[:, :KV_RANK], w_kpe, zpad, _swap_halves(w_kpe), zpad], axis=1)),
        kv_norm_g=row(kv_norm_g), w_uv=bf(jnp.transpose(w_uv, (1, 0, 2))), w_mla_proj=bf(w_mla_proj),
        w_out=bf(w_out), ln1_g=row(ln1_g), ln1_b=row(ln1_b), w_up=bf(w_up), w_down=bf(w_down),
        ln2_g=row(ln2_g), ln2_b=row(ln2_b))


def _prompt_layer(x, w, alpha):
    b, l, d = x.shape
    cos, sin = _rope_tables(0, l)
    a_ssm, ssm, conv = _ssd_prompt(x, w)
    a_mla, ckv, kpe = _mla_prompt(x, cos, sin, w)
    y = _ffn(x.reshape(b * l, d), a_ssm.reshape(b * l, d), a_mla.reshape(b * l, d), w, alpha)
    return y.reshape(b, l, d), ckv, kpe, ssm.reshape(b, SSM_HEADS, SSM_HEAD_DIM, SSM_STATE), conv


def _sample_layer(x, cache_ckv, cache_kpe, state_ssm, state_conv, w, alpha):
    b, l, d = x.shape
    cos, sin = _rope_tables(cache_ckv.shape[1], l)
    a_ssm, ssm, conv = _ssd_sample(x, state_ssm, state_conv, w)
    a_mla, ckv, kpe = _mla_sample(x, cache_ckv, cache_kpe, cos, sin, w)
    y = _ffn(x.reshape(b * l, d), a_ssm, a_mla, w, alpha)
    return y.reshape(b, l, d), ckv, kpe, ssm.reshape(b, SSM_HEADS, SSM_HEAD_DIM, SSM_STATE), conv


def kernel(x_prompt, x_sample, cache_ckv, cache_kpe, state_ssm, state_conv, w_in, b_gate, conv_w, conv_b, dt_bias, a_log, d_skip, ssm_norm_g, w_ssm_proj, q_norm_g, w_q_b, kv_norm_g, w_uk, w_uv, w_mla_proj, w_out, ln1_g, ln1_b, w_up, w_down, ln2_g, ln2_b):
    depth = w_in.shape[0]
    alpha = (2 * depth) ** 0.25
    layer_params = (w_in, b_gate, conv_w, conv_b, dt_bias, a_log, d_skip, ssm_norm_g, w_ssm_proj, q_norm_g, w_q_b,
                    kv_norm_g, w_uk, w_uv, w_mla_proj, w_out, ln1_g, ln1_b, w_up, w_down, ln2_g, ln2_b)
    yp, ys = x_prompt, x_sample
    st_p, st_s = [], []
    for i in range(depth):
        w = _layer_weights(*(p[i] for p in layer_params))
        yp, *sp = _prompt_layer(yp, w, alpha)
        ys, *ss = _sample_layer(ys, cache_ckv[i], cache_kpe[i], state_ssm[i], state_conv[i], w, alpha)
        st_p.append(sp)
        st_s.append(ss)
    stack = lambda sts, k: jnp.stack([s[k] for s in sts])
    return (yp, ys, stack(st_p, 0), stack(st_p, 1), stack(st_p, 2), stack(st_p, 3),
            stack(st_s, 0), stack(st_s, 1), stack(st_s, 2), stack(st_s, 3))
```

```python
import functools
import math

import jax
import jax.numpy as jnp
import numpy as np
from jax import lax
from jax.experimental import pallas as pl
from jax.experimental.pallas import tpu as pltpu

D_MODEL = 1024
D_INNER = 2 * D_MODEL
SSM_HEAD_DIM = 64
SSM_HEADS = D_INNER // SSM_HEAD_DIM
SSM_GROUPS = 8
HEADS_PER_GROUP = SSM_HEADS // SSM_GROUPS
GROUP_DIM = HEADS_PER_GROUP * SSM_HEAD_DIM
SSM_STATE = 128
CONV_WIDTH = 4
CONV_DIM = D_INNER + 2 * SSM_GROUPS * SSM_STATE
MLA_HEADS = 16
QK_NOPE = 128
QK_ROPE = 64
V_HEAD = 128
Q_RANK = 512
KV_RANK = 512
QK_LAT = KV_RANK + QK_ROPE
ROPE_THETA = 10000.0
ATTN_SCALE = (QK_NOPE + QK_ROPE) ** -0.5
RMS_EPS = 1e-6
LN_EPS = 1e-5
CHUNK = 64
CHUNK_SHIFT = 6

LANES = 128
SUBLANES = 8
VMEM_LIMIT = 62 * 1024 * 1024

SSD_Q = 128
SSD_ROWS = 256
SSD_PROJ_TILE = 512
CONV_HALO = (CONV_WIDTH - 1) * SUBLANES
MLA_BLK = 256
DEC_TK = 4096
ROW_TILE = 256
MLA_ROWS = 1024
FFN_ROWS = 512

BF16 = jnp.bfloat16
F32 = jnp.float32
NEG_INF = float("-inf")
LOG2_E = math.log2(math.e)


def _dot(a, b):
    return jnp.dot(a, b, preferred_element_type=F32)


def _dot_nt(a, b):
    return lax.dot_general(a, b, (((1,), (1,)), ((), ())), preferred_element_type=F32)


def _rms(t):
    return t * lax.rsqrt(jnp.mean(t * t, axis=-1, keepdims=True) + RMS_EPS)


def _layernorm(t, g, b):
    mu = jnp.mean(t, axis=-1, keepdims=True)
    d = t - mu
    var = jnp.mean(d * d, axis=-1, keepdims=True)
    return d * lax.rsqrt(var + LN_EPS) * g + b


def _silu(t):
    return t * jax.nn.sigmoid(t)


def _softplus(t):
    return jnp.maximum(t, 0.0) + jnp.log(1.0 + jnp.exp(-jnp.abs(t)))


def _const_spec(shape):
    zeros = (0,) * len(shape)
    return pl.BlockSpec(shape, lambda *_: zeros, pipeline_mode=pl.Buffered(1))


def _column_spec(rows, width, start):
    assert start % width == 0
    return pl.BlockSpec((rows, width), lambda *_: (0, start // width), pipeline_mode=pl.Buffered(1))


def _operands(w, names):
    pairs = [w[n] if isinstance(w[n], tuple) else (w[n], _const_spec(w[n].shape)) for n in names]
    return [a for a, _ in pairs], [s for _, s in pairs]


def _params(n_axes):
    return pltpu.CompilerParams(dimension_semantics=("arbitrary",) * n_axes, vmem_limit_bytes=VMEM_LIMIT)


def _expand_heads(v, e_ref):
    hi = v.astype(BF16).astype(F32)
    r1 = v - hi
    mid = r1.astype(BF16).astype(F32)
    lo = r1 - mid
    lane = lax.broadcasted_iota(jnp.int32, v.shape, 1)
    packed = jnp.where(lane < 32, hi,
                       jnp.where(lane < 64, pltpu.roll(mid, 32, 1),
                                 jnp.where(lane < 96, pltpu.roll(lo, 64, 1), 0.0)))
    return _dot(packed.astype(BF16), e_ref[...])


def _time_of_row(r, q):
    return (r & (SUBLANES - 1)) * (q // SUBLANES) + (r >> 3)


def _ssd_block(xc, z, dt, neg_a2, dskip, norm_g, e_ref, st_ref, bd_ref, side_jobs=None, interleaved=False,
               seq_len=None):
    q = xc.shape[0]
    nseq = q // seq_len if seq_len else 1
    seq = seq_len or q
    row = lax.broadcasted_iota(jnp.int32, (q, q), 0)
    col = lax.broadcasted_iota(jnp.int32, (q, q), 1)
    same_seq = None
    if seq_len:
        shift = seq_len.bit_length() - 1
        same_seq = (row >> shift) == (col >> shift)
    if interleaved:
        row, col = _time_of_row(row, q), _time_of_row(col, q)
    causal = row >= col
    if seq_len:
        causal = causal & same_seq
    da = dt * neg_a2
    acum = jnp.dot(causal.astype(F32), da, precision=lax.Precision.HIGHEST, preferred_element_type=F32)
    acum_t = acum.T
    dt_full = _expand_heads(dt, e_ref)
    acum_full = _expand_heads(acum, e_ref)
    decay_full = jnp.exp2(acum_full)
    alasts = [acum_full[(i + 1) * seq - 1:(i + 1) * seq, :] for i in range(nseq)]
    alast_rows = alasts[0] if nseq == 1 else jnp.concatenate(
        [jnp.broadcast_to(t, (seq, D_INNER)) for t in alasts], axis=0)
    xs = xc[:, :D_INNER]
    xdt = xs * dt_full
    xdt_b = xdt.astype(BF16)
    xw_b = (xdt * jnp.exp2(alast_rows - acum_full)).astype(BF16)
    state_decays = [jnp.exp2(t) for t in alasts]
    token = lax.broadcasted_iota(jnp.int32, (1, q), 1)
    bm_t = xc[:, D_INNER:D_INNER + SSM_GROUPS * SSM_STATE].T.astype(BF16)
    cm_b = xc[:, D_INNER + SSM_GROUPS * SSM_STATE:].astype(BF16)
    ys = []
    for g in range(SSM_GROUPS):
        if side_jobs:
            side_jobs.pop(0)()
        gsl = slice(g * GROUP_DIM, (g + 1) * GROUP_DIM)
        nsl = slice(g * SSM_STATE, (g + 1) * SSM_STATE)
        cg = cm_b[:, nsl]
        bg_t = bm_t[nsl, :]
        cb = _dot(cg, bg_t)
        ms = []
        for r in range(HEADS_PER_GROUP):
            h = g * HEADS_PER_GROUP + r
            hsl = slice(r * SSM_HEAD_DIM, (r + 1) * SSM_HEAD_DIM)
            bd_ref[g, r * q:(r + 1) * q, hsl] = xdt_b[:, g * GROUP_DIM + r * SSM_HEAD_DIM:g * GROUP_DIM + (r + 1) * SSM_HEAD_DIM]
            seg = acum[:, h:h + 1] - acum_t[h:h + 1, :]
            ms.append((cb * jnp.exp2(jnp.where(causal, seg, NEG_INF))).astype(BF16))
        yg = _dot(jnp.concatenate(ms, axis=1), bd_ref[g])
        if seq_len is None:
            st = st_ref[g]
            yg = yg + _dot(cg, st.astype(BF16)) * decay_full[:, gsl]
            st_ref[g] = st * state_decays[0][:, gsl] + _dot(bg_t, xw_b[:, gsl])
        else:
            sts = [st_ref[i, g] for i in range(nseq)]
            wide = _dot(cg, jnp.concatenate([t.astype(BF16) for t in sts], axis=1))
            inter = jnp.concatenate(
                [wide[i * seq:(i + 1) * seq, i * GROUP_DIM:(i + 1) * GROUP_DIM] for i in range(nseq)], axis=0)
            yg = yg + inter * decay_full[:, gsl]
            for i in range(nseq):
                own = (token >= i * seq) & (token < (i + 1) * seq)
                st_ref[i, g] = (sts[i] * state_decays[i][:, gsl]
                                + _dot(jnp.where(own, bg_t, jnp.zeros_like(bg_t)), xw_b[:, gsl]))
        ys.append(yg)
    y = jnp.concatenate(ys, axis=1) + xs * dskip
    yz = y * _silu(z)
    outs = []
    for g in range(SSM_GROUPS):
        outs.append(_rms(yz[:, g * GROUP_DIM:(g + 1) * GROUP_DIM]))
    return jnp.concatenate(outs, axis=1) * norm_g


def _conv_silu(xp_ref, rows, convw, convb):
    acc = convb
    for k in range(CONV_WIDTH):
        acc = acc + xp_ref[SUBLANES - (CONV_WIDTH - 1) + k:SUBLANES - (CONV_WIDTH - 1) + k + rows, :] * convw[k:k + 1, :]
    return _silu(acc)


def _interleave_rows(xb, q, inverse=False):
    row = lax.broadcasted_iota(jnp.int32, (q, q), 0)
    col = lax.broadcasted_iota(jnp.int32, (q, q), 1)
    hit = (row == _time_of_row(col, q)) if inverse else (col == _time_of_row(row, q))
    perm = jnp.where(hit, 1.0, 0.0).astype(BF16)
    return jnp.concatenate([_dot(perm, xb[b0:b0 + q]).astype(BF16) for b0 in range(0, xb.shape[0], q)], axis=0)


def _conv_silu_interleaved(xp_ref, blk, prev, convw, convb):
    q = xp_ref.shape[1] - CONV_HALO
    first = lax.broadcasted_iota(jnp.int32, (SUBLANES, CONV_DIM), 0) == 0
    for j in range(CONV_WIDTH - 1):
        js = slice(j * SUBLANES, (j + 1) * SUBLANES)
        cur_j = xp_ref[blk, q + j * SUBLANES:q + (j + 1) * SUBLANES, :]
        xp_ref[blk, js, :] = jnp.where(first, pltpu.roll(prev[js], 1, 0), pltpu.roll(cur_j, 1, 0))
    acc = convb
    for k in range(CONV_WIDTH):
        acc = acc + xp_ref[blk, k * SUBLANES:k * SUBLANES + q, :] * convw[k:k + 1, :]
    return _silu(acc)


def _ssd_project_jobs(get_xb, get_xb_tokens, wz_ref, wxbc_ref, wdt_ref, wg_ref, bg_ref, dtb_ref,
                      z_ref, xp_ref, dt_ref, gate_ref):
    tile = SSD_PROJ_TILE
    q = xp_ref.shape[1] - CONV_HALO

    def xbc_job(c0):
        def job():
            res = _dot(get_xb(), wxbc_ref[:, c0:c0 + tile])
            for b in range(xp_ref.shape[0]):
                xp_ref[b, CONV_HALO:, c0:c0 + tile] = res[b * q:(b + 1) * q]
        return job

    def z_job(c0):
        def job():
            z_ref[:, c0:c0 + tile] = _dot(get_xb(), wz_ref[:, c0:c0 + tile])
        return job

    def gate_job(c0):
        def job():
            gate_ref[:, c0:c0 + tile] = jax.nn.sigmoid(_dot(get_xb_tokens(), wg_ref[:, c0:c0 + tile])
                                                       + bg_ref[:, c0:c0 + tile])
        return job

    def dt_job():
        dt_ref[...] = _softplus(_dot(get_xb(), wdt_ref[...]) + dtb_ref[...])

    return ([xbc_job(c0) for c0 in range(0, CONV_DIM, tile)] + [z_job(c0) for c0 in range(0, D_INNER, tile)]
            + [gate_job(c0) for c0 in range(0, D_MODEL, tile)] + [dt_job])


def _ssd_prompt_kernel(x0_ref, xn_ref, wz_ref, wxbc_ref, wdt_ref, wg_ref, bg_ref, convw_ref, convb_ref, dtb_ref,
                       alog_ref, dskip_ref, ng_ref, e_ref, wproj_ref,
                       a_out_ref, ssm_out_ref, conv_out_ref,
                       st_ref, bd_ref, hist_ref, xbn_ref, z_a, xp_a, dt_a, gate_a, z_b, xp_b, dt_b, gate_b, *,
                       steps_per_row):
    g = pl.program_id(0)
    c = g % steps_per_row
    nblk = xp_a.shape[0]
    proj_w = (wz_ref, wxbc_ref, wdt_ref, wg_ref, bg_ref, dtb_ref)

    @pl.when(g == 0)
    def _():
        bd_ref[...] = jnp.zeros_like(bd_ref)
        xb0 = x0_ref[...].astype(BF16)
        xbn_ref[...] = _interleave_rows(xb0, SSD_Q)
        for job in _ssd_project_jobs(lambda: xbn_ref[...], lambda: xb0, *proj_w, z_a, xp_a, dt_a, gate_a):
            job()

    @pl.when(c == 0)
    def _():
        st_ref[...] = jnp.zeros_like(st_ref)
        hist_ref[...] = jnp.zeros_like(hist_ref)

    def step(cur, nxt):
        z_ref, xp_ref, dt_ref, gate_ref = cur
        xbn_ref[...] = _interleave_rows(xn_ref[...].astype(BF16), SSD_Q)
        jobs = _ssd_project_jobs(lambda: xbn_ref[...], lambda: xn_ref[...].astype(BF16), *proj_w, *nxt)
        neg_a2 = -jnp.exp(alog_ref[...]) * LOG2_E
        yn = []
        for s in range(nblk):
            sl = slice(s * SSD_Q, (s + 1) * SSD_Q)
            prev = hist_ref[...] if s == 0 else xp_ref[s - 1, SSD_Q:, :]
            xc = _conv_silu_interleaved(xp_ref, s, prev, convw_ref[...], convb_ref[...])
            yn.append(_ssd_block(xc, z_ref[sl, :], dt_ref[sl, :], neg_a2, dskip_ref[...], ng_ref[...], e_ref,
                                 st_ref, bd_ref, jobs, interleaved=True))
        hist_ref[...] = xp_ref[nblk - 1, SSD_Q:, :]
        for job in jobs:
            job()
        yn = _interleave_rows(jnp.concatenate(yn, axis=0).astype(BF16), SSD_Q, inverse=True)
        a_out_ref[...] = gate_ref[...] * _dot(yn, wproj_ref[...])

    slot_a, slot_b = (z_a, xp_a, dt_a, gate_a), (z_b, xp_b, dt_b, gate_b)

    @pl.when(g % 2 == 0)
    def _():
        step(slot_a, slot_b)

    @pl.when(g % 2 == 1)
    def _():
        step(slot_b, slot_a)

    @pl.when(c == steps_per_row - 1)
    def _():
        for j in range(CONV_WIDTH - 1):
            conv_out_ref[j:j + 1, :] = hist_ref[(j + 1) * SUBLANES - 1:(j + 1) * SUBLANES, :]
        for k in range(SSM_GROUPS):
            ssm_out_ref[k * GROUP_DIM:(k + 1) * GROUP_DIM, :] = st_ref[k].T


def _ssd_sample_kernel(x_ref, state_ref, cstate_ref, wz_ref, wxbc_ref, wdt_ref, wg_ref, bg_ref, convw_ref, convb_ref,
                       dtb_ref, alog_ref, dskip_ref, ng_ref, e_ref, wproj_ref,
                       a_out_ref, ssm_out_ref, conv_out_ref,
                       st_ref, bd_ref, xp_ref, xc_scr):
    nseq, seq = cstate_ref.shape[0], x_ref.shape[0] // cstate_ref.shape[0]
    xb = x_ref[...].astype(BF16)
    z = _dot(xb, wz_ref[...])
    xbc = _dot(xb, wxbc_ref[...])
    dt = _softplus(_dot(xb, wdt_ref[...]) + dtb_ref[...])
    gate = jax.nn.sigmoid(_dot(xb, wg_ref[...]) + bg_ref[...])
    bd_ref[...] = jnp.zeros_like(bd_ref)
    lo = SUBLANES - (CONV_WIDTH - 1)
    for s in range(nseq):
        xp_ref[s, lo:SUBLANES, :] = cstate_ref[s]
        xp_ref[s, SUBLANES:, :] = xbc[s * seq:(s + 1) * seq]
        xc_scr[s * seq:(s + 1) * seq, :] = _conv_silu(xp_ref.at[s], seq, convw_ref[...], convb_ref[...])
        conv_out_ref[s] = xp_ref[s, seq + lo:, :]
        for g in range(SSM_GROUPS):
            st_ref[s, g] = state_ref[s, g * GROUP_DIM:(g + 1) * GROUP_DIM, :].T
    yn = _ssd_block(xc_scr[...], z, dt, -jnp.exp(alog_ref[...]) * LOG2_E, dskip_ref[...], ng_ref[...], e_ref, st_ref,
                    bd_ref, seq_len=seq)
    for s in range(nseq):
        for g in range(SSM_GROUPS):
            ssm_out_ref[s, g * GROUP_DIM:(g + 1) * GROUP_DIM, :] = st_ref[s, g].T
    a_out_ref[...] = gate * _dot(yn.astype(BF16), wproj_ref[...])


def _ssd_weight_specs(w):
    names = ("w_z", "w_xbc", "w_dt", "w_gs", "b_gs", "conv_w", "conv_b", "dt_bias", "a_log", "d_skip", "norm_g",
             "expand", "w_ssm_proj")
    return _operands(w, names)


def _ssd_prompt(x, w):
    b, l, d = x.shape
    rows = min(SSD_ROWS, l)
    assert l % rows == 0 and rows % SSD_Q == 0
    arrs, specs = _ssd_weight_specs(w)
    nc = l // rows
    last = b * nc - 1
    slot = [pltpu.VMEM((rows, D_INNER), F32), pltpu.VMEM((rows // SSD_Q, CONV_HALO + SSD_Q, CONV_DIM), F32),
            pltpu.VMEM((rows, LANES), F32), pltpu.VMEM((rows, D_MODEL), F32)]

    def next_block(g):
        n = jnp.minimum(g + 1, last)
        return (n // nc, n % nc, 0)

    return pl.pallas_call(
        functools.partial(_ssd_prompt_kernel, steps_per_row=nc),
        grid=(b * nc,),
        in_specs=[pl.BlockSpec((None, rows, d), lambda g: (0, 0, 0)),
                  pl.BlockSpec((None, rows, d), next_block)] + specs,
        out_specs=[pl.BlockSpec((None, rows, D_MODEL), lambda g: (g // nc, g % nc, 0)),
                   pl.BlockSpec((None, D_INNER, SSM_STATE), lambda g: (g // nc, 0, 0)),
                   pl.BlockSpec((None, CONV_WIDTH - 1, CONV_DIM), lambda g: (g // nc, 0, 0))],
        out_shape=[jax.ShapeDtypeStruct((b, l, D_MODEL), F32),
                   jax.ShapeDtypeStruct((b, D_INNER, SSM_STATE), F32),
                   jax.ShapeDtypeStruct((b, CONV_WIDTH - 1, CONV_DIM), F32)],
        scratch_shapes=[pltpu.VMEM((SSM_GROUPS, SSM_STATE, GROUP_DIM), F32),
                        pltpu.VMEM((SSM_GROUPS, HEADS_PER_GROUP * SSD_Q, GROUP_DIM), BF16),
                        pltpu.VMEM((CONV_HALO, CONV_DIM), F32),
                        pltpu.VMEM((rows, d), BF16)] + slot + slot,
        compiler_params=_params(1),
        name="ssd_prompt",
    )(x, x, *arrs)


def _ssd_sample(x, state, cstate, w):
    b, l, d = x.shape
    nseq = SSD_Q // l
    assert SSD_Q % l == 0 and (l & (l - 1)) == 0 and l % SUBLANES == 0 and b % nseq == 0
    rows = nseq * l
    arrs, specs = _ssd_weight_specs(w)
    a, ssm, conv = pl.pallas_call(
        _ssd_sample_kernel,
        grid=(b // nseq,),
        in_specs=[pl.BlockSpec((rows, d), lambda i: (i, 0)),
                  pl.BlockSpec((nseq, D_INNER, SSM_STATE), lambda i: (i, 0, 0)),
                  pl.BlockSpec((nseq, CONV_WIDTH - 1, CONV_DIM), lambda i: (i, 0, 0))] + specs,
        out_specs=[pl.BlockSpec((rows, D_MODEL), lambda i: (i, 0)),
                   pl.BlockSpec((nseq, D_INNER, SSM_STATE), lambda i: (i, 0, 0)),
                   pl.BlockSpec((nseq, CONV_WIDTH - 1, CONV_DIM), lambda i: (i, 0, 0))],
        out_shape=[jax.ShapeDtypeStruct((b * l, D_MODEL), F32),
                   jax.ShapeDtypeStruct((b, D_INNER, SSM_STATE), F32),
                   jax.ShapeDtypeStruct((b, CONV_WIDTH - 1, CONV_DIM), F32)],
        scratch_shapes=[pltpu.VMEM((nseq, SSM_GROUPS, SSM_STATE, GROUP_DIM), F32),
                        pltpu.VMEM((SSM_GROUPS, HEADS_PER_GROUP * SSD_Q, GROUP_DIM), BF16),
                        pltpu.VMEM((nseq, l + SUBLANES, CONV_DIM), F32),
                        pltpu.VMEM((rows, CONV_DIM), F32)],
        compiler_params=_params(1),
        name="ssd_sample",
    )(x.reshape(b * l, d), state.reshape(b, D_INNER, SSM_STATE), cstate, *arrs)
    return a, ssm, conv


def _mla_keys(xb, cos, sin, wkva_ref, kvg_ref):
    kva = _dot(xb, wkva_ref[...])
    ckv = _rms(kva[:, :KV_RANK]) * kvg_ref[...]
    kpe = (kva[:, KV_RANK:KV_RANK + QK_ROPE] * cos[:, :QK_ROPE]
           + kva[:, KV_RANK + LANES:KV_RANK + LANES + QK_ROPE] * sin[:, :QK_ROPE])
    return ckv, kpe


def _mla_queries(xb, cos, sin, wqa_ref, qg_ref, wqb_ref, wukt_ref, store):
    qn = (_rms(_dot(xb, wqa_ref[...])) * qg_ref[...]).astype(BF16)
    q = _dot(qn, wqb_ref[...])
    nope_w = MLA_HEADS * QK_NOPE
    rope_w = MLA_HEADS * QK_ROPE
    for p in range(MLA_HEADS // 2):
        sl = slice(nope_w + p * LANES, nope_w + (p + 1) * LANES)
        sl_sw = slice(nope_w + rope_w + p * LANES, nope_w + rope_w + (p + 1) * LANES)
        pair = (q[:, sl] * cos + q[:, sl_sw] * sin) * ATTN_SCALE
        pair_hi = pltpu.roll(pair, QK_ROPE, 1)
        for h, pe in ((2 * p, pair), (2 * p + 1, pair_hi)):
            q_lat = _dot(q[:, h * QK_NOPE:(h + 1) * QK_NOPE].astype(BF16), wukt_ref[h]) * ATTN_SCALE
            store(h, q_lat.astype(BF16), pe[:, :QK_ROPE].astype(BF16))


def _lane_tile(t, width):
    return jnp.concatenate([t] * (width // LANES), axis=1) if width > LANES else t


def _softmax_update(s, v_b, m_scr, l_scr, acc_scr):
    m_prev = m_scr[...]
    m_new = jnp.maximum(m_prev, jnp.max(s, axis=-1, keepdims=True))
    alpha = jnp.exp(m_prev - m_new)
    p = jnp.exp(s - _lane_tile(m_new, s.shape[1]))
    p_lanes = p[:, 0:LANES]
    for c in range(1, s.shape[1] // LANES):
        p_lanes = p_lanes + p[:, c * LANES:(c + 1) * LANES]
    l_scr[...] = alpha * l_scr[...] + p_lanes
    acc_scr[...] = _lane_tile(alpha, acc_scr.shape[1]) * acc_scr[...] + _dot(p.astype(BF16), v_b)
    m_scr[...] = m_new


def _softmax_finish(l_scr, acc_scr):
    return acc_scr[...] / jnp.sum(l_scr[...], axis=-1, keepdims=True)


def _mla_out(o_of_head, xb, wuv_ref, wmla_ref, wg_ref, bg_ref, o_scr):
    for h in range(MLA_HEADS):
        hsl = slice(h * V_HEAD, (h + 1) * V_HEAD)
        o_scr[:, hsl] = _dot(o_of_head(h), wuv_ref[:, hsl]).astype(BF16)
    gate = jax.nn.sigmoid(_dot(xb, wg_ref[...]) + bg_ref[...])
    return gate * _dot(o_scr[...], wmla_ref[...])


def _mla_pre_kernel(x_ref, cos_ref, sin_ref, wqa_ref, qg_ref, wkva_ref, kvg_ref, qn_out_ref, ckv_out_ref, kpe_out_ref):
    xb = x_ref[...].astype(BF16)
    ckv, kpe = _mla_keys(xb, cos_ref[...], sin_ref[...], wkva_ref, kvg_ref)
    ckv_out_ref[...] = ckv
    kpe_out_ref[...] = kpe
    qn_out_ref[...] = (_rms(_dot(xb, wqa_ref[...])) * qg_ref[...]).astype(BF16)


def _mla_prompt_attn_kernel(qn_ref, ckv_ref, kpe_ref, cos_ref, sin_ref, wqp_ref, wuk_ref, wuv_ref, o_ref,
                            qa_scr, qb_scr, ka_scr, kb_scr, vbd_scr, m_scr, l_scr, acc_scr):
    l = qn_ref.shape[0]
    g = min(MLA_BLK, l)
    nb = l // g
    half = QK_ROPE
    zpad = jnp.zeros((l, LANES - half), BF16)
    ckv_b = ckv_ref[...].astype(BF16)
    kv = jnp.concatenate([_dot(ckv_b, wuk_ref[...]), _dot(ckv_b, wuv_ref[...])], axis=1)
    kpe_b = kpe_ref[...].astype(BF16)
    for scr, c0 in ((ka_scr, 0), (kb_scr, QK_NOPE)):
        scr[:, 0:QK_NOPE] = kv[:, c0:c0 + QK_NOPE].astype(BF16)
        scr[:, QK_NOPE:QK_NOPE + half] = kpe_b
        scr[:, QK_NOPE + half:] = zpad
    zv = jnp.zeros((g, V_HEAD), BF16)
    for j in range(nb):
        rs = slice(j * g, (j + 1) * g)
        vbd_scr[j, 0:g, 0:V_HEAD] = kv[rs, 2 * QK_NOPE:2 * QK_NOPE + V_HEAD].astype(BF16)
        vbd_scr[j, 0:g, V_HEAD:] = zv
        vbd_scr[j, g:, 0:V_HEAD] = zv
        vbd_scr[j, g:, V_HEAD:] = kv[rs, 2 * QK_NOPE + V_HEAD:].astype(BF16)
    q = _dot(qn_ref[...], wqp_ref[...])
    c = ATTN_SCALE * LOG2_E
    rope = q[:, 2 * QK_NOPE:]
    upper = (lax.broadcasted_iota(jnp.int32, rope.shape, 1) & (half // 2)) != 0
    swapped = jnp.where(upper, pltpu.roll(rope, half // 2, 1), pltpu.roll(rope, LANES - half // 2, 1))
    pair = (rope * cos_ref[...] + swapped * sin_ref[...]) * c
    pair_hi = pltpu.roll(pair, half, 1)
    for scr, c0, pe in ((qa_scr, 0, pair), (qb_scr, QK_NOPE, pair_hi)):
        scr[:, 0:QK_NOPE] = (q[:, c0:c0 + QK_NOPE] * c).astype(BF16)
        scr[:, QK_NOPE:QK_NOPE + half] = pe[:, :half].astype(BF16)
        scr[:, QK_NOPE + half:] = zpad
    row = lax.broadcasted_iota(jnp.int32, (g, g), 0)
    col = lax.broadcasted_iota(jnp.int32, (g, g), 1)
    diag_visible = (row >> CHUNK_SHIFT) >= (col >> CHUNK_SHIFT)
    for j in range(nb):
        r0 = j * g
        ps, alphas = [], []
        for hi, (q_scr, k_scr) in enumerate(((qa_scr, ka_scr), (qb_scr, kb_scr))):
            s = _dot_nt(q_scr[r0:, :], k_scr[r0:r0 + g, :])
            top = jnp.where(diag_visible, s[:g], NEG_INF)
            s = top if j == nb - 1 else jnp.concatenate([top, s[g:]], axis=0)
            m_cur = jnp.max(s, axis=-1, keepdims=True)
            if j == 0:
                m_new = jnp.broadcast_to(m_cur, (l, LANES))
            else:
                m_prev = m_scr[hi, r0:, :]
                m_new = jnp.maximum(m_prev, m_cur)
                alphas.append(jnp.exp2(m_prev - m_new))
            p = jnp.exp2(s - _lane_tile(m_new, g))
            p_lanes = p[:, 0:LANES]
            for cc in range(1, g // LANES):
                p_lanes = p_lanes + p[:, cc * LANES:(cc + 1) * LANES]
            l_scr[hi, r0:, :] = p_lanes if j == 0 else alphas[-1] * l_scr[hi, r0:, :] + p_lanes
            m_scr[hi, r0:, :] = m_new
            ps.append(p.astype(BF16))
        pv = _dot(jnp.concatenate(ps, axis=1), vbd_scr[j])
        if j == 0:
            acc_scr[...] = pv
        else:
            acc_scr[r0:, :] = jnp.concatenate(alphas, axis=1) * acc_scr[r0:, :] + pv
    inv = [1.0 / jnp.sum(l_scr[hi], axis=-1, keepdims=True) for hi in range(2)]
    scale = jnp.concatenate([jnp.broadcast_to(t, (l, V_HEAD)) for t in inv], axis=1)
    o_ref[...] = (acc_scr[...] * scale).astype(BF16)


def _mla_post_kernel(o_ref, x_ref, wmla_ref, wg_ref, bg_ref, a_out_ref):
    gate = jax.nn.sigmoid(_dot(x_ref[...].astype(BF16), wg_ref[...]) + bg_ref[...])
    a_out_ref[...] = gate * _dot(o_ref[...], wmla_ref[...])


def _mla_prompt(x, cos, sin, w):
    b, l, d = x.shape
    g = min(MLA_BLK, l)
    tm = min(MLA_ROWS, l)
    npair = MLA_HEADS // 2
    assert l % g == 0 and g % CHUNK == 0 and g % LANES == 0 and l % tm == 0
    nt = l // tm
    pre = [w[n] for n in ("w_qa", "q_norm_g", "w_kva", "kv_norm_g")]
    qn, ckv, kpe = pl.pallas_call(
        _mla_pre_kernel,
        grid=(b, nt),
        in_specs=[pl.BlockSpec((None, tm, d), lambda i, c: (i, c, 0)),
                  pl.BlockSpec((tm, LANES), lambda i, c: (c, 0)),
                  pl.BlockSpec((tm, LANES), lambda i, c: (c, 0))] + [_const_spec(a.shape) for a in pre],
        out_specs=[pl.BlockSpec((None, tm, Q_RANK), lambda i, c: (i, c, 0)),
                   pl.BlockSpec((None, tm, KV_RANK), lambda i, c: (i, c, 0)),
                   pl.BlockSpec((None, tm, QK_ROPE), lambda i, c: (i, c, 0))],
        out_shape=[jax.ShapeDtypeStruct((b, l, Q_RANK), BF16),
                   jax.ShapeDtypeStruct((b, l, KV_RANK), F32),
                   jax.ShapeDtypeStruct((b, l, QK_ROPE), F32)],
        compiler_params=_params(2),
        name="mla_pre",
    )(x, cos, sin, *pre)
    o = pl.pallas_call(
        _mla_prompt_attn_kernel,
        grid=(b, npair),
        in_specs=[pl.BlockSpec((None, l, Q_RANK), lambda i, p: (i, 0, 0)),
                  pl.BlockSpec((None, l, KV_RANK), lambda i, p: (i, 0, 0)),
                  pl.BlockSpec((None, l, QK_ROPE), lambda i, p: (i, 0, 0)),
                  _const_spec(cos.shape), _const_spec(sin.shape),
                  pl.BlockSpec((None, Q_RANK, 2 * QK_NOPE + 2 * QK_ROPE), lambda i, p: (p, 0, 0)),
                  pl.BlockSpec((KV_RANK, 2 * QK_NOPE), lambda i, p: (0, p)),
                  pl.BlockSpec((KV_RANK, 2 * V_HEAD), lambda i, p: (0, p))],
        out_specs=pl.BlockSpec((None, l, 2 * V_HEAD), lambda i, p: (i, 0, p)),
        out_shape=jax.ShapeDtypeStruct((b, l, MLA_HEADS * V_HEAD), BF16),
        scratch_shapes=[pltpu.VMEM((l, 2 * LANES), BF16)] * 4
        + [pltpu.VMEM((l // g, 2 * g, 2 * V_HEAD), BF16),
           pltpu.VMEM((2, l, LANES), F32),
           pltpu.VMEM((2, l, LANES), F32),
           pltpu.VMEM((l, 2 * V_HEAD), F32)],
        compiler_params=_params(2),
        name="mla_prompt_attn",
    )(qn, ckv, kpe, cos, sin, w["w_qpair"], w["w_uk2d"], w["w_uv2d"])
    post, post_specs = _operands(w, ("w_mla_proj", "w_gm", "b_gm"))
    a = pl.pallas_call(
        _mla_post_kernel,
        grid=(b, nt),
        in_specs=[pl.BlockSpec((None, tm, MLA_HEADS * V_HEAD), lambda i, c: (i, c, 0)),
                  pl.BlockSpec((None, tm, d), lambda i, c: (i, c, 0))] + post_specs,
        out_specs=pl.BlockSpec((None, tm, D_MODEL), lambda i, c: (i, c, 0)),
        out_shape=jax.ShapeDtypeStruct((b, l, D_MODEL), F32),
        compiler_params=_params(2),
        name="mla_post",
    )(o, x, *post)
    return a, ckv, kpe


def _mla_dec_pre_kernel(x_ref, cos_ref, sin_ref, wqa_ref, qg_ref, wqb_ref, wukt_ref, wkva_ref, kvg_ref,
                        q_out_ref, ckv_out_ref, kpe_out_ref):
    xb = x_ref[...].astype(BF16)
    cos, sin = cos_ref[...], sin_ref[...]
    ckv, kpe = _mla_keys(xb, cos, sin, wkva_ref, kvg_ref)
    ckv_out_ref[...] = ckv
    kpe_out_ref[...] = kpe

    def store(h, q_lat, q_pe):
        q_out_ref[h, :, 0:KV_RANK] = q_lat
        q_out_ref[h, :, KV_RANK:QK_LAT] = q_pe

    _mla_queries(xb, cos, sin, wqa_ref, qg_ref, wqb_ref, wukt_ref, store)


def _mla_dec_attn_kernel(q_ref, cckv_ref, ckpe_t_ref, nckv_ref, nkpe_ref, o_ref,
                         m_scr, l_scr, acc_scr, nk_scr, *, past_len):
    j = pl.program_id(1)
    seq = q_ref.shape[1]
    rows = MLA_HEADS * seq

    @pl.when(j == 0)
    def _():
        m_scr[...] = jnp.full_like(m_scr, NEG_INF)
        l_scr[...] = jnp.zeros_like(l_scr)
        acc_scr[...] = jnp.zeros_like(acc_scr)

    q = q_ref[...].reshape(rows, QK_LAT)
    ck = cckv_ref[...].astype(BF16)
    kp_t = ckpe_t_ref[...].astype(BF16)
    s = _dot_nt(q[:, :KV_RANK], ck) + _dot(q[:, KV_RANK:], kp_t)
    _softmax_update(s, ck, m_scr, l_scr, acc_scr)

    @pl.when(j == pl.num_programs(1) - 1)
    def _():
        nk_scr[...] = jnp.zeros_like(nk_scr)
        nk_scr[0:seq, 0:KV_RANK] = nckv_ref[...].astype(BF16)
        nk_scr[0:seq, KV_RANK:QK_LAT] = nkpe_ref[...].astype(BF16)
        nk = nk_scr[...]
        s2 = _dot_nt(q, nk)
        row = lax.broadcasted_iota(jnp.int32, (rows, 1), 0)
        q_chunk = (past_len + (row & (seq - 1))) >> CHUNK_SHIFT
        k_lane = lax.broadcasted_iota(jnp.int32, (1, LANES), 1)
        visible = (k_lane < seq) & (((past_len + k_lane) >> CHUNK_SHIFT) <= q_chunk)
        _softmax_update(jnp.where(visible, s2, NEG_INF), nk[:, :KV_RANK], m_scr, l_scr, acc_scr)
        o_ref[...] = _softmax_finish(l_scr, acc_scr).astype(BF16).reshape(MLA_HEADS, seq, KV_RANK)


def _mla_dec_post_kernel(o_ref, x_ref, wuv_ref, wmla_ref, wg_ref, bg_ref, a_out_ref, o_scr):
    xb = x_ref[...].astype(BF16)
    a_out_ref[...] = _mla_out(lambda h: o_ref[h], xb, wuv_ref, wmla_ref, wg_ref, bg_ref, o_scr)


def _mla_sample(x, cache_ckv, cache_kpe, cos, sin, w):
    b, l, d = x.shape
    t = b * l
    past_len = cache_ckv.shape[1]
    tm = min(ROW_TILE, t)
    tk = min(DEC_TK, past_len)
    assert t % tm == 0 and past_len % tk == 0 and (l & (l - 1)) == 0 and l <= LANES and l % 16 == 0
    x2 = x.reshape(t, d)
    pre_names = ("w_qa", "q_norm_g", "w_qb", "w_ukt", "w_kva", "kv_norm_g")
    pre = [w[n] for n in pre_names]
    q, ckv, kpe = pl.pallas_call(
        _mla_dec_pre_kernel,
        grid=(t // tm,),
        in_specs=[pl.BlockSpec((tm, d), lambda i: (i, 0)),
                  pl.BlockSpec((tm, LANES), lambda i: (i, 0)),
                  pl.BlockSpec((tm, LANES), lambda i: (i, 0))] + [_const_spec(a.shape) for a in pre],
        out_specs=[pl.BlockSpec((MLA_HEADS, tm, QK_LAT), lambda i: (0, i, 0)),
                   pl.BlockSpec((tm, KV_RANK), lambda i: (i, 0)),
                   pl.BlockSpec((tm, QK_ROPE), lambda i: (i, 0))],
        out_shape=[jax.ShapeDtypeStruct((MLA_HEADS, t, QK_LAT), BF16),
                   jax.ShapeDtypeStruct((t, KV_RANK), F32),
                   jax.ShapeDtypeStruct((t, QK_ROPE), F32)],
        compiler_params=_params(1),
        name="mla_dec_pre",
    )(x2, jnp.tile(cos, (b, 1)), jnp.tile(sin, (b, 1)), *pre)
    rows = MLA_HEADS * l
    o = pl.pallas_call(
        functools.partial(_mla_dec_attn_kernel, past_len=past_len),
        grid=(b, past_len // tk),
        in_specs=[pl.BlockSpec((MLA_HEADS, l, QK_LAT), lambda i, j: (0, i, 0)),
                  pl.BlockSpec((None, tk, KV_RANK), lambda i, j: (i, j, 0)),
                  pl.BlockSpec((None, QK_ROPE, tk), lambda i, j: (i, 0, j)),
                  pl.BlockSpec((l, KV_RANK), lambda i, j: (i, 0)),
                  pl.BlockSpec((l, QK_ROPE), lambda i, j: (i, 0))],
        out_specs=pl.BlockSpec((MLA_HEADS, l, KV_RANK), lambda i, j: (0, i, 0)),
        out_shape=jax.ShapeDtypeStruct((MLA_HEADS, t, KV_RANK), BF16),
        scratch_shapes=[pltpu.VMEM((rows, LANES), F32),
                        pltpu.VMEM((rows, LANES), F32),
                        pltpu.VMEM((rows, KV_RANK), F32),
                        pltpu.VMEM((LANES, QK_LAT), BF16)],
        compiler_params=_params(2),
        name="mla_dec_attn",
    )(q, cache_ckv, jnp.swapaxes(cache_kpe, 1, 2), ckv, kpe)
    post_names = ("w_uv2d", "w_mla_proj", "w_gm", "b_gm")
    post, post_specs = _operands(w, post_names)
    a = pl.pallas_call(
        _mla_dec_post_kernel,
        grid=(t // tm,),
        in_specs=[pl.BlockSpec((MLA_HEADS, tm, KV_RANK), lambda i: (0, i, 0)),
                  pl.BlockSpec((tm, d), lambda i: (i, 0))] + post_specs,
        out_specs=pl.BlockSpec((tm, D_MODEL), lambda i: (i, 0)),
        out_shape=jax.ShapeDtypeStruct((t, D_MODEL), F32),
        scratch_shapes=[pltpu.VMEM((tm, MLA_HEADS * V_HEAD), BF16)],
        compiler_params=_params(1),
        name="mla_dec_post",
    )(o, x2, *post)
    return a, ckv.reshape(b, l, KV_RANK), kpe.reshape(b, l, QK_ROPE)


def _ffn_kernel(x_ref, as_ref, am_ref, wout_ref, g1_ref, b1_ref, wup_ref, wdown_ref, g2_ref, b2_ref, y_ref, *, alpha):
    mix = _dot((as_ref[...] + am_ref[...]).astype(BF16), wout_ref[...])
    h = _layernorm(alpha * x_ref[...] + mix, g1_ref[...], b1_ref[...])
    up = jnp.maximum(_dot(h.astype(BF16), wup_ref[...]), 0.0)
    ff = _dot((up * up).astype(BF16), wdown_ref[...])
    y_ref[...] = _layernorm(alpha * h + ff, g2_ref[...], b2_ref[...])


def _ffn(x2, a_ssm, a_mla, w, alpha):
    t, d = x2.shape
    tm = min(FFN_ROWS, t)
    assert t % tm == 0
    names = ("w_out", "ln1_g", "ln1_b", "w_up", "w_down", "ln2_g", "ln2_b")
    arrs = [w[n] for n in names]
    row_spec = pl.BlockSpec((tm, d), lambda i: (i, 0))
    return pl.pallas_call(
        functools.partial(_ffn_kernel, alpha=alpha),
        grid=(t // tm,),
        in_specs=[row_spec, row_spec, row_spec] + [_const_spec(a.shape) for a in arrs],
        out_specs=row_spec,
        out_shape=jax.ShapeDtypeStruct((t, d), F32),
        compiler_params=_params(1),
        name="ffn",
    )(x2, a_ssm, a_mla, *arrs)


def _rope_tables(pos0, n):
    inv = ROPE_THETA ** (-np.arange(0, QK_ROPE, 2, dtype=np.float64) / QK_ROPE)
    ang = (pos0 + np.arange(n, dtype=np.float64))[:, None] * inv[None, :]
    cos, sin = np.cos(ang), np.sin(ang)
    cos2 = np.concatenate([cos, cos], axis=-1)
    sin2 = np.concatenate([-sin, sin], axis=-1)
    reps = (1, LANES // QK_ROPE)
    return jnp.asarray(np.tile(cos2, reps), F32), jnp.asarray(np.tile(sin2, reps), F32)


def _swap_halves(t):
    half = t.shape[-1] // 2
    return jnp.concatenate([t[..., half:], t[..., :half]], axis=-1)


def _layer_weights(w_in, b_gate, conv_w, conv_b, dt_bias, a_log, d_skip, ssm_norm_g, w_ssm_proj, q_norm_g, w_q_b,
                   kv_norm_g, w_uk, w_uv, w_mla_proj, w_out, ln1_g, ln1_b, w_up, w_down, ln2_g, ln2_b):
    d = w_in.shape[0]
    o_z = 2 * D_MODEL
    o_xbc = o_z + D_INNER
    o_dt = o_xbc + CONV_DIM
    o_qa = o_dt + SSM_HEADS
    o_kva = o_qa + Q_RANK
    bf = lambda t: t.astype(BF16)
    row = lambda t: t.reshape(1, -1)
    lane_pad = lambda t: jnp.pad(t, ((0, 0), (0, LANES - t.shape[-1])))
    w_kv = w_in[:, o_kva:]
    w_kpe = w_kv[:, KV_RANK:]
    zpad = jnp.zeros((d, LANES - QK_ROPE), w_in.dtype)
    qb = w_q_b.reshape(Q_RANK, MLA_HEADS, QK_NOPE + QK_ROPE)
    qb_rope = qb[:, :, QK_NOPE:]
    npair = MLA_HEADS // 2
    k = np.arange(LANES)[:, None]
    c = np.arange(D_INNER)[None, :]
    expand = jnp.asarray((k < 3 * SSM_HEADS) & ((k % SSM_HEADS) == (c // SSM_HEAD_DIM)), BF16)
    w_in_b = bf(w_in)
    col_view = lambda width, start: (w_in_b, _column_spec(d, width, start))
    return dict(
        w_gs=col_view(D_MODEL, 0), b_gs=row(b_gate[:D_MODEL]),
        w_gm=col_view(D_MODEL, D_MODEL), b_gm=row(b_gate[D_MODEL:]),
        w_z=col_view(D_INNER, o_z), w_xbc=col_view(CONV_DIM, o_xbc), w_dt=col_view(LANES, o_dt),
        conv_w=conv_w, conv_b=row(conv_b), dt_bias=lane_pad(row(dt_bias)), a_log=lane_pad(row(a_log)),
        d_skip=row(jnp.repeat(d_skip, SSM_HEAD_DIM)), norm_g=row(ssm_norm_g), expand=expand,
        w_ssm_proj=bf(w_ssm_proj),
        w_qa=bf(w_in[:, o_qa:o_kva]), q_norm_g=row(q_norm_g),
        w_qb=bf(jnp.concatenate([qb[:, :, :QK_NOPE].reshape(Q_RANK, -1), qb_rope.reshape(Q_RANK, -1),
                                 _swap_halves(qb_rope).reshape(Q_RANK, -1)], axis=1)),
        w_ukt=bf(jnp.transpose(w_uk, (1, 2, 0))),
        w_qpair=bf(jnp.transpose(jnp.concatenate(
            [qb[:, :, :QK_NOPE].reshape(Q_RANK, npair, 2 * QK_NOPE), qb_rope.reshape(Q_RANK, npair, 2 * QK_ROPE)],
            axis=2), (1, 0, 2))),
        w_uk2d=bf(w_uk.reshape(KV_RANK, MLA_HEADS * QK_NOPE)), w_uv2d=bf(w_uv.reshape(KV_RANK, MLA_HEADS * V_HEAD)),
        w_kva=bf(jnp.concatenate([w_kv[:, :KV_RANK], w_kpe, zpad, _swap_halves(w_kpe), zpad], axis=1)),
        kv_norm_g=row(kv_norm_g), w_mla_proj=bf(w_mla_proj),
        w_out=bf(w_out), ln1_g=row(ln1_g), ln1_b=row(ln1_b), w_up=bf(w_up), w_down=bf(w_down),
        ln2_g=row(ln2_g), ln2_b=row(ln2_b))


def _prompt_layer(x, w, alpha):
    b, l, d = x.shape
    cos, sin = _rope_tables(0, l)
    a_ssm, ssm, conv = _ssd_prompt(x, w)
    a_mla, ckv, kpe = _mla_prompt(x, cos, sin, w)
    y = _ffn(x.reshape(b * l, d), a_ssm.reshape(b * l, d), a_mla.reshape(b * l, d), w, alpha)
    return y.reshape(b, l, d), ckv, kpe, ssm.reshape(b, SSM_HEADS, SSM_HEAD_DIM, SSM_STATE), conv


def _sample_layer(x, cache_ckv, cache_kpe, state_ssm, state_conv, w, alpha):
    b, l, d = x.shape
    cos, sin = _rope_tables(cache_ckv.shape[1], l)
    a_ssm, ssm, conv = _ssd_sample(x, state_ssm, state_conv, w)
    a_mla, ckv, kpe = _mla_sample(x, cache_ckv, cache_kpe, cos, sin, w)
    y = _ffn(x.reshape(b * l, d), a_ssm, a_mla, w, alpha)
    return y.reshape(b, l, d), ckv, kpe, ssm.reshape(b, SSM_HEADS, SSM_HEAD_DIM, SSM_STATE), conv


def kernel(x_prompt, x_sample, cache_ckv, cache_kpe, state_ssm, state_conv, w_in, b_gate, conv_w, conv_b, dt_bias, a_log, d_skip, ssm_norm_g, w_ssm_proj, q_norm_g, w_q_b, kv_norm_g, w_uk, w_uv, w_mla_proj, w_out, ln1_g, ln1_b, w_up, w_down, ln2_g, ln2_b):
    depth = w_in.shape[0]
    alpha = (2 * depth) ** 0.25
    layer_params = (w_in, b_gate, conv_w, conv_b, dt_bias, a_log, d_skip, ssm_norm_g, w_ssm_proj, q_norm_g, w_q_b,
                    kv_norm_g, w_uk, w_uv, w_mla_proj, w_out, ln1_g, ln1_b, w_up, w_down, ln2_g, ln2_b)
    yp, ys = x_prompt, x_sample
    st_p, st_s = [], []
    for i in range(depth):
        w = _layer_weights(*(p[i] for p in layer_params))
        yp, *sp = _prompt_layer(yp, w, alpha)
        ys, *ss = _sample_layer(ys, cache_ckv[i], cache_kpe[i], state_ssm[i], state_conv[i], w, alpha)
        st_p.append(sp)
        st_s.append(ss)
    stack = lambda sts, k: jnp.stack([s[k] for s in sts])
    return (yp, ys, stack(st_p, 0), stack(st_p, 1), stack(st_p, 2), stack(st_p, 3),
            stack(st_s, 0), stack(st_s, 1), stack(st_s, 2), stack(st_s, 3))
```
